```python
import math
import jax, jax.numpy as jnp
from jax import lax
import numpy as np

D_MODEL = 2048
BATCH = 8
SEQ = 2048
DEPTH = 1

D_SSM = 1024
SSM_GROUP = 16
N_SSM_GROUPS = D_SSM // SSM_GROUP
SSM_STATE = 64
DT_MIN = 1e-3
DT_MAX = 1e-1
N_HEADS = 8
HEAD_DIM = 128
D_ATT = N_HEADS * HEAD_DIM
Q_BLOCK = 128
N_BRANCHES = 2
D_IN_PROJ = D_SSM + 3 * D_ATT + N_BRANCHES * D_MODEL
N_EXPERT_GROUPS = 4
EXPERTS_PER_GROUP = 8
N_EXPERTS = N_EXPERT_GROUPS * EXPERTS_PER_GROUP
TOP_K_IN_GROUP = 2
D_FF_EXPERT = 512
EPS = 1e-6

kernel_name = "hybrid_s5_stickbreaking_hmoe_block"


def rms_norm(x, g):
    x32 = x.astype(jnp.float32)
    y = x32 * lax.rsqrt(jnp.mean(x32 * x32, axis=-1, keepdims=True) + EPS)
    return (y * g.astype(jnp.float32)).astype(x.dtype)


def _cmul(ar, ai, br, bi):
    return ar * br - ai * bi, ar * bi + ai * br


def s5_mixer(u, lambda_re, lambda_im, log_dt, b_re, b_im, c_re, c_im, d_skip, w_glu):
    bsz, seq_len, _ = u.shape
    u32 = u.astype(jnp.float32).reshape(bsz, seq_len, N_SSM_GROUPS, SSM_GROUP)
    dt = jnp.exp(log_dt.astype(jnp.float32))[:, None]
    lr = lambda_re.astype(jnp.float32)
    li = lambda_im.astype(jnp.float32)
    mag = jnp.exp(lr * dt)
    abar_re, abar_im = mag * jnp.cos(li * dt), mag * jnp.sin(li * dt)
    nr, ni = abar_re - 1.0, abar_im
    den = lr * lr + li * li
    coef_re = (nr * lr + ni * li) / den
    coef_im = (ni * lr - nr * li) / den
    bbar_re, bbar_im = _cmul(coef_re[..., None], coef_im[..., None],
                             b_re.astype(jnp.float32), b_im.astype(jnp.float32))
    bu_re = jnp.einsum('blgh,gph->blgp', u32, bbar_re)
    bu_im = jnp.einsum('blgh,gph->blgp', u32, bbar_im)
    a_re = jnp.broadcast_to(abar_re, (1, seq_len, N_SSM_GROUPS, SSM_STATE))
    a_im = jnp.broadcast_to(abar_im, (1, seq_len, N_SSM_GROUPS, SSM_STATE))

    def combine(left, right):
        a1r, a1i, b1r, b1i = left
        a2r, a2i, b2r, b2i = right
        ar, ai = _cmul(a2r, a2i, a1r, a1i)
        br, bi = _cmul(a2r, a2i, b1r, b1i)
        return ar, ai, br + b2r, bi + b2i

    _, _, xs_re, xs_im = lax.associative_scan(combine, (a_re, a_im, bu_re, bu_im), axis=1)
    y = (jnp.einsum('blgp,ghp->blgh', xs_re, c_re.astype(jnp.float32))
         - jnp.einsum('blgp,ghp->blgh', xs_im, c_im.astype(jnp.float32))
         + d_skip.astype(jnp.float32).reshape(N_SSM_GROUPS, SSM_GROUP) * u32)
    y = y.reshape(bsz, seq_len, D_SSM)
    z = jax.nn.gelu(y)
    out = z * jax.nn.sigmoid(z @ w_glu.astype(jnp.float32))
    return out.astype(u.dtype)


def stick_breaking_attention(q, k, v):
    seq_len = q.shape[1]
    scale = HEAD_DIM ** -0.5
    outs = []
    for blk in range(seq_len // Q_BLOCK):
        q0 = blk * Q_BLOCK
        kl = q0 + Q_BLOCK
        qb, kb, vb = q[:, q0:kl], k[:, :kl], v[:, :kl]
        z = jnp.einsum('bqhd,bkhd->bhqk', qb, kb).astype(jnp.float32) * scale
        q_pos = q0 + jnp.arange(Q_BLOCK)[:, None]
        k_pos = jnp.arange(kl)[None, :]
        causal = k_pos < q_pos
        log_1m_beta = jnp.where(causal, jax.nn.log_sigmoid(-z), 0.0)
        log_tail = lax.cumsum(log_1m_beta, axis=3, reverse=True) - log_1m_beta
        weights = jnp.where(causal, jnp.exp(jax.nn.log_sigmoid(z) + log_tail), 0.0)
        outs.append(jnp.einsum('bhqk,bkhd->bqhd', weights.astype(vb.dtype), vb))
    return jnp.concatenate(outs, axis=1)


def hierarchical_moe(h, rg_w, rg_b, re_w, re_b, w_gate, w_up, w_down):
    bsz, seq_len, d = h.shape
    t = h.reshape(-1, d)
    n_tok = t.shape[0]
    g_prob = jax.nn.softmax((t @ rg_w + rg_b).astype(jnp.float32), axis=-1)
    g_top, g_idx = lax.top_k(g_prob, 1)
    e_logits = (t @ re_w + re_b).astype(jnp.float32).reshape(n_tok, N_EXPERT_GROUPS, EXPERTS_PER_GROUP)
    e_logits = e_logits[jnp.arange(n_tok), g_idx[:, 0]]
    e_prob = jax.nn.softmax(e_logits, axis=-1)
    e_top, e_idx = lax.top_k(e_prob, TOP_K_IN_GROUP)
    e_top = e_top / jnp.sum(e_top, axis=-1, keepdims=True)
    expert_id = g_idx * EXPERTS_PER_GROUP + e_idx
    gate = g_top * e_top
    combine = jnp.sum(jax.nn.one_hot(expert_id, N_EXPERTS, dtype=jnp.float32) * gate[..., None], axis=1)
    combine = combine.astype(t.dtype)
    out = jnp.zeros_like(t)
    for e in range(N_EXPERTS):
        hid = jax.nn.silu(t @ w_gate[e]) * (t @ w_up[e])
        out = out + combine[:, e:e + 1] * (hid @ w_down[e])
    return out.reshape(bsz, seq_len, d)


def setup_inputs(seed: int = 0) -> dict:
    key = jax.random.key(seed)
    ks = jax.random.split(key, 26)
    f32 = jnp.float32
    nrm = lambda k, shape, s: jax.random.normal(k, shape, f32) * s
    G, P, H = N_SSM_GROUPS, SSM_STATE, SSM_GROUP
    x = jax.random.normal(ks[0], (BATCH, SEQ, D_MODEL), f32)
    attn_norm_g = 1.0 + nrm(ks[1], (DEPTH, D_MODEL), 0.02)
    w_in = nrm(ks[2], (DEPTH, D_MODEL, D_IN_PROJ), D_MODEL ** -0.5)
    lambda_re = -0.5 + nrm(ks[3], (DEPTH, G, P), 0.01)
    lambda_im = jnp.pi * jnp.arange(P, dtype=f32)[None, None, :] + nrm(ks[4], (DEPTH, G, P), 0.01)
    log_dt = jax.random.uniform(ks[5], (DEPTH, G), f32, math.log(DT_MIN), math.log(DT_MAX))
    ssm_b_re = nrm(ks[6], (DEPTH, G, P, H), (2.0 * H) ** -0.5)
    ssm_b_im = nrm(ks[7], (DEPTH, G, P, H), (2.0 * H) ** -0.5)
    ssm_c_re = nrm(ks[8], (DEPTH, G, H, P), 0.5 ** 0.5)
    ssm_c_im = nrm(ks[9], (DEPTH, G, H, P), 0.5 ** 0.5)
    ssm_d = nrm(ks[10], (DEPTH, D_SSM), 1.0)
    w_glu = nrm(ks[11], (DEPTH, D_SSM, D_SSM), D_SSM ** -0.5)
    q_norm_g = 1.0 + nrm(ks[12], (DEPTH, HEAD_DIM), 0.02)
    k_norm_g = 1.0 + nrm(ks[13], (DEPTH, HEAD_DIM), 0.02)
    w_branch_ssm = nrm(ks[14], (DEPTH, D_SSM, D_MODEL), D_SSM ** -0.5)
    w_branch_att = nrm(ks[15], (DEPTH, D_ATT, D_MODEL), D_ATT ** -0.5)
    w_out = nrm(ks[16], (DEPTH, D_MODEL, D_MODEL), D_MODEL ** -0.5)
    ffn_norm_g = 1.0 + nrm(ks[17], (DEPTH, D_MODEL), 0.02)
    router_group_w = nrm(ks[18], (DEPTH, D_MODEL, N_EXPERT_GROUPS), D_MODEL ** -0.5)
    router_group_b = nrm(ks[19], (DEPTH, N_EXPERT_GROUPS), 0.01)
    router_expert_w = nrm(ks[20], (DEPTH, D_MODEL, N_EXPERTS), D_MODEL ** -0.5)
    router_expert_b = nrm(ks[21], (DEPTH, N_EXPERTS), 0.01)
    expert_w_gate = nrm(ks[22], (DEPTH, N_EXPERTS, D_MODEL, D_FF_EXPERT), D_MODEL ** -0.5)
    expert_w_up = nrm(ks[23], (DEPTH, N_EXPERTS, D_MODEL, D_FF_EXPERT), D_MODEL ** -0.5)
    expert_w_down = nrm(ks[24], (DEPTH, N_EXPERTS, D_FF_EXPERT, D_MODEL), D_FF_EXPERT ** -0.5)
    return {"x": x, "attn_norm_g": attn_norm_g, "w_in": w_in,
            "lambda_re": lambda_re, "lambda_im": lambda_im, "log_dt": log_dt,
            "ssm_b_re": ssm_b_re, "ssm_b_im": ssm_b_im, "ssm_c_re": ssm_c_re, "ssm_c_im": ssm_c_im,
            "ssm_d": ssm_d, "w_glu": w_glu, "q_norm_g": q_norm_g, "k_norm_g": k_norm_g,
            "w_branch_ssm": w_branch_ssm, "w_branch_att": w_branch_att, "w_out": w_out,
            "ffn_norm_g": ffn_norm_g, "router_group_w": router_group_w, "router_group_b": router_group_b,
            "router_expert_w": router_expert_w, "router_expert_b": router_expert_b,
            "expert_w_gate": expert_w_gate, "expert_w_up": expert_w_up, "expert_w_down": expert_w_down}


def reference(x, attn_norm_g, w_in, lambda_re, lambda_im, log_dt, ssm_b_re, ssm_b_im,
              ssm_c_re, ssm_c_im, ssm_d, w_glu, q_norm_g, k_norm_g, w_branch_ssm,
              w_branch_att, w_out, ffn_norm_g, router_group_w, router_group_b,
              router_expert_w, router_expert_b, expert_w_gate, expert_w_up, expert_w_down):
    bsz, seq_len, _ = x.shape
    splits = [D_SSM, D_SSM + D_ATT, D_SSM + 2 * D_ATT, D_SSM + 3 * D_ATT,
              D_SSM + 3 * D_ATT + D_MODEL]
    for l in range(DEPTH):
        h = rms_norm(x, attn_norm_g[l])
        proj = h @ w_in[l]
        u_ssm, q, k, v, gate_ssm, gate_att = jnp.split(proj, splits, axis=-1)
        y_ssm = s5_mixer(u_ssm, lambda_re[l], lambda_im[l], log_dt[l], ssm_b_re[l], ssm_b_im[l],
                         ssm_c_re[l], ssm_c_im[l], ssm_d[l], w_glu[l])
        q = rms_norm(q.reshape(bsz, seq_len, N_HEADS, HEAD_DIM), q_norm_g[l])
        k = rms_norm(k.reshape(bsz, seq_len, N_HEADS, HEAD_DIM), k_norm_g[l])
        v = v.reshape(bsz, seq_len, N_HEADS, HEAD_DIM)
        y_att = stick_breaking_attention(q, k, v).reshape(bsz, seq_len, D_ATT)
        merged = (jax.nn.sigmoid(gate_ssm) * (y_ssm @ w_branch_ssm[l])
                  + jax.nn.sigmoid(gate_att) * (y_att @ w_branch_att[l]))
        x = x + merged @ w_out[l]
        h2 = rms_norm(x, ffn_norm_g[l])
        x = x + hierarchical_moe(h2, router_group_w[l], router_group_b[l], router_expert_w[l],
                                 router_expert_b[l], expert_w_gate[l], expert_w_up[l], expert_w_down[l])
    return x
```

```python
import functools

import jax
import jax.numpy as jnp
from jax import lax
from jax.experimental import pallas as pl
from jax.experimental.pallas import tpu as pltpu

F32 = jnp.float32
BF16 = jnp.bfloat16
I32 = jnp.int32
U32 = jnp.uint32

EPS = 1e-6
N_HEADS = 8
HEAD_DIM = 128
SSM_GROUP = 16
SSM_STATE = 64
GROUPS_PER_BLOCK = 16
N_EXPERT_GROUPS = 4
EXPERTS_PER_GROUP = 8
N_EXPERTS = N_EXPERT_GROUPS * EXPERTS_PER_GROUP
ROUTE_LANES = 128
NEG_BIG = -1e30

VMEM_LIMIT = 56 * 1024 * 1024


def _cparams(sem):
    return pltpu.CompilerParams(dimension_semantics=sem, vmem_limit_bytes=VMEM_LIMIT)


def _rmsnorm_body(x_ref, g_ref, o_ref):
    x = x_ref[...]
    ms = jnp.mean(x * x, axis=-1, keepdims=True)
    o_ref[...] = (x * lax.rsqrt(ms + EPS) * g_ref[...]).astype(o_ref.dtype)


def _rmsnorm(x2, g, tm=512):
    t, d = x2.shape
    return pl.pallas_call(
        _rmsnorm_body,
        grid=(t // tm,),
        in_specs=[pl.BlockSpec((tm, d), lambda i: (i, 0)), pl.BlockSpec((1, d), lambda i: (0, 0))],
        out_specs=pl.BlockSpec((tm, d), lambda i: (i, 0)),
        out_shape=jax.ShapeDtypeStruct((t, d), BF16),
        compiler_params=_cparams(("arbitrary",)),
        name="rmsnorm",
    )(x2, g.reshape(1, d))


def _proj_body(h_ref, w_ref, g_ref, o_ref, *, mode):
    acc = jnp.dot(h_ref[...], w_ref[...], preferred_element_type=F32)
    if mode == "plain":
        o_ref[...] = acc.astype(o_ref.dtype)
    elif mode == "sigmoid":
        o_ref[...] = jax.nn.sigmoid(acc).astype(o_ref.dtype)
    else:
        n = acc.shape[1] // HEAD_DIM
        for hh in range(n):
            blk = acc[:, hh * HEAD_DIM:(hh + 1) * HEAD_DIM]
            ms = jnp.mean(blk * blk, axis=-1, keepdims=True)
            o_ref[:, hh * HEAD_DIM:(hh + 1) * HEAD_DIM] = (blk * lax.rsqrt(ms + EPS) * g_ref[...]).astype(o_ref.dtype)


def _proj(h, w, mode, out_dtype, gain=None, tm=1024, tn=1024, name="proj"):
    t, k = h.shape
    n = w.shape[1]
    if gain is None:
        gain = jnp.ones((HEAD_DIM,), F32)
    return pl.pallas_call(
        functools.partial(_proj_body, mode=mode),
        grid=(n // tn, t // tm),
        in_specs=[pl.BlockSpec((tm, k), lambda j, i: (i, 0)),
                  pl.BlockSpec((k, tn), lambda j, i: (0, j)),
                  pl.BlockSpec((1, HEAD_DIM), lambda j, i: (0, 0))],
        out_specs=pl.BlockSpec((tm, tn), lambda j, i: (i, j)),
        out_shape=jax.ShapeDtypeStruct((t, n), out_dtype),
        compiler_params=_cparams(("arbitrary", "arbitrary")),
        name=name,
    )(h, w, gain.reshape(1, HEAD_DIM).astype(F32))


def _ssm_body(u_ref, perm_ref, permt_ref, are_ref, aim_ref, wbre_ref, wbim_ref, wc_ref, d_ref, wglu_ref,
              o_ref, bure_ref, buim_ref, sre_ref, sim_ref, *, tc, lane_chunk):
    nb = u_ref.shape[0]
    ds = u_ref.shape[2]
    rows = nb * tc
    nblk = wbre_ref.shape[0]
    cb = wbre_ref.shape[1]
    sb = wbre_ref.shape[2]
    n_state = nblk * sb

    @pl.when(pl.program_id(0) == 0)
    def _():
        sre_ref[...] = jnp.zeros_like(sre_ref)
        sim_ref[...] = jnp.zeros_like(sim_ref)

    u_bm = u_ref[...].reshape(rows, ds)
    u_tm = jnp.dot(perm_ref[...], u_bm.astype(BF16), preferred_element_type=F32).astype(BF16)
    for gb in range(nblk):
        lhs = u_tm[:, gb * cb:(gb + 1) * cb]
        bure_ref[:, gb * sb:(gb + 1) * sb] = jnp.dot(lhs, wbre_ref[gb], preferred_element_type=F32)
        buim_ref[:, gb * sb:(gb + 1) * sb] = jnp.dot(lhs, wbim_ref[gb], preferred_element_type=F32)

    for lc in range(n_state // lane_chunk):
        sl = slice(lc * lane_chunk, (lc + 1) * lane_chunk)
        ar = jnp.broadcast_to(are_ref[:, sl], (nb, lane_chunk))
        ai = jnp.broadcast_to(aim_ref[:, sl], (nb, lane_chunk))

        def step(t, carry, sl=sl, ar=ar, ai=ai):
            xr, xi = carry
            r0 = pl.multiple_of(t * nb, nb)
            br = bure_ref[pl.ds(r0, nb), sl]
            bi = buim_ref[pl.ds(r0, nb), sl]
            nr = ar * xr - ai * xi + br
            ni = ar * xi + ai * xr + bi
            bure_ref[pl.ds(r0, nb), sl] = nr
            buim_ref[pl.ds(r0, nb), sl] = ni
            return nr, ni

        xr, xi = lax.fori_loop(0, tc, step, (sre_ref[:, sl], sim_ref[:, sl]))
        sre_ref[:, sl] = xr
        sim_ref[:, sl] = xi

    ys = []
    for gb in range(nblk):
        xs = jnp.concatenate([bure_ref[:, gb * sb:(gb + 1) * sb], buim_ref[:, gb * sb:(gb + 1) * sb]], axis=1)
        ys.append(jnp.dot(xs.astype(BF16), wc_ref[gb], preferred_element_type=F32))
    y_tm = jnp.concatenate(ys, axis=1)
    y_hi = y_tm.astype(BF16)
    y_lo = (y_tm - y_hi.astype(F32)).astype(BF16)
    y_bm = (jnp.dot(permt_ref[...], y_hi, preferred_element_type=F32)
            + jnp.dot(permt_ref[...], y_lo, preferred_element_type=F32))
    y = y_bm + d_ref[...] * u_bm
    z = jax.nn.gelu(y)
    gate = jax.nn.sigmoid(jnp.dot(z.astype(BF16), wglu_ref[...], preferred_element_type=F32))
    o_ref[...] = (z * gate).reshape(nb, tc, ds).astype(o_ref.dtype)


def _ssm(u, a_re, a_im, wb_re, wb_im, wc, d_skip, w_glu, tc=64, lane_chunk=512):
    nb, seq, ds = u.shape
    rows = nb * tc
    n_state = a_re.shape[1]
    r = jnp.arange(rows)
    src = (r % nb) * tc + r // nb
    perm = (src[:, None] == r[None, :]).astype(BF16)
    permt = perm.T
    const = lambda *shape: pl.BlockSpec(shape, lambda c: (0,) * len(shape))
    return pl.pallas_call(
        functools.partial(_ssm_body, tc=tc, lane_chunk=lane_chunk),
        grid=(seq // tc,),
        in_specs=[pl.BlockSpec((nb, tc, ds), lambda c: (0, c, 0)),
                  const(rows, rows), const(rows, rows),
                  const(1, n_state), const(1, n_state),
                  const(*wb_re.shape), const(*wb_im.shape), const(*wc.shape),
                  const(1, ds), const(ds, ds)],
        out_specs=pl.BlockSpec((nb, tc, ds), lambda c: (0, c, 0)),
        out_shape=jax.ShapeDtypeStruct((nb, seq, ds), BF16),
        scratch_shapes=[pltpu.VMEM((rows, n_state), F32), pltpu.VMEM((rows, n_state), F32),
                        pltpu.VMEM((nb, n_state), F32), pltpu.VMEM((nb, n_state), F32)],
        compiler_params=_cparams(("arbitrary",)),
        name="s5_mixer",
    )(u, perm, permt, a_re, a_im, wb_re, wb_im, wc, d_skip.reshape(1, ds), w_glu)


def _ssm_params(lambda_re, lambda_im, log_dt, b_re, b_im, c_re, c_im):
    g, p = lambda_re.shape
    h = b_re.shape[2]
    nblk = g // GROUPS_PER_BLOCK
    dt = jnp.exp(log_dt.astype(F32))[:, None]
    lr = lambda_re.astype(F32)
    li = lambda_im.astype(F32)
    mag = jnp.exp(lr * dt)
    abar_re, abar_im = mag * jnp.cos(li * dt), mag * jnp.sin(li * dt)
    nr, ni = abar_re - 1.0, abar_im
    den = lr * lr + li * li
    coef_re = (nr * lr + ni * li) / den
    coef_im = (ni * lr - nr * li) / den
    bbar_re = coef_re[..., None] * b_re - coef_im[..., None] * b_im
    bbar_im = coef_re[..., None] * b_im + coef_im[..., None] * b_re
    eye = jnp.eye(GROUPS_PER_BLOCK, dtype=F32)

    def pack_b(bb):
        bb = bb.reshape(nblk, GROUPS_PER_BLOCK, p, h)
        return jnp.einsum("bgph,gk->bghkp", bb, eye).reshape(nblk, GROUPS_PER_BLOCK * h, GROUPS_PER_BLOCK * p)

    def pack_c(cc):
        cc = cc.reshape(nblk, GROUPS_PER_BLOCK, h, p)
        return jnp.einsum("bghp,gk->bgpkh", cc, eye).reshape(nblk, GROUPS_PER_BLOCK * p, GROUPS_PER_BLOCK * h)

    wc = jnp.concatenate([pack_c(c_re.astype(F32)), -pack_c(c_im.astype(F32))], axis=1)
    return (abar_re.reshape(1, g * p), abar_im.reshape(1, g * p),
            pack_b(bbar_re).astype(BF16), pack_b(bbar_im).astype(BF16), wc.astype(BF16))


def _attn_body(q_ref, k_ref, v_ref, tri_ref, o_ref, *, tq):
    qi = pl.program_id(2)
    q = q_ref[0]
    tri = tri_ref[...]
    row = lax.broadcasted_iota(I32, (tq, tq), 0)
    col = lax.broadcasted_iota(I32, (tq, tq), 1)
    causal = col < row

    def block(j, acc, rsum, masked):
        k0 = pl.multiple_of(j * tq, tq)
        kb = k_ref[0, pl.ds(k0, tq), :]
        vb = v_ref[0, pl.ds(k0, tq), :]
        z = lax.dot_general(q, kb, (((1,), (1,)), ((), ())), preferred_element_type=F32)
        soft = jnp.log(1.0 + jnp.exp(-jnp.abs(z)))
        log_1m = jnp.minimum(-z, 0.0) - soft
        log_b = log_1m + z
        if masked:
            log_1m = jnp.where(causal, log_1m, 0.0)
        hi = log_1m.astype(BF16)
        lo = (log_1m - hi.astype(F32)).astype(BF16)
        tail = (jnp.dot(hi, tri, preferred_element_type=F32)
                + jnp.dot(lo, tri, preferred_element_type=F32))
        w = jnp.exp(log_b + tail + rsum)
        if masked:
            w = jnp.where(causal, w, 0.0)
        acc = acc + jnp.dot(w.astype(BF16), vb, preferred_element_type=F32)
        rsum = rsum + jnp.sum(log_1m, axis=1, keepdims=True)
        return acc, rsum

    acc0 = jnp.zeros((tq, q.shape[1]), F32)
    r0 = jnp.zeros((tq, 1), F32)
    acc, rsum = block(qi, acc0, r0, True)

    def body(it, carry):
        return block(qi - 1 - it, carry[0], carry[1], False)

    acc, rsum = lax.fori_loop(0, qi, body, (acc, rsum))
    o_ref[0] = acc.astype(o_ref.dtype)


def _attention(q, k, v, tq=256):
    nb, seq, da = q.shape
    nh = da // HEAD_DIM
    r = jnp.arange(tq)
    tri = (r[:, None] > r[None, :]).astype(BF16)
    return pl.pallas_call(
        functools.partial(_attn_body, tq=tq),
        grid=(nb, nh, seq // tq),
        in_specs=[pl.BlockSpec((1, tq, HEAD_DIM), lambda b, h, i: (b, i, h)),
                  pl.BlockSpec((1, seq, HEAD_DIM), lambda b, h, i: (b, 0, h)),
                  pl.BlockSpec((1, seq, HEAD_DIM), lambda b, h, i: (b, 0, h)),
                  pl.BlockSpec((tq, tq), lambda b, h, i: (0, 0))],
        out_specs=pl.BlockSpec((1, tq, HEAD_DIM), lambda b, h, i: (b, i, h)),
        out_shape=jax.ShapeDtypeStruct((nb, seq, da), BF16),
        compiler_params=_cparams(("arbitrary", "arbitrary", "arbitrary")),
        name="stick_attention",
    )(q, k, v, tri)


def _merge_body(ys_ref, ya_ref, gs_ref, ga_ref, ws_ref, wa_ref, o_ref):
    ps = jnp.dot(ys_ref[...], ws_ref[...], preferred_element_type=F32)
    pa = jnp.dot(ya_ref[...], wa_ref[...], preferred_element_type=F32)
    o_ref[...] = (gs_ref[...].astype(F32) * ps + ga_ref[...].astype(F32) * pa).astype(o_ref.dtype)


def _merge(y_ssm, y_att, gates, w_s, w_a, tm=1024, tn=1024):
    t, ks = y_ssm.shape
    d = w_s.shape[1]
    nj = d // tn
    return pl.pallas_call(
        _merge_body,
        grid=(nj, t // tm),
        in_specs=[pl.BlockSpec((tm, ks), lambda j, i: (i, 0)),
                  pl.BlockSpec((tm, y_att.shape[1]), lambda j, i: (i, 0)),
                  pl.BlockSpec((tm, tn), lambda j, i: (i, j)),
                  pl.BlockSpec((tm, tn), lambda j, i: (i, nj + j)),
                  pl.BlockSpec((ks, tn), lambda j, i: (0, j)),
                  pl.BlockSpec((w_a.shape[0], tn), lambda j, i: (0, j))],
        out_specs=pl.BlockSpec((tm, tn), lambda j, i: (i, j)),
        out_shape=jax.ShapeDtypeStruct((t, d), BF16),
        compiler_params=_cparams(("arbitrary", "arbitrary")),
        name="branch_merge",
    )(y_ssm, y_att, gates, gates, w_s, w_a)


def _lane_min_index(mask, lane):
    return jnp.min(jnp.where(mask, lane, float(ROUTE_LANES)), axis=1, keepdims=True)


def _outproj_body(x_ref, m_ref, w_ref, g_ref, wrh_ref, wrl_ref, br_ref,
                  x1_ref, hp_ref, rr_ref, rt_ref):
    x1 = x_ref[...] + jnp.dot(m_ref[...], w_ref[...], preferred_element_type=F32)
    x1_ref[...] = x1
    ms = jnp.mean(x1 * x1, axis=-1, keepdims=True)
    h2 = x1 * lax.rsqrt(ms + EPS) * g_ref[...]
    hb = h2.astype(BF16)
    half = h2.shape[1] // 2
    hf = hb.astype(F32)
    ua = lax.bitcast_convert_type(hf[:, :half], U32)
    ub = lax.bitcast_convert_type(hf[:, half:], U32)
    hp_ref[...] = ua | (ub >> 16)

    hl = (h2 - hf).astype(BF16)
    logits = (jnp.dot(hb, wrh_ref[...], preferred_element_type=F32)
              + jnp.dot(hl, wrh_ref[...], preferred_element_type=F32)
              + jnp.dot(hb, wrl_ref[...], preferred_element_type=F32)) + br_ref[...]
    tm = logits.shape[0]
    lane = lax.broadcasted_iota(I32, (tm, ROUTE_LANES), 1).astype(F32)
    is_group = (lane >= N_EXPERTS) & (lane < N_EXPERTS + N_EXPERT_GROUPS)
    gl = jnp.where(is_group, logits, NEG_BIG)
    gmax = jnp.max(gl, axis=1, keepdims=True)
    gidx = _lane_min_index(gl == gmax, lane) - N_EXPERTS
    g_top = 1.0 / jnp.sum(jnp.exp(gl - gmax), axis=1, keepdims=True)
    lo = gidx * EXPERTS_PER_GROUP
    in_grp = (lane >= lo) & (lane < lo + EXPERTS_PER_GROUP)
    el = jnp.where(in_grp, logits, NEG_BIG)
    m1 = jnp.max(el, axis=1, keepdims=True)
    i1 = _lane_min_index(el == m1, lane)
    el2 = jnp.where(lane == i1, NEG_BIG, el)
    m2 = jnp.max(el2, axis=1, keepdims=True)
    i2 = _lane_min_index(el2 == m2, lane)
    dlt = jnp.exp(m2 - m1)
    w1 = 1.0 / (1.0 + dlt)
    w2 = dlt * w1
    route = jnp.where(lane == 0, i1,
                      jnp.where(lane == 1, i2,
                                jnp.where(lane == 2, g_top * w1,
                                          jnp.where(lane == 3, g_top * w2, 0.0))))
    rr_ref[...] = route
    rt_ref[...] = jnp.transpose(route)[0:8, :]


def _outproj(x2, merged, w_out, g, wr_hi, wr_lo, br, tm=512):
    t, d = x2.shape
    const = lambda *shape: pl.BlockSpec(shape, lambda i: (0,) * len(shape))
    return pl.pallas_call(
        _outproj_body,
        grid=(t // tm,),
        in_specs=[pl.BlockSpec((tm, d), lambda i: (i, 0)), pl.BlockSpec((tm, d), lambda i: (i, 0)),
                  const(d, d), const(1, d), const(d, ROUTE_LANES), const(d, ROUTE_LANES), const(1, ROUTE_LANES)],
        out_specs=[pl.BlockSpec((tm, d), lambda i: (i, 0)),
                   pl.BlockSpec((tm, d // 2), lambda i: (i, 0)),
                   pl.BlockSpec((tm, ROUTE_LANES), lambda i: (i, 0)),
                   pl.BlockSpec((8, tm), lambda i: (0, i))],
        out_shape=[jax.ShapeDtypeStruct((t, d), F32),
                   jax.ShapeDtypeStruct((t, d // 2), U32),
                   jax.ShapeDtypeStruct((t, ROUTE_LANES), F32),
                   jax.ShapeDtypeStruct((8, t), F32)],
        compiler_params=_cparams(("arbitrary",)),
        name="outproj_router",
    )(x2, merged, w_out, g.reshape(1, d), wr_hi, wr_lo, br)


def _count_body(ids_ref, cnt_ref):
    nc, _, c = ids_ref.shape
    eidx = lax.broadcasted_iota(I32, (N_EXPERTS, c), 0)

    def step(i, acc):
        return acc + jnp.where(ids_ref[i] == eidx, 1.0, 0.0)

    acc = lax.fori_loop(0, nc, step, jnp.zeros((N_EXPERTS, c), F32))
    tot = jnp.sum(acc, axis=1, keepdims=True)
    cnt_ref[...] = jnp.broadcast_to(tot, cnt_ref.shape).astype(I32)


def _expert_counts(ids3):
    return pl.pallas_call(
        _count_body,
        out_shape=jax.ShapeDtypeStruct((N_EXPERTS, 128), I32),
        name="expert_counts",
    )(ids3)


def _pos_body(ids_ref, start_ref, incl_ref, pos_ref):
    nc, _, c = ids_ref.shape
    eidx = lax.broadcasted_iota(I32, (N_EXPERTS, c), 0)
    incl = incl_ref[...]

    def step(i, carry):
        onehot = ids_ref[i] == eidx
        cum = jnp.dot(jnp.where(onehot, 1.0, 0.0).astype(BF16), incl, preferred_element_type=F32)
        val = jnp.where(onehot, cum - 1.0 + carry, 0.0)
        pos_ref[i] = jnp.sum(val, axis=0, keepdims=True).astype(I32)
        return carry + cum[:, c - 1:c]

    lax.fori_loop(0, nc, step, start_ref[:, 0:1].astype(F32))


def _sorted_positions(ids3, starts):
    nc, _, c = ids3.shape
    r = jnp.arange(c)
    incl = (r[:, None] <= r[None, :]).astype(BF16)
    start_b = jnp.broadcast_to(starts.astype(I32)[:, None], (N_EXPERTS, 128))
    return pl.pallas_call(
        _pos_body,
        out_shape=jax.ShapeDtypeStruct((nc, 1, c), I32),
        name="sorted_positions",
    )(ids3, start_b, incl)


def _dispatch_body(pos_ref, src_ref, dst_ref, sem, *, chunk, n_tok):
    c = pl.program_id(0)
    nsteps = pl.num_programs(0)
    slot = c % 2
    base = c * chunk

    def issue(i, carry):
        r = base + i
        t = jnp.where(r >= n_tok, r - n_tok, r)
        pltpu.make_async_copy(src_ref.at[pl.ds(t, 1), :], dst_ref.at[pl.ds(pos_ref[r], 1), :], sem.at[slot]).start()
        return carry

    lax.fori_loop(0, chunk, issue, 0)

    def wait_chunk(s):
        pltpu.make_async_copy(src_ref.at[pl.ds(0, chunk), :], dst_ref.at[pl.ds(0, chunk), :], sem.at[s]).wait()

    @pl.when(c > 0)
    def _():
        wait_chunk(1 - slot)

    @pl.when(c == nsteps - 1)
    def _():
        wait_chunk(slot)


def _dispatch(pos_flat, hp, chunk=512):
    n = pos_flat.shape[0]
    t, w = hp.shape
    return pl.pallas_call(
        functools.partial(_dispatch_body, chunk=chunk, n_tok=t),
        grid_spec=pltpu.PrefetchScalarGridSpec(
            num_scalar_prefetch=1, grid=(n // chunk,),
            in_specs=[pl.BlockSpec(memory_space=pl.ANY)],
            out_specs=pl.BlockSpec(memory_space=pl.ANY),
            scratch_shapes=[pltpu.SemaphoreType.DMA((2,))]),
        out_shape=jax.ShapeDtypeStruct((n, w), hp.dtype),
        compiler_params=_cparams(("arbitrary",)),
        name="dispatch",
    )(pos_flat, hp)


def _gmm_body(tile_ref, exp_ref, lo_ref, hi_ref, x_ref, wg_ref, wu_ref, wd_ref, o_ref, wgs, wus, wds):
    w = pl.program_id(0)
    prev = jnp.maximum(w - 1, 0)
    new_expert = (w == 0) | (exp_ref[w] != exp_ref[prev])
    first_of_tile = (w == 0) | (tile_ref[w] != tile_ref[prev])
    lo = lo_ref[w]
    hi = hi_ref[w]

    @pl.when(new_expert)
    def _():
        wgs[...] = wg_ref[0].astype(BF16)
        wus[...] = wu_ref[0].astype(BF16)
        wds[...] = wd_ref[0].astype(BF16)

    @pl.when(hi > lo)
    def _():
        xu = x_ref[...]
        xa = lax.bitcast_convert_type(xu & jnp.uint32(0xFFFF0000), F32).astype(BF16)
        xb = lax.bitcast_convert_type(xu << 16, F32).astype(BF16)
        xx = jnp.concatenate([xa, xb], axis=1)
        g = jnp.dot(xx, wgs[...], preferred_element_type=F32)
        u = jnp.dot(xx, wus[...], preferred_element_type=F32)
        hid = (g * jax.nn.sigmoid(g)) * u
        y = jnp.dot(hid.astype(BF16), wds[...], preferred_element_type=F32)
        rows = lax.broadcasted_iota(I32, (y.shape[0], 1), 0)
        y = jnp.where((rows >= lo) & (rows < hi), y, 0.0)

        @pl.when(first_of_tile)
        def _():
            o_ref[...] = y

        @pl.when(jnp.logical_not(first_of_tile))
        def _():
            o_ref[...] += y


def _gmm(meta, xs, w_gate, w_up, w_down, tm):
    tile_id, expert_id, row_lo, row_hi = meta
    n, half = xs.shape
    e, d, f = w_gate.shape
    nw = tile_id.shape[0]
    return pl.pallas_call(
        _gmm_body,
        grid_spec=pltpu.PrefetchScalarGridSpec(
            num_scalar_prefetch=4, grid=(nw,),
            in_specs=[pl.BlockSpec((tm, half), lambda w, ti, ex, lo, hi: (ti[w], 0)),
                      pl.BlockSpec((1, d, f), lambda w, ti, ex, lo, hi: (ex[w], 0, 0)),
                      pl.BlockSpec((1, d, f), lambda w, ti, ex, lo, hi: (ex[w], 0, 0)),
                      pl.BlockSpec((1, f, d), lambda w, ti, ex, lo, hi: (ex[w], 0, 0))],
            out_specs=pl.BlockSpec((tm, d), lambda w, ti, ex, lo, hi: (ti[w], 0)),
            scratch_shapes=[pltpu.VMEM((d, f), BF16), pltpu.VMEM((d, f), BF16), pltpu.VMEM((f, d), BF16)]),
        out_shape=jax.ShapeDtypeStruct((n, d), F32),
        compiler_params=_cparams(("arbitrary",)),
        name="expert_gmm",
    )(tile_id, expert_id, row_lo, row_hi, xs, w_gate, w_up, w_down)


def _gmm_metadata(counts, n_rows, tm):
    nt = n_rows // tm
    nw = nt + N_EXPERTS - 1
    ends = jnp.cumsum(counts)
    starts = ends - counts
    first_tile = starts // tm
    n_items = jnp.where(counts > 0, (ends - 1) // tm - first_tile + 1, 0)
    item_end = jnp.cumsum(n_items)
    item_start = item_end - n_items
    total = item_end[-1]
    w = jnp.arange(nw, dtype=I32)
    wc = jnp.minimum(w, total - 1)
    ex = jnp.searchsorted(item_end, wc, side="right").astype(I32)
    tile = first_tile[ex] + (wc - item_start[ex])
    lo = jnp.maximum(starts[ex], tile * tm) - tile * tm
    hi = jnp.minimum(ends[ex], (tile + 1) * tm) - tile * tm
    valid = w < total
    lo = jnp.where(valid, lo, 0)
    hi = jnp.where(valid, hi, 0)
    return tile.astype(I32), ex, lo.astype(I32), hi.astype(I32)


def _combine_body(pos_ref, x1_ref, rr_ref, ys_ref, o_ref, buf0, buf1, sem, *, n_tok):
    i = pl.program_id(0)
    tm = x1_ref.shape[0]
    base = i * tm

    def issue(j, carry):
        pltpu.make_async_copy(ys_ref.at[pl.ds(pos_ref[base + j], 1), :], buf0.at[pl.ds(j, 1), :], sem.at[0]).start()
        pltpu.make_async_copy(ys_ref.at[pl.ds(pos_ref[n_tok + base + j], 1), :], buf1.at[pl.ds(j, 1), :], sem.at[1]).start()
        return carry

    lax.fori_loop(0, tm, issue, 0)
    pltpu.make_async_copy(ys_ref.at[pl.ds(0, tm), :], buf0, sem.at[0]).wait()
    pltpu.make_async_copy(ys_ref.at[pl.ds(0, tm), :], buf1, sem.at[1]).wait()
    rr = rr_ref[...]
    o_ref[...] = x1_ref[...] + rr[:, 2:3] * buf0[...] + rr[:, 3:4] * buf1[...]


def _combine(pos_flat, x1, route_rows, ys, tm=256):
    t, d = x1.shape
    return pl.pallas_call(
        functools.partial(_combine_body, n_tok=t),
        grid_spec=pltpu.PrefetchScalarGridSpec(
            num_scalar_prefetch=1, grid=(t // tm,),
            in_specs=[pl.BlockSpec((tm, d), lambda i, p: (i, 0)),
                      pl.BlockSpec((tm, ROUTE_LANES), lambda i, p: (i, 0)),
                      pl.BlockSpec(memory_space=pl.ANY)],
            out_specs=pl.BlockSpec((tm, d), lambda i, p: (i, 0)),
            scratch_shapes=[pltpu.VMEM((tm, d), F32), pltpu.VMEM((tm, d), F32), pltpu.SemaphoreType.DMA((2,))]),
        out_shape=jax.ShapeDtypeStruct((t, d), F32),
        compiler_params=_cparams(("arbitrary",)),
        name="combine",
    )(pos_flat, x1, route_rows, ys)


def _moe(x1, hp, route_rows, route_t, w_gate, w_up, w_down, gmm_tm=256, sort_chunk=512):
    t = x1.shape[0]
    n = 2 * t
    ids3 = route_t[0:2].astype(I32).reshape(n // sort_chunk, 1, sort_chunk)
    counts = _expert_counts(ids3)[:, 0]
    starts = jnp.cumsum(counts) - counts
    pos_flat = _sorted_positions(ids3, starts).reshape(n)
    xs = _dispatch(pos_flat, hp)
    ys = _gmm(_gmm_metadata(counts, n, gmm_tm), xs, w_gate, w_up, w_down, gmm_tm)
    return _combine(pos_flat, x1, route_rows, ys)


def _layer(x, attn_norm_g, w_in, lambda_re, lambda_im, log_dt, ssm_b_re, ssm_b_im, ssm_c_re, ssm_c_im,
           ssm_d, w_glu, q_norm_g, k_norm_g, w_branch_ssm, w_branch_att, w_out, ffn_norm_g,
           router_group_w, router_group_b, router_expert_w, router_expert_b,
           expert_w_gate, expert_w_up, expert_w_down):
    nb, seq, d = x.shape
    t = nb * seq
    ds = w_glu.shape[0]
    da = N_HEADS * HEAD_DIM
    x2 = x.reshape(t, d)

    h = _rmsnorm(x2, attn_norm_g)
    wb = w_in.astype(BF16)
    o = 0
    u = _proj(h, wb[:, o:o + ds], "plain", F32, name="proj_u"); o += ds
    qg = q_norm_g.astype(F32) * (HEAD_DIM ** -0.5)
    q = _proj(h, wb[:, o:o + da], "headnorm", BF16, gain=qg, name="proj_q"); o += da
    k = _proj(h, wb[:, o:o + da], "headnorm", BF16, gain=k_norm_g, name="proj_k"); o += da
    v = _proj(h, wb[:, o:o + da], "plain", BF16, name="proj_v"); o += da
    gates = _proj(h, wb[:, o:], "sigmoid", BF16, name="proj_gates")

    a_re, a_im, wb_re, wb_im, wc = _ssm_params(lambda_re, lambda_im, log_dt, ssm_b_re, ssm_b_im, ssm_c_re, ssm_c_im)
    y_ssm = _ssm(u.reshape(nb, seq, ds), a_re, a_im, wb_re, wb_im, wc, ssm_d.astype(F32), w_glu.astype(BF16))
    y_att = _attention(q.reshape(nb, seq, da), k.reshape(nb, seq, da), v.reshape(nb, seq, da))

    merged = _merge(y_ssm.reshape(t, ds), y_att.reshape(t, da), gates,
                    w_branch_ssm.astype(BF16), w_branch_att.astype(BF16))

    wr = jnp.zeros((d, ROUTE_LANES), F32)
    wr = wr.at[:, :N_EXPERTS].set(router_expert_w.astype(F32))
    wr = wr.at[:, N_EXPERTS:N_EXPERTS + N_EXPERT_GROUPS].set(router_group_w.astype(F32))
    br = jnp.zeros((1, ROUTE_LANES), F32)
    br = br.at[0, :N_EXPERTS].set(router_expert_b.astype(F32))
    br = br.at[0, N_EXPERTS:N_EXPERTS + N_EXPERT_GROUPS].set(router_group_b.astype(F32))
    wr_hi = wr.astype(BF16)
    wr_lo = (wr - wr_hi.astype(F32)).astype(BF16)
    x1, hp, route_rows, route_t = _outproj(x2, merged, w_out.astype(BF16), ffn_norm_g, wr_hi, wr_lo, br)

    out = _moe(x1, hp, route_rows, route_t, expert_w_gate, expert_w_up, expert_w_down)
    return out.reshape(nb, seq, d)


def kernel(x, attn_norm_g, w_in, lambda_re, lambda_im, log_dt, ssm_b_re, ssm_b_im, ssm_c_re, ssm_c_im, ssm_d, w_glu, q_norm_g, k_norm_g, w_branch_ssm, w_branch_att, w_out, ffn_norm_g, router_group_w, router_group_b, router_expert_w, router_expert_b, expert_w_gate, expert_w_up, expert_w_down):
    depth = attn_norm_g.shape[0]
    for l in range(depth):
        x = _layer(x, attn_norm_g[l], w_in[l], lambda_re[l], lambda_im[l], log_dt[l], ssm_b_re[l], ssm_b_im[l],
                   ssm_c_re[l], ssm_c_im[l], ssm_d[l], w_glu[l], q_norm_g[l], k_norm_g[l], w_branch_ssm[l],
                   w_branch_att[l], w_out[l], ffn_norm_g[l], router_group_w[l], router_group_b[l],
                   router_expert_w[l], router_expert_b[l], expert_w_gate[l], expert_w_up[l], expert_w_down[l])
    return x
```

```python
import functools

import jax
import jax.numpy as jnp
from jax import lax
from jax.experimental import pallas as pl
from jax.experimental.pallas import tpu as pltpu

F32 = jnp.float32
BF16 = jnp.bfloat16
I32 = jnp.int32
U32 = jnp.uint32

EPS = 1e-6
N_HEADS = 8
HEAD_DIM = 128
SSM_GROUP = 16
SSM_STATE = 64
GROUPS_PER_BLOCK = 16
N_EXPERT_GROUPS = 4
EXPERTS_PER_GROUP = 8
N_EXPERTS = N_EXPERT_GROUPS * EXPERTS_PER_GROUP
LANES = 128
ROUTE_LANES = 128
NEG_BIG = -1e30
DMA_UNROLL = 8
EXP_ZERO_BELOW = -104.0
LOG2_E = 1.4426950408889634

VMEM_LIMIT = 56 * 1024 * 1024


def _cparams(sem):
    return pltpu.CompilerParams(dimension_semantics=sem, vmem_limit_bytes=VMEM_LIMIT)


def _rmsnorm_body(x_ref, g_ref, o_ref):
    x = x_ref[...]
    ms = jnp.mean(x * x, axis=-1, keepdims=True)
    o_ref[...] = (x * lax.rsqrt(ms + EPS) * g_ref[...]).astype(o_ref.dtype)


def _rmsnorm(x2, g, tm=512):
    t, d = x2.shape
    return pl.pallas_call(
        _rmsnorm_body,
        grid=(t // tm,),
        in_specs=[pl.BlockSpec((tm, d), lambda i: (i, 0)), pl.BlockSpec((1, d), lambda i: (0, 0))],
        out_specs=pl.BlockSpec((tm, d), lambda i: (i, 0)),
        out_shape=jax.ShapeDtypeStruct((t, d), BF16),
        compiler_params=_cparams(("arbitrary",)),
        name="rmsnorm",
    )(x2, g.reshape(1, d))


def _proj_body(h_ref, w_ref, g_ref, o_ref, *, mode):
    acc = jnp.dot(h_ref[...], w_ref[...], preferred_element_type=F32)
    if mode == "plain":
        o_ref[...] = acc.astype(o_ref.dtype)
    elif mode == "sigmoid":
        o_ref[...] = jax.nn.sigmoid(acc).astype(o_ref.dtype)
    else:
        n = acc.shape[1] // HEAD_DIM
        for hh in range(n):
            blk = acc[:, hh * HEAD_DIM:(hh + 1) * HEAD_DIM]
            ms = jnp.mean(blk * blk, axis=-1, keepdims=True)
            o_ref[:, hh * HEAD_DIM:(hh + 1) * HEAD_DIM] = (blk * lax.rsqrt(ms + EPS) * g_ref[...]).astype(o_ref.dtype)


def _proj(h, w, mode, out_dtype, gain=None, tm=1024, tn=1024, name="proj"):
    t, k = h.shape
    n = w.shape[1]
    if gain is None:
        gain = jnp.ones((HEAD_DIM,), F32)
    return pl.pallas_call(
        functools.partial(_proj_body, mode=mode),
        grid=(n // tn, t // tm),
        in_specs=[pl.BlockSpec((tm, k), lambda j, i: (i, 0)),
                  pl.BlockSpec((k, tn), lambda j, i: (0, j)),
                  pl.BlockSpec((1, HEAD_DIM), lambda j, i: (0, 0))],
        out_specs=pl.BlockSpec((tm, tn), lambda j, i: (i, j)),
        out_shape=jax.ShapeDtypeStruct((t, n), out_dtype),
        compiler_params=_cparams(("arbitrary", "arbitrary")),
        name=name,
    )(h, w, gain.reshape(1, HEAD_DIM).astype(F32))


def _ssm_body(u_ref, perm_ref, permt_ref, are_ref, aim_ref, wbre_ref, wbim_ref, wc_ref, d_ref, wglu_ref,
              o_ref, bure_ref, buim_ref, sre_ref, sim_ref, *, tc, lane_chunk):
    nb = u_ref.shape[0]
    ds = u_ref.shape[2]
    rows = nb * tc
    nblk = wbre_ref.shape[0]
    cb = wbre_ref.shape[1]
    sb = wbre_ref.shape[2]
    n_state = nblk * sb

    @pl.when(pl.program_id(0) == 0)
    def _():
        sre_ref[...] = jnp.zeros_like(sre_ref)
        sim_ref[...] = jnp.zeros_like(sim_ref)

    u_bm = u_ref[...].reshape(rows, ds)
    u_tm = jnp.dot(perm_ref[...], u_bm.astype(BF16), preferred_element_type=F32).astype(BF16)
    for gb in range(nblk):
        lhs = u_tm[:, gb * cb:(gb + 1) * cb]
        bure_ref[:, gb * sb:(gb + 1) * sb] = jnp.dot(lhs, wbre_ref[gb], preferred_element_type=F32)
        buim_ref[:, gb * sb:(gb + 1) * sb] = jnp.dot(lhs, wbim_ref[gb], preferred_element_type=F32)

    for lc in range(n_state // lane_chunk):
        sl = slice(lc * lane_chunk, (lc + 1) * lane_chunk)
        ar = jnp.broadcast_to(are_ref[:, sl], (nb, lane_chunk))
        ai = jnp.broadcast_to(aim_ref[:, sl], (nb, lane_chunk))

        def step(t, carry, sl=sl, ar=ar, ai=ai):
            xr, xi = carry
            r0 = pl.multiple_of(t * nb, nb)
            br = bure_ref[pl.ds(r0, nb), sl]
            bi = buim_ref[pl.ds(r0, nb), sl]
            nr = ar * xr - ai * xi + br
            ni = ar * xi + ai * xr + bi
            bure_ref[pl.ds(r0, nb), sl] = nr
            buim_ref[pl.ds(r0, nb), sl] = ni
            return nr, ni

        xr, xi = lax.fori_loop(0, tc, step, (sre_ref[:, sl], sim_ref[:, sl]))
        sre_ref[:, sl] = xr
        sim_ref[:, sl] = xi

    ys = []
    for gb in range(nblk):
        xs = jnp.concatenate([bure_ref[:, gb * sb:(gb + 1) * sb], buim_ref[:, gb * sb:(gb + 1) * sb]], axis=1)
        ys.append(jnp.dot(xs.astype(BF16), wc_ref[gb], preferred_element_type=F32))
    y_tm = jnp.concatenate(ys, axis=1)
    y_hi = y_tm.astype(BF16)
    y_lo = (y_tm - y_hi.astype(F32)).astype(BF16)
    y_bm = (jnp.dot(permt_ref[...], y_hi, preferred_element_type=F32)
            + jnp.dot(permt_ref[...], y_lo, preferred_element_type=F32))
    y = y_bm + d_ref[...] * u_bm
    z = jax.nn.gelu(y)
    gate = jax.nn.sigmoid(jnp.dot(z.astype(BF16), wglu_ref[...], preferred_element_type=F32))
    o_ref[...] = (z * gate).reshape(nb, tc, ds).astype(o_ref.dtype)


def _ssm(u, a_re, a_im, wb_re, wb_im, wc, d_skip, w_glu, tc=64, lane_chunk=512):
    nb, seq, ds = u.shape
    rows = nb * tc
    n_state = a_re.shape[1]
    r = jnp.arange(rows)
    src = (r % nb) * tc + r // nb
    perm = (src[:, None] == r[None, :]).astype(BF16)
    permt = perm.T
    const = lambda *shape: pl.BlockSpec(shape, lambda c: (0,) * len(shape))
    return pl.pallas_call(
        functools.partial(_ssm_body, tc=tc, lane_chunk=lane_chunk),
        grid=(seq // tc,),
        in_specs=[pl.BlockSpec((nb, tc, ds), lambda c: (0, c, 0)),
                  const(rows, rows), const(rows, rows),
                  const(1, n_state), const(1, n_state),
                  const(*wb_re.shape), const(*wb_im.shape), const(*wc.shape),
                  const(1, ds), const(ds, ds)],
        out_specs=pl.BlockSpec((nb, tc, ds), lambda c: (0, c, 0)),
        out_shape=jax.ShapeDtypeStruct((nb, seq, ds), BF16),
        scratch_shapes=[pltpu.VMEM((rows, n_state), F32), pltpu.VMEM((rows, n_state), F32),
                        pltpu.VMEM((nb, n_state), F32), pltpu.VMEM((nb, n_state), F32)],
        compiler_params=_cparams(("arbitrary",)),
        name="s5_mixer",
    )(u, perm, permt, a_re, a_im, wb_re, wb_im, wc, d_skip.reshape(1, ds), w_glu)


def _ssm_params(lambda_re, lambda_im, log_dt, b_re, b_im, c_re, c_im):
    g, p = lambda_re.shape
    h = b_re.shape[2]
    nblk = g // GROUPS_PER_BLOCK
    dt = jnp.exp(log_dt.astype(F32))[:, None]
    lr = lambda_re.astype(F32)
    li = lambda_im.astype(F32)
    mag = jnp.exp(lr * dt)
    abar_re, abar_im = mag * jnp.cos(li * dt), mag * jnp.sin(li * dt)
    nr, ni = abar_re - 1.0, abar_im
    den = lr * lr + li * li
    coef_re = (nr * lr + ni * li) / den
    coef_im = (ni * lr - nr * li) / den
    bbar_re = coef_re[..., None] * b_re - coef_im[..., None] * b_im
    bbar_im = coef_re[..., None] * b_im + coef_im[..., None] * b_re
    eye = jnp.eye(GROUPS_PER_BLOCK, dtype=F32)

    def pack_b(bb):
        bb = bb.reshape(nblk, GROUPS_PER_BLOCK, p, h)
        return jnp.einsum("bgph,gk->bghkp", bb, eye).reshape(nblk, GROUPS_PER_BLOCK * h, GROUPS_PER_BLOCK * p)

    def pack_c(cc):
        cc = cc.reshape(nblk, GROUPS_PER_BLOCK, h, p)
        return jnp.einsum("bghp,gk->bgpkh", cc, eye).reshape(nblk, GROUPS_PER_BLOCK * p, GROUPS_PER_BLOCK * h)

    wc = jnp.concatenate([pack_c(c_re.astype(F32)), -pack_c(c_im.astype(F32))], axis=1)
    return (abar_re.reshape(1, g * p), abar_im.reshape(1, g * p),
            pack_b(bbar_re).astype(BF16), pack_b(bbar_im).astype(BF16), wc.astype(BF16))


def _attn_body(q_ref, k_ref, v_ref, tri_ref, o_ref, acc_ref, rs_ref, *, tq, nhs):
    qi = pl.program_id(2)
    tri = tri_ref[...]
    row = lax.broadcasted_iota(I32, (nhs * tq, tq), 0) % tq
    col = lax.broadcasted_iota(I32, (nhs * tq, tq), 1)
    causal = col < row

    def sweep(j, masked):
        k0 = pl.multiple_of(j * tq, tq)
        heads = [slice(h * HEAD_DIM, (h + 1) * HEAD_DIM) for h in range(nhs)]
        z = jnp.concatenate(
            [lax.dot_general(q_ref[0, :, hs], k_ref[0, pl.ds(k0, tq), hs], (((1,), (1,)), ((), ())),
                             preferred_element_type=F32) for hs in heads], axis=0)
        sp = jnp.maximum(z, 0.0) + jnp.log(1.0 + jnp.exp2(-jnp.abs(z))) * LOG2_E
        spm = jnp.where(causal, sp, 0.0) if masked else sp
        sums = jnp.dot(spm.astype(BF16), tri, preferred_element_type=F32)
        rsum = rs_ref[...]
        w = jnp.exp2(z - (sp + sums[:, :tq] + jnp.concatenate([rsum] * (tq // LANES), axis=1)))
        if masked:
            w = jnp.where(causal, w, 0.0)
        wb = w.astype(BF16)
        for h, hs in enumerate(heads):
            acc_ref[h] += jnp.dot(wb[h * tq:(h + 1) * tq], v_ref[0, pl.ds(k0, tq), hs], preferred_element_type=F32)
        rsum = rsum + sums[:, tq:]
        rs_ref[...] = rsum
        return (jnp.min(rsum) < -EXP_ZERO_BELOW * LOG2_E).astype(I32)

    acc_ref[...] = jnp.zeros_like(acc_ref)
    rs_ref[...] = jnp.zeros_like(rs_ref)
    live = sweep(qi, True)

    def cond(c):
        return (c[0] >= 0) & (c[1] > 0)

    def body(c):
        return c[0] - 1, sweep(c[0], False)

    lax.while_loop(cond, body, (qi - 1, live))
    for h in range(nhs):
        o_ref[0, :, h * HEAD_DIM:(h + 1) * HEAD_DIM] = acc_ref[h].astype(o_ref.dtype)


def _attention(q, k, v, tq=256, nhs=4):
    nb, seq, da = q.shape
    nh = da // HEAD_DIM
    wd = nhs * HEAD_DIM
    r = jnp.arange(tq)
    tri = jnp.concatenate([(r[:, None] > r[None, :]).astype(BF16), jnp.ones((tq, LANES), BF16)], axis=1)
    return pl.pallas_call(
        functools.partial(_attn_body, tq=tq, nhs=nhs),
        grid=(nb, nh // nhs, seq // tq),
        in_specs=[pl.BlockSpec((1, tq, wd), lambda b, h, i: (b, i, h)),
                  pl.BlockSpec((1, seq, wd), lambda b, h, i: (b, 0, h)),
                  pl.BlockSpec((1, seq, wd), lambda b, h, i: (b, 0, h)),
                  pl.BlockSpec((tq, tq + LANES), lambda b, h, i: (0, 0))],
        out_specs=pl.BlockSpec((1, tq, wd), lambda b, h, i: (b, i, h)),
        out_shape=jax.ShapeDtypeStruct((nb, seq, da), BF16),
        scratch_shapes=[pltpu.VMEM((nhs, tq, HEAD_DIM), F32), pltpu.VMEM((nhs * tq, LANES), F32)],
        compiler_params=_cparams(("arbitrary", "arbitrary", "arbitrary")),
        name="stick_attention",
    )(q, k, v, tri)


def _merge_body(ys_ref, ya_ref, gs_ref, ga_ref, ws_ref, wa_ref, o_ref):
    ps = jnp.dot(ys_ref[...], ws_ref[...], preferred_element_type=F32)
    pa = jnp.dot(ya_ref[...], wa_ref[...], preferred_element_type=F32)
    o_ref[...] = (gs_ref[...].astype(F32) * ps + ga_ref[...].astype(F32) * pa).astype(o_ref.dtype)


def _merge(y_ssm, y_att, gates, w_s, w_a, tm=1024, tn=1024):
    t, ks = y_ssm.shape
    d = w_s.shape[1]
    nj = d // tn
    return pl.pallas_call(
        _merge_body,
        grid=(nj, t // tm),
        in_specs=[pl.BlockSpec((tm, ks), lambda j, i: (i, 0)),
                  pl.BlockSpec((tm, y_att.shape[1]), lambda j, i: (i, 0)),
                  pl.BlockSpec((tm, tn), lambda j, i: (i, j)),
                  pl.BlockSpec((tm, tn), lambda j, i: (i, nj + j)),
                  pl.BlockSpec((ks, tn), lambda j, i: (0, j)),
                  pl.BlockSpec((w_a.shape[0], tn), lambda j, i: (0, j))],
        out_specs=pl.BlockSpec((tm, tn), lambda j, i: (i, j)),
        out_shape=jax.ShapeDtypeStruct((t, d), BF16),
        compiler_params=_cparams(("arbitrary", "arbitrary")),
        name="branch_merge",
    )(y_ssm, y_att, gates, gates, w_s, w_a)


def _lane_min_index(mask, lane):
    return jnp.min(jnp.where(mask, lane, float(ROUTE_LANES)), axis=1, keepdims=True)


def _outproj_body(x_ref, m_ref, w_ref, g_ref, wrh_ref, wrl_ref, br_ref,
                  x1_ref, hp_ref, rr_ref, rt_ref):
    x1 = x_ref[...] + jnp.dot(m_ref[...], w_ref[...], preferred_element_type=F32)
    x1_ref[...] = x1
    ms = jnp.mean(x1 * x1, axis=-1, keepdims=True)
    h2 = x1 * lax.rsqrt(ms + EPS) * g_ref[...]
    hb = h2.astype(BF16)
    half = h2.shape[1] // 2
    hf = hb.astype(F32)
    ua = lax.bitcast_convert_type(hf[:, :half], U32)
    ub = lax.bitcast_convert_type(hf[:, half:], U32)
    packed = ua | (ub >> 16)
    for s in range(hp_ref.shape[1]):
        hp_ref[:, s, :] = packed[:, s * LANES:(s + 1) * LANES]

    hl = (h2 - hf).astype(BF16)
    logits = (jnp.dot(hb, wrh_ref[...], preferred_element_type=F32)
              + jnp.dot(hl, wrh_ref[...], preferred_element_type=F32)
              + jnp.dot(hb, wrl_ref[...], preferred_element_type=F32)) + br_ref[...]
    tm = logits.shape[0]
    lane = lax.broadcasted_iota(I32, (tm, ROUTE_LANES), 1).astype(F32)
    is_group = (lane >= N_EXPERTS) & (lane < N_EXPERTS + N_EXPERT_GROUPS)
    gl = jnp.where(is_group, logits, NEG_BIG)
    gmax = jnp.max(gl, axis=1, keepdims=True)
    gidx = _lane_min_index(gl == gmax, lane) - N_EXPERTS
    g_top = 1.0 / jnp.sum(jnp.exp(gl - gmax), axis=1, keepdims=True)
    lo = gidx * EXPERTS_PER_GROUP
    in_grp = (lane >= lo) & (lane < lo + EXPERTS_PER_GROUP)
    el = jnp.where(in_grp, logits, NEG_BIG)
    m1 = jnp.max(el, axis=1, keepdims=True)
    i1 = _lane_min_index(el == m1, lane)
    el2 = jnp.where(lane == i1, NEG_BIG, el)
    m2 = jnp.max(el2, axis=1, keepdims=True)
    i2 = _lane_min_index(el2 == m2, lane)
    dlt = jnp.exp(m2 - m1)
    w1 = 1.0 / (1.0 + dlt)
    w2 = dlt * w1
    route = jnp.where(lane == 0, i1,
                      jnp.where(lane == 1, i2,
                                jnp.where(lane == 2, g_top * w1,
                                          jnp.where(lane == 3, g_top * w2, 0.0))))
    rr_ref[...] = route
    rt_ref[...] = jnp.transpose(route)[0:8, :]


def _outproj(x2, merged, w_out, g, wr_hi, wr_lo, br, tm=512):
    t, d = x2.shape
    const = lambda *shape: pl.BlockSpec(shape, lambda i: (0,) * len(shape))
    return pl.pallas_call(
        _outproj_body,
        grid=(t // tm,),
        in_specs=[pl.BlockSpec((tm, d), lambda i: (i, 0)), pl.BlockSpec((tm, d), lambda i: (i, 0)),
                  const(d, d), const(1, d), const(d, ROUTE_LANES), const(d, ROUTE_LANES), const(1, ROUTE_LANES)],
        out_specs=[pl.BlockSpec((tm, d), lambda i: (i, 0)),
                   pl.BlockSpec((tm, d // 2 // LANES, LANES), lambda i: (i, 0, 0)),
                   pl.BlockSpec((tm, ROUTE_LANES), lambda i: (i, 0)),
                   pl.BlockSpec((8, tm), lambda i: (0, i))],
        out_shape=[jax.ShapeDtypeStruct((t, d), F32),
                   jax.ShapeDtypeStruct((t, d // 2 // LANES, LANES), U32),
                   jax.ShapeDtypeStruct((t, ROUTE_LANES), F32),
                   jax.ShapeDtypeStruct((8, t), F32)],
        compiler_params=_cparams(("arbitrary",)),
        name="outproj_router",
    )(x2, merged, w_out, g.reshape(1, d), wr_hi, wr_lo, br)


def _count_body(ids_ref, cnt_ref):
    nc, _, c = ids_ref.shape
    eidx = lax.broadcasted_iota(I32, (N_EXPERTS, c), 0)

    def step(i, acc):
        return acc + jnp.where(ids_ref[i] == eidx, 1.0, 0.0)

    acc = lax.fori_loop(0, nc, step, jnp.zeros((N_EXPERTS, c), F32))
    tot = jnp.sum(acc, axis=1, keepdims=True)
    cnt_ref[...] = jnp.broadcast_to(tot, cnt_ref.shape).astype(I32)


def _expert_counts(ids3):
    return pl.pallas_call(
        _count_body,
        out_shape=jax.ShapeDtypeStruct((N_EXPERTS, 128), I32),
        name="expert_counts",
    )(ids3)


def _pos_body(ids_ref, start_ref, incl_ref, pos_ref):
    nc, _, c = ids_ref.shape
    eidx = lax.broadcasted_iota(I32, (N_EXPERTS, c), 0)
    incl = incl_ref[...]

    def step(i, carry):
        onehot = ids_ref[i] == eidx
        cum = jnp.dot(jnp.where(onehot, 1.0, 0.0).astype(BF16), incl, preferred_element_type=F32)
        val = jnp.where(onehot, cum - 1.0 + carry, 0.0)
        pos_ref[i] = jnp.sum(val, axis=0, keepdims=True).astype(I32)
        return carry + cum[:, c - 1:c]

    lax.fori_loop(0, nc, step, start_ref[:, 0:1].astype(F32))


def _sorted_positions(ids3, starts):
    nc, _, c = ids3.shape
    r = jnp.arange(c)
    incl = (r[:, None] <= r[None, :]).astype(BF16)
    start_b = jnp.broadcast_to(starts.astype(I32)[:, None], (N_EXPERTS, 128))
    return pl.pallas_call(
        _pos_body,
        out_shape=jax.ShapeDtypeStruct((nc, 1, c), I32),
        name="sorted_positions",
    )(ids3, start_b, incl)


def _dispatch_body(pos_ref, src_ref, dst_ref, sem, *, chunk, n_tok):
    c = pl.program_id(0)
    nsteps = pl.num_programs(0)
    slot = c % 2
    base = c * chunk
    tok0 = jnp.where(base >= n_tok, base - n_tok, base)

    def issue(i, carry):
        for k in range(DMA_UNROLL):
            j = i * DMA_UNROLL + k
            pltpu.make_async_copy(src_ref.at[tok0 + j], dst_ref.at[pos_ref[base + j]], sem.at[slot]).start()
        return carry

    lax.fori_loop(0, chunk // DMA_UNROLL, issue, 0)

    def wait_chunk(s):
        pltpu.make_async_copy(src_ref.at[pl.ds(0, chunk)], dst_ref.at[pl.ds(0, chunk)], sem.at[s]).wait()

    @pl.when(c > 0)
    def _():
        wait_chunk(1 - slot)

    @pl.when(c == nsteps - 1)
    def _():
        wait_chunk(slot)


def _dispatch(pos_flat, hp, chunk=512):
    n = pos_flat.shape[0]
    t = hp.shape[0]
    assert t % chunk == 0 and chunk % DMA_UNROLL == 0
    return pl.pallas_call(
        functools.partial(_dispatch_body, chunk=chunk, n_tok=t),
        grid_spec=pltpu.PrefetchScalarGridSpec(
            num_scalar_prefetch=1, grid=(n // chunk,),
            in_specs=[pl.BlockSpec(memory_space=pl.ANY)],
            out_specs=pl.BlockSpec(memory_space=pl.ANY),
            scratch_shapes=[pltpu.SemaphoreType.DMA((2,))]),
        out_shape=jax.ShapeDtypeStruct((n,) + hp.shape[1:], hp.dtype),
        compiler_params=_cparams(("arbitrary",)),
        name="dispatch",
    )(pos_flat, hp)


def _gmm_body(tile_ref, exp_ref, lo_ref, hi_ref, x_ref, wg_ref, wu_ref, wd_ref, o_ref, wgs, wus, wds):
    w = pl.program_id(0)
    prev = jnp.maximum(w - 1, 0)
    new_expert = (w == 0) | (exp_ref[w] != exp_ref[prev])
    first_of_tile = (w == 0) | (tile_ref[w] != tile_ref[prev])
    lo = lo_ref[w]
    hi = hi_ref[w]

    @pl.when(new_expert)
    def _():
        wgs[...] = wg_ref[0].astype(BF16)
        wus[...] = wu_ref[0].astype(BF16)
        wds[...] = wd_ref[0].astype(BF16)

    @pl.when(hi > lo)
    def _():
        his, los = [], []
        for s in range(x_ref.shape[1]):
            xu = x_ref[:, s, :]
            his.append(lax.bitcast_convert_type(xu & jnp.uint32(0xFFFF0000), F32).astype(BF16))
            los.append(lax.bitcast_convert_type(xu << 16, F32).astype(BF16))
        xx = jnp.concatenate(his + los, axis=1)
        g = jnp.dot(xx, wgs[...], preferred_element_type=F32)
        u = jnp.dot(xx, wus[...], preferred_element_type=F32)
        hid = (g * jax.nn.sigmoid(g)) * u
        y = jnp.dot(hid.astype(BF16), wds[...], preferred_element_type=F32)
        rows = lax.broadcasted_iota(I32, (y.shape[0], 1), 0)
        y = jnp.where((rows >= lo) & (rows < hi), y, 0.0)

        @pl.when(first_of_tile)
        def _():
            for s in range(o_ref.shape[1]):
                o_ref[:, s, :] = y[:, s * LANES:(s + 1) * LANES]

        @pl.when(jnp.logical_not(first_of_tile))
        def _():
            for s in range(o_ref.shape[1]):
                o_ref[:, s, :] += y[:, s * LANES:(s + 1) * LANES]


def _gmm(meta, xs, w_gate, w_up, w_down, tm):
    tile_id, expert_id, row_lo, row_hi = meta
    n, xsub, _ = xs.shape
    e, d, f = w_gate.shape
    nw = tile_id.shape[0]
    ysub = d // LANES
    return pl.pallas_call(
        _gmm_body,
        grid_spec=pltpu.PrefetchScalarGridSpec(
            num_scalar_prefetch=4, grid=(nw,),
            in_specs=[pl.BlockSpec((tm, xsub, LANES), lambda w, ti, ex, lo, hi: (ti[w], 0, 0)),
                      pl.BlockSpec((1, d, f), lambda w, ti, ex, lo, hi: (ex[w], 0, 0)),
                      pl.BlockSpec((1, d, f), lambda w, ti, ex, lo, hi: (ex[w], 0, 0)),
                      pl.BlockSpec((1, f, d), lambda w, ti, ex, lo, hi: (ex[w], 0, 0))],
            out_specs=pl.BlockSpec((tm, ysub, LANES), lambda w, ti, ex, lo, hi: (ti[w], 0, 0)),
            scratch_shapes=[pltpu.VMEM((d, f), BF16), pltpu.VMEM((d, f), BF16), pltpu.VMEM((f, d), BF16)]),
        out_shape=jax.ShapeDtypeStruct((n, ysub, LANES), F32),
        compiler_params=_cparams(("arbitrary",)),
        name="expert_gmm",
    )(tile_id, expert_id, row_lo, row_hi, xs, w_gate, w_up, w_down)


def _gmm_metadata(counts, n_rows, tm):
    nt = n_rows // tm
    nw = nt + N_EXPERTS - 1
    ends = jnp.cumsum(counts)
    starts = ends - counts
    first_tile = starts // tm
    n_items = jnp.where(counts > 0, (ends - 1) // tm - first_tile + 1, 0)
    item_end = jnp.cumsum(n_items)
    item_start = item_end - n_items
    total = item_end[-1]
    w = jnp.arange(nw, dtype=I32)
    wc = jnp.minimum(w, total - 1)
    ex = jnp.sum((item_end[None, :] <= wc[:, None]).astype(I32), axis=1)
    tile = first_tile[ex] + (wc - item_start[ex])
    lo = jnp.maximum(starts[ex], tile * tm) - tile * tm
    hi = jnp.minimum(ends[ex], (tile + 1) * tm) - tile * tm
    valid = w < total
    lo = jnp.where(valid, lo, 0)
    hi = jnp.where(valid, hi, 0)
    return tile.astype(I32), ex, lo.astype(I32), hi.astype(I32)


def _combine_body(pos_ref, x1_ref, rr_ref, ys_ref, o_ref, a0, a1, b0, b1, sem, *, n_tok):
    i = pl.program_id(0)
    nsteps = pl.num_programs(0)
    tm = x1_ref.shape[0]

    def gather(tile, bufs, sems):
        base = tile * tm

        def issue(jj, carry):
            for k in range(DMA_UNROLL):
                j = jj * DMA_UNROLL + k
                pltpu.make_async_copy(ys_ref.at[pos_ref[base + j]], bufs[0].at[j], sem.at[sems[0]]).start()
                pltpu.make_async_copy(ys_ref.at[pos_ref[n_tok + base + j]], bufs[1].at[j], sem.at[sems[1]]).start()
            return carry

        lax.fori_loop(0, tm // DMA_UNROLL, issue, 0)

    def step(cur, cur_sems, nxt, nxt_sems):
        @pl.when(i == 0)
        def _():
            gather(i, cur, cur_sems)

        @pl.when(i + 1 < nsteps)
        def _():
            gather(i + 1, nxt, nxt_sems)

        for buf, s in zip(cur, cur_sems):
            pltpu.make_async_copy(ys_ref.at[pl.ds(0, tm)], buf, sem.at[s]).wait()
        rr = rr_ref[...]
        g0 = rr[:, 2:3]
        g1 = rr[:, 3:4]
        for s in range(cur[0].shape[1]):
            sl = slice(s * LANES, (s + 1) * LANES)
            o_ref[:, sl] = x1_ref[:, sl] + g0 * cur[0][:, s, :] + g1 * cur[1][:, s, :]

    @pl.when(i % 2 == 0)
    def _():
        step((a0, a1), (0, 1), (b0, b1), (2, 3))

    @pl.when(i % 2 == 1)
    def _():
        step((b0, b1), (2, 3), (a0, a1), (0, 1))


def _combine(pos_flat, x1, route_rows, ys, tm=256):
    t, d = x1.shape
    assert tm % DMA_UNROLL == 0
    buf = pltpu.VMEM((tm,) + ys.shape[1:], F32)
    return pl.pallas_call(
        functools.partial(_combine_body, n_tok=t),
        grid_spec=pltpu.PrefetchScalarGridSpec(
            num_scalar_prefetch=1, grid=(t // tm,),
            in_specs=[pl.BlockSpec((tm, d), lambda i, p: (i, 0)),
                      pl.BlockSpec((tm, ROUTE_LANES), lambda i, p: (i, 0)),
                      pl.BlockSpec(memory_space=pl.ANY)],
            out_specs=pl.BlockSpec((tm, d), lambda i, p: (i, 0)),
            scratch_shapes=[buf, buf, buf, buf, pltpu.SemaphoreType.DMA((4,))]),
        out_shape=jax.ShapeDtypeStruct((t, d), F32),
        compiler_params=_cparams(("arbitrary",)),
        name="combine",
    )(pos_flat, x1, route_rows, ys)


def _moe(x1, hp, route_rows, route_t, w_gate, w_up, w_down, gmm_tm=256, sort_chunk=512):
    t = x1.shape[0]
    n = 2 * t
    ids3 = route_t[0:2].astype(I32).reshape(n // sort_chunk, 1, sort_chunk)
    counts = _expert_counts(ids3)[:, 0]
    starts = jnp.cumsum(counts) - counts
    pos_flat = _sorted_positions(ids3, starts).reshape(n)
    xs = _dispatch(pos_flat, hp)
    ys = _gmm(_gmm_metadata(counts, n, gmm_tm), xs, w_gate, w_up, w_down, gmm_tm)
    return _combine(pos_flat, x1, route_rows, ys)


def _layer(x, attn_norm_g, w_in, lambda_re, lambda_im, log_dt, ssm_b_re, ssm_b_im, ssm_c_re, ssm_c_im,
           ssm_d, w_glu, q_norm_g, k_norm_g, w_branch_ssm, w_branch_att, w_out, ffn_norm_g,
           router_group_w, router_group_b, router_expert_w, router_expert_b,
           expert_w_gate, expert_w_up, expert_w_down):
    nb, seq, d = x.shape
    t = nb * seq
    ds = w_glu.shape[0]
    da = N_HEADS * HEAD_DIM
    x2 = x.reshape(t, d)

    h = _rmsnorm(x2, attn_norm_g)
    wb = w_in.astype(BF16)
    o = 0
    u = _proj(h, wb[:, o:o + ds], "plain", F32, name="proj_u"); o += ds
    qg = q_norm_g.astype(F32) * (HEAD_DIM ** -0.5 * LOG2_E)
    q = _proj(h, wb[:, o:o + da], "headnorm", BF16, gain=qg, name="proj_q"); o += da
    k = _proj(h, wb[:, o:o + da], "headnorm", BF16, gain=k_norm_g, name="proj_k"); o += da
    v = _proj(h, wb[:, o:o + da], "plain", BF16, name="proj_v"); o += da
    gates = _proj(h, wb[:, o:], "sigmoid", BF16, name="proj_gates")

    a_re, a_im, wb_re, wb_im, wc = _ssm_params(lambda_re, lambda_im, log_dt, ssm_b_re, ssm_b_im, ssm_c_re, ssm_c_im)
    y_ssm = _ssm(u.reshape(nb, seq, ds), a_re, a_im, wb_re, wb_im, wc, ssm_d.astype(F32), w_glu.astype(BF16))
    y_att = _attention(q.reshape(nb, seq, da), k.reshape(nb, seq, da), v.reshape(nb, seq, da))

    merged = _merge(y_ssm.reshape(t, ds), y_att.reshape(t, da), gates,
                    w_branch_ssm.astype(BF16), w_branch_att.astype(BF16))

    wr = jnp.zeros((d, ROUTE_LANES), F32)
    wr = wr.at[:, :N_EXPERTS].set(router_expert_w.astype(F32))
    wr = wr.at[:, N_EXPERTS:N_EXPERTS + N_EXPERT_GROUPS].set(router_group_w.astype(F32))
    br = jnp.zeros((1, ROUTE_LANES), F32)
    br = br.at[0, :N_EXPERTS].set(router_expert_b.astype(F32))
    br = br.at[0, N_EXPERTS:N_EXPERTS + N_EXPERT_GROUPS].set(router_group_b.astype(F32))
    wr_hi = wr.astype(BF16)
    wr_lo = (wr - wr_hi.astype(F32)).astype(BF16)
    x1, hp, route_rows, route_t = _outproj(x2, merged, w_out.astype(BF16), ffn_norm_g, wr_hi, wr_lo, br)

    out = _moe(x1, hp, route_rows, route_t, expert_w_gate, expert_w_up, expert_w_down)
    return out.reshape(nb, seq, d)


def kernel(x, attn_norm_g, w_in, lambda_re, lambda_im, log_dt, ssm_b_re, ssm_b_im, ssm_c_re, ssm_c_im, ssm_d, w_glu, q_norm_g, k_norm_g, w_branch_ssm, w_branch_att, w_out, ffn_norm_g, router_group_w, router_group_b, router_expert_w, router_expert_b, expert_w_gate, expert_w_up, expert_w_down):
    depth = attn_norm_g.shape[0]
    for l in range(depth):
        x = _layer(x, attn_norm_g[l], w_in[l], lambda_re[l], lambda_im[l], log_dt[l], ssm_b_re[l], ssm_b_im[l],
                   ssm_c_re[l], ssm_c_im[l], ssm_d[l], w_glu[l], q_norm_g[l], k_norm_g[l], w_branch_ssm[l],
                   w_branch_att[l], w_out[l], ffn_norm_g[l], router_group_w[l], router_group_b[l],
                   router_expert_w[l], router_expert_b[l], expert_w_gate[l], expert_w_up[l], expert_w_down[l])
    return x
```

```python
import functools

import jax
import jax.numpy as jnp
from jax import lax
from jax.experimental import pallas as pl
from jax.experimental.pallas import tpu as pltpu

F32 = jnp.float32
BF16 = jnp.bfloat16
I32 = jnp.int32
U32 = jnp.uint32

EPS = 1e-6
N_HEADS = 8
HEAD_DIM = 128
SSM_GROUP = 16
SSM_STATE = 64
GROUPS_PER_BLOCK = 16
N_EXPERT_GROUPS = 4
EXPERTS_PER_GROUP = 8
N_EXPERTS = N_EXPERT_GROUPS * EXPERTS_PER_GROUP
LANES = 128
ROUTE_LANES = 128
NEG_BIG = -1e30
DMA_UNROLL = 8
EXP_ZERO_BELOW = -104.0
LOG2_E = 1.4426950408889634

VMEM_LIMIT = 56 * 1024 * 1024


def _cparams(sem):
    return pltpu.CompilerParams(dimension_semantics=sem, vmem_limit_bytes=VMEM_LIMIT)


def _rmsnorm_body(x_ref, g_ref, o_ref):
    x = x_ref[...]
    ms = jnp.mean(x * x, axis=-1, keepdims=True)
    o_ref[...] = (x * lax.rsqrt(ms + EPS) * g_ref[...]).astype(o_ref.dtype)


def _rmsnorm(x2, g, tm=512):
    t, d = x2.shape
    return pl.pallas_call(
        _rmsnorm_body,
        grid=(t // tm,),
        in_specs=[pl.BlockSpec((tm, d), lambda i: (i, 0)), pl.BlockSpec((1, d), lambda i: (0, 0))],
        out_specs=pl.BlockSpec((tm, d), lambda i: (i, 0)),
        out_shape=jax.ShapeDtypeStruct((t, d), BF16),
        compiler_params=_cparams(("arbitrary",)),
        name="rmsnorm",
    )(x2, g.reshape(1, d))


def _proj_body(h_ref, w_ref, g_ref, o_ref, *, mode):
    acc = jnp.dot(h_ref[...], w_ref[...], preferred_element_type=F32)
    if mode == "plain":
        o_ref[...] = acc.astype(o_ref.dtype)
    elif mode == "sigmoid":
        o_ref[...] = jax.nn.sigmoid(acc).astype(o_ref.dtype)
    else:
        n = acc.shape[1] // HEAD_DIM
        for hh in range(n):
            blk = acc[:, hh * HEAD_DIM:(hh + 1) * HEAD_DIM]
            ms = jnp.mean(blk * blk, axis=-1, keepdims=True)
            o_ref[:, hh * HEAD_DIM:(hh + 1) * HEAD_DIM] = (blk * lax.rsqrt(ms + EPS) * g_ref[...]).astype(o_ref.dtype)


def _proj(h, w, mode, out_dtype, gain=None, tm=1024, tn=1024, name="proj"):
    t, k = h.shape
    n = w.shape[1]
    if gain is None:
        gain = jnp.ones((HEAD_DIM,), F32)
    return pl.pallas_call(
        functools.partial(_proj_body, mode=mode),
        grid=(n // tn, t // tm),
        in_specs=[pl.BlockSpec((tm, k), lambda j, i: (i, 0)),
                  pl.BlockSpec((k, tn), lambda j, i: (0, j)),
                  pl.BlockSpec((1, HEAD_DIM), lambda j, i: (0, 0))],
        out_specs=pl.BlockSpec((tm, tn), lambda j, i: (i, j)),
        out_shape=jax.ShapeDtypeStruct((t, n), out_dtype),
        compiler_params=_cparams(("arbitrary", "arbitrary")),
        name=name,
    )(h, w, gain.reshape(1, HEAD_DIM).astype(F32))


def _ssm_body(u_ref, perm_ref, permt_ref, are_ref, aim_ref, wbre_ref, wbim_ref, wc_ref, d_ref, wglu_ref,
              o_ref, bure_ref, buim_ref, sre_ref, sim_ref, *, tc, lane_chunk):
    nb = u_ref.shape[0]
    ds = u_ref.shape[2]
    rows = nb * tc
    nblk = wbre_ref.shape[0]
    cb = wbre_ref.shape[1]
    sb = wbre_ref.shape[2]
    n_state = nblk * sb

    @pl.when(pl.program_id(0) == 0)
    def _():
        sre_ref[...] = jnp.zeros_like(sre_ref)
        sim_ref[...] = jnp.zeros_like(sim_ref)

    u_bm = u_ref[...].reshape(rows, ds)
    u_tm = jnp.dot(perm_ref[...], u_bm.astype(BF16), preferred_element_type=F32).astype(BF16)
    for gb in range(nblk):
        lhs = u_tm[:, gb * cb:(gb + 1) * cb]
        bure_ref[:, gb * sb:(gb + 1) * sb] = jnp.dot(lhs, wbre_ref[gb], preferred_element_type=F32)
        buim_ref[:, gb * sb:(gb + 1) * sb] = jnp.dot(lhs, wbim_ref[gb], preferred_element_type=F32)

    for lc in range(n_state // lane_chunk):
        sl = slice(lc * lane_chunk, (lc + 1) * lane_chunk)
        ar = jnp.broadcast_to(are_ref[:, sl], (nb, lane_chunk))
        ai = jnp.broadcast_to(aim_ref[:, sl], (nb, lane_chunk))

        def step(t, carry, sl=sl, ar=ar, ai=ai):
            xr, xi = carry
            r0 = pl.multiple_of(t * nb, nb)
            br = bure_ref[pl.ds(r0, nb), sl]
            bi = buim_ref[pl.ds(r0, nb), sl]
            nr = ar * xr - ai * xi + br
            ni = ar * xi + ai * xr + bi
            bure_ref[pl.ds(r0, nb), sl] = nr
            buim_ref[pl.ds(r0, nb), sl] = ni
            return nr, ni

        xr, xi = lax.fori_loop(0, tc, step, (sre_ref[:, sl], sim_ref[:, sl]))
        sre_ref[:, sl] = xr
        sim_ref[:, sl] = xi

    ys = []
    for gb in range(nblk):
        xs = jnp.concatenate([bure_ref[:, gb * sb:(gb + 1) * sb], buim_ref[:, gb * sb:(gb + 1) * sb]], axis=1)
        ys.append(jnp.dot(xs.astype(BF16), wc_ref[gb], preferred_element_type=F32))
    y_tm = jnp.concatenate(ys, axis=1)
    y_hi = y_tm.astype(BF16)
    y_lo = (y_tm - y_hi.astype(F32)).astype(BF16)
    y_bm = (jnp.dot(permt_ref[...], y_hi, preferred_element_type=F32)
            + jnp.dot(permt_ref[...], y_lo, preferred_element_type=F32))
    y = y_bm + d_ref[...] * u_bm
    z = jax.nn.gelu(y)
    gate = jax.nn.sigmoid(jnp.dot(z.astype(BF16), wglu_ref[...], preferred_element_type=F32))
    o_ref[...] = (z * gate).reshape(nb, tc, ds).astype(o_ref.dtype)


def _ssm(u, a_re, a_im, wb_re, wb_im, wc, d_skip, w_glu, tc=64, lane_chunk=512):
    nb, seq, ds = u.shape
    rows = nb * tc
    n_state = a_re.shape[1]
    r = jnp.arange(rows)
    src = (r % nb) * tc + r // nb
    perm = (src[:, None] == r[None, :]).astype(BF16)
    permt = perm.T
    const = lambda *shape: pl.BlockSpec(shape, lambda c: (0,) * len(shape))
    return pl.pallas_call(
        functools.partial(_ssm_body, tc=tc, lane_chunk=lane_chunk),
        grid=(seq // tc,),
        in_specs=[pl.BlockSpec((nb, tc, ds), lambda c: (0, c, 0)),
                  const(rows, rows), const(rows, rows),
                  const(1, n_state), const(1, n_state),
                  const(*wb_re.shape), const(*wb_im.shape), const(*wc.shape),
                  const(1, ds), const(ds, ds)],
        out_specs=pl.BlockSpec((nb, tc, ds), lambda c: (0, c, 0)),
        out_shape=jax.ShapeDtypeStruct((nb, seq, ds), BF16),
        scratch_shapes=[pltpu.VMEM((rows, n_state), F32), pltpu.VMEM((rows, n_state), F32),
                        pltpu.VMEM((nb, n_state), F32), pltpu.VMEM((nb, n_state), F32)],
        compiler_params=_cparams(("arbitrary",)),
        name="s5_mixer",
    )(u, perm, permt, a_re, a_im, wb_re, wb_im, wc, d_skip.reshape(1, ds), w_glu)


def _ssm_params(lambda_re, lambda_im, log_dt, b_re, b_im, c_re, c_im):
    g, p = lambda_re.shape
    h = b_re.shape[2]
    nblk = g // GROUPS_PER_BLOCK
    dt = jnp.exp(log_dt.astype(F32))[:, None]
    lr = lambda_re.astype(F32)
    li = lambda_im.astype(F32)
    mag = jnp.exp(lr * dt)
    abar_re, abar_im = mag * jnp.cos(li * dt), mag * jnp.sin(li * dt)
    nr, ni = abar_re - 1.0, abar_im
    den = lr * lr + li * li
    coef_re = (nr * lr + ni * li) / den
    coef_im = (ni * lr - nr * li) / den
    bbar_re = coef_re[..., None] * b_re - coef_im[..., None] * b_im
    bbar_im = coef_re[..., None] * b_im + coef_im[..., None] * b_re
    eye = jnp.eye(GROUPS_PER_BLOCK, dtype=F32)

    def pack_b(bb):
        bb = bb.reshape(nblk, GROUPS_PER_BLOCK, p, h)
        return jnp.einsum("bgph,gk->bghkp", bb, eye).reshape(nblk, GROUPS_PER_BLOCK * h, GROUPS_PER_BLOCK * p)

    def pack_c(cc):
        cc = cc.reshape(nblk, GROUPS_PER_BLOCK, h, p)
        return jnp.einsum("bghp,gk->bgpkh", cc, eye).reshape(nblk, GROUPS_PER_BLOCK * p, GROUPS_PER_BLOCK * h)

    wc = jnp.concatenate([pack_c(c_re.astype(F32)), -pack_c(c_im.astype(F32))], axis=1)
    return (abar_re.reshape(1, g * p), abar_im.reshape(1, g * p),
            pack_b(bbar_re).astype(BF16), pack_b(bbar_im).astype(BF16), wc.astype(BF16))


def _attn_body(q_ref, k_ref, v_ref, tri_ref, o_ref, acc_ref, rs_ref, *, tq, nhs):
    qi = pl.program_id(2)
    tri = tri_ref[...]
    row = lax.broadcasted_iota(I32, (nhs * tq, tq), 0) % tq
    col = lax.broadcasted_iota(I32, (nhs * tq, tq), 1)
    causal = col < row

    def sweep(j, masked):
        k0 = pl.multiple_of(j * tq, tq)
        heads = [slice(h * HEAD_DIM, (h + 1) * HEAD_DIM) for h in range(nhs)]
        z = jnp.concatenate(
            [lax.dot_general(q_ref[0, :, hs], k_ref[0, pl.ds(k0, tq), hs], (((1,), (1,)), ((), ())),
                             preferred_element_type=F32) for hs in heads], axis=0)
        sp = jnp.maximum(z, 0.0) + jnp.log(1.0 + jnp.exp2(-jnp.abs(z))) * LOG2_E
        spm = jnp.where(causal, sp, 0.0) if masked else sp
        sums = jnp.dot(spm.astype(BF16), tri, preferred_element_type=F32)
        rsum = rs_ref[...]
        w = jnp.exp2(z - (sp + sums[:, :tq] + jnp.concatenate([rsum] * (tq // LANES), axis=1)))
        if masked:
            w = jnp.where(causal, w, 0.0)
        wb = w.astype(BF16)
        for h, hs in enumerate(heads):
            acc_ref[h] += jnp.dot(wb[h * tq:(h + 1) * tq], v_ref[0, pl.ds(k0, tq), hs], preferred_element_type=F32)
        rsum = rsum + sums[:, tq:]
        rs_ref[...] = rsum
        return (jnp.min(rsum) < -EXP_ZERO_BELOW * LOG2_E).astype(I32)

    acc_ref[...] = jnp.zeros_like(acc_ref)
    rs_ref[...] = jnp.zeros_like(rs_ref)
    live = sweep(qi, True)

    def cond(c):
        return (c[0] >= 0) & (c[1] > 0)

    def body(c):
        return c[0] - 1, sweep(c[0], False)

    lax.while_loop(cond, body, (qi - 1, live))
    for h in range(nhs):
        o_ref[0, :, h * HEAD_DIM:(h + 1) * HEAD_DIM] = acc_ref[h].astype(o_ref.dtype)


def _attention(q, k, v, tq=256, nhs=4):
    nb, seq, da = q.shape
    nh = da // HEAD_DIM
    wd = nhs * HEAD_DIM
    r = jnp.arange(tq)
    tri = jnp.concatenate([(r[:, None] > r[None, :]).astype(BF16), jnp.ones((tq, LANES), BF16)], axis=1)
    return pl.pallas_call(
        functools.partial(_attn_body, tq=tq, nhs=nhs),
        grid=(nb, nh // nhs, seq // tq),
        in_specs=[pl.BlockSpec((1, tq, wd), lambda b, h, i: (b, i, h)),
                  pl.BlockSpec((1, seq, wd), lambda b, h, i: (b, 0, h)),
                  pl.BlockSpec((1, seq, wd), lambda b, h, i: (b, 0, h)),
                  pl.BlockSpec((tq, tq + LANES), lambda b, h, i: (0, 0))],
        out_specs=pl.BlockSpec((1, tq, wd), lambda b, h, i: (b, i, h)),
        out_shape=jax.ShapeDtypeStruct((nb, seq, da), BF16),
        scratch_shapes=[pltpu.VMEM((nhs, tq, HEAD_DIM), F32), pltpu.VMEM((nhs * tq, LANES), F32)],
        compiler_params=_cparams(("arbitrary", "arbitrary", "arbitrary")),
        name="stick_attention",
    )(q, k, v, tri)


def _merge_body(ys_ref, ya_ref, gs_ref, ga_ref, ws_ref, wa_ref, o_ref):
    ps = jnp.dot(ys_ref[...], ws_ref[...], preferred_element_type=F32)
    pa = jnp.dot(ya_ref[...], wa_ref[...], preferred_element_type=F32)
    o_ref[...] = (gs_ref[...].astype(F32) * ps + ga_ref[...].astype(F32) * pa).astype(o_ref.dtype)


def _merge(y_ssm, y_att, gates, w_s, w_a, tm=1024, tn=1024):
    t, ks = y_ssm.shape
    d = w_s.shape[1]
    nj = d // tn
    return pl.pallas_call(
        _merge_body,
        grid=(nj, t // tm),
        in_specs=[pl.BlockSpec((tm, ks), lambda j, i: (i, 0)),
                  pl.BlockSpec((tm, y_att.shape[1]), lambda j, i: (i, 0)),
                  pl.BlockSpec((tm, tn), lambda j, i: (i, j)),
                  pl.BlockSpec((tm, tn), lambda j, i: (i, nj + j)),
                  pl.BlockSpec((ks, tn), lambda j, i: (0, j)),
                  pl.BlockSpec((w_a.shape[0], tn), lambda j, i: (0, j))],
        out_specs=pl.BlockSpec((tm, tn), lambda j, i: (i, j)),
        out_shape=jax.ShapeDtypeStruct((t, d), BF16),
        compiler_params=_cparams(("arbitrary", "arbitrary")),
        name="branch_merge",
    )(y_ssm, y_att, gates, gates, w_s, w_a)


def _lane_min_index(mask, lane):
    return jnp.min(jnp.where(mask, lane, float(ROUTE_LANES)), axis=1, keepdims=True)


def _outproj_body(x_ref, m_ref, w_ref, g_ref, wrh_ref, wrl_ref, br_ref,
                  x1_ref, hp_ref, rr_ref, rt_ref):
    x1 = x_ref[...] + jnp.dot(m_ref[...], w_ref[...], preferred_element_type=F32)
    x1_ref[...] = x1
    ms = jnp.mean(x1 * x1, axis=-1, keepdims=True)
    h2 = x1 * lax.rsqrt(ms + EPS) * g_ref[...]
    hp_ref[...] = h2
    hb = h2.astype(BF16)

    hl = (h2 - hb.astype(F32)).astype(BF16)
    logits = (jnp.dot(hb, wrh_ref[...], preferred_element_type=F32)
              + jnp.dot(hl, wrh_ref[...], preferred_element_type=F32)
              + jnp.dot(hb, wrl_ref[...], preferred_element_type=F32)) + br_ref[...]
    tm = logits.shape[0]
    lane = lax.broadcasted_iota(I32, (tm, ROUTE_LANES), 1).astype(F32)
    is_group = (lane >= N_EXPERTS) & (lane < N_EXPERTS + N_EXPERT_GROUPS)
    gl = jnp.where(is_group, logits, NEG_BIG)
    gmax = jnp.max(gl, axis=1, keepdims=True)
    gidx = _lane_min_index(gl == gmax, lane) - N_EXPERTS
    g_top = 1.0 / jnp.sum(jnp.exp(gl - gmax), axis=1, keepdims=True)
    lo = gidx * EXPERTS_PER_GROUP
    in_grp = (lane >= lo) & (lane < lo + EXPERTS_PER_GROUP)
    el = jnp.where(in_grp, logits, NEG_BIG)
    m1 = jnp.max(el, axis=1, keepdims=True)
    i1 = _lane_min_index(el == m1, lane)
    el2 = jnp.where(lane == i1, NEG_BIG, el)
    m2 = jnp.max(el2, axis=1, keepdims=True)
    i2 = _lane_min_index(el2 == m2, lane)
    dlt = jnp.exp(m2 - m1)
    w1 = 1.0 / (1.0 + dlt)
    w2 = dlt * w1
    route = jnp.where(lane == 0, i1,
                      jnp.where(lane == 1, i2,
                                jnp.where(lane == 2, g_top * w1,
                                          jnp.where(lane == 3, g_top * w2, 0.0))))
    rr_ref[...] = route
    rt_ref[...] = jnp.transpose(route)[0:8, :]


def _outproj(x2, merged, w_out, g, wr_hi, wr_lo, br, tm=512):
    t, d = x2.shape
    const = lambda *shape: pl.BlockSpec(shape, lambda i: (0,) * len(shape))
    return pl.pallas_call(
        _outproj_body,
        grid=(t // tm,),
        in_specs=[pl.BlockSpec((tm, d), lambda i: (i, 0)), pl.BlockSpec((tm, d), lambda i: (i, 0)),
                  const(d, d), const(1, d), const(d, ROUTE_LANES), const(d, ROUTE_LANES), const(1, ROUTE_LANES)],
        out_specs=[pl.BlockSpec((tm, d), lambda i: (i, 0)),
                   pl.BlockSpec((tm, d), lambda i: (i, 0)),
                   pl.BlockSpec((tm, ROUTE_LANES), lambda i: (i, 0)),
                   pl.BlockSpec((8, tm), lambda i: (0, i))],
        out_shape=[jax.ShapeDtypeStruct((t, d), F32),
                   jax.ShapeDtypeStruct((t, d), F32),
                   jax.ShapeDtypeStruct((t, ROUTE_LANES), F32),
                   jax.ShapeDtypeStruct((8, t), F32)],
        compiler_params=_cparams(("arbitrary",)),
        name="outproj_router",
    )(x2, merged, w_out, g.reshape(1, d), wr_hi, wr_lo, br)


def _count_body(ids_ref, cnt_ref):
    nc, _, c = ids_ref.shape
    eidx = lax.broadcasted_iota(I32, (N_EXPERTS, c), 0)

    def step(i, acc):
        return acc + jnp.where(ids_ref[i] == eidx, 1.0, 0.0)

    acc = lax.fori_loop(0, nc, step, jnp.zeros((N_EXPERTS, c), F32))
    tot = jnp.sum(acc, axis=1, keepdims=True)
    cnt_ref[...] = jnp.broadcast_to(tot, cnt_ref.shape).astype(I32)


def _expert_counts(ids3):
    return pl.pallas_call(
        _count_body,
        out_shape=jax.ShapeDtypeStruct((N_EXPERTS, 128), I32),
        name="expert_counts",
    )(ids3)


def _pos_body(ids_ref, start_ref, incl_ref, pos_ref):
    nc, _, c = ids_ref.shape
    eidx = lax.broadcasted_iota(I32, (N_EXPERTS, c), 0)
    incl = incl_ref[...]

    def step(i, carry):
        onehot = ids_ref[i] == eidx
        cum = jnp.dot(jnp.where(onehot, 1.0, 0.0).astype(BF16), incl, preferred_element_type=F32)
        val = jnp.where(onehot, cum - 1.0 + carry, 0.0)
        pos_ref[i] = jnp.sum(val, axis=0, keepdims=True).astype(I32)
        return carry + cum[:, c - 1:c]

    lax.fori_loop(0, nc, step, start_ref[:, 0:1].astype(F32))


def _sorted_positions(ids3, starts):
    nc, _, c = ids3.shape
    r = jnp.arange(c)
    incl = (r[:, None] <= r[None, :]).astype(BF16)
    start_b = jnp.broadcast_to(starts.astype(I32)[:, None], (N_EXPERTS, 128))
    return pl.pallas_call(
        _pos_body,
        out_shape=jax.ShapeDtypeStruct((nc, 1, c), I32),
        name="sorted_positions",
    )(ids3, start_b, incl)


def _dispatch_body(pos_ref, src_ref, dst_ref, sem, *, n_tok):
    tm = src_ref.shape[0]
    base = pl.program_id(0) * tm

    def issue(i, carry):
        for k in range(DMA_UNROLL):
            j = i * DMA_UNROLL + k
            row = src_ref.at[pl.ds(j, 1), :]
            pltpu.make_async_copy(row, dst_ref.at[pl.ds(pos_ref[base + j], 1), :], sem.at[0]).start()
            pltpu.make_async_copy(row, dst_ref.at[pl.ds(pos_ref[n_tok + base + j], 1), :], sem.at[1]).start()
        return carry

    lax.fori_loop(0, tm // DMA_UNROLL, issue, 0)
    for s in range(2):
        pltpu.make_async_copy(src_ref, dst_ref.at[pl.ds(0, tm), :], sem.at[s]).wait()


def _dispatch(pos_flat, hp, tm=256):
    n = pos_flat.shape[0]
    t, d = hp.shape
    assert t % tm == 0 and tm % DMA_UNROLL == 0
    return pl.pallas_call(
        functools.partial(_dispatch_body, n_tok=t),
        grid_spec=pltpu.PrefetchScalarGridSpec(
            num_scalar_prefetch=1, grid=(t // tm,),
            in_specs=[pl.BlockSpec((tm, d), lambda i, p: (i, 0))],
            out_specs=pl.BlockSpec(memory_space=pl.ANY),
            scratch_shapes=[pltpu.SemaphoreType.DMA((2,))]),
        out_shape=jax.ShapeDtypeStruct((n, d), hp.dtype),
        compiler_params=_cparams(("arbitrary",)),
        name="dispatch",
    )(pos_flat, hp)


def _gmm_body(tile_ref, exp_ref, lo_ref, hi_ref, x_ref, wg_ref, wu_ref, wd_ref, o_ref, wgs, wus, wds):
    w = pl.program_id(0)
    prev = jnp.maximum(w - 1, 0)
    new_expert = (w == 0) | (exp_ref[w] != exp_ref[prev])
    first_of_tile = (w == 0) | (tile_ref[w] != tile_ref[prev])
    lo = lo_ref[w]
    hi = hi_ref[w]

    @pl.when(new_expert)
    def _():
        wgs[...] = wg_ref[0].astype(BF16)
        wus[...] = wu_ref[0].astype(BF16)
        wds[...] = wd_ref[0].astype(BF16)

    @pl.when(hi > lo)
    def _():
        xx = x_ref[...].astype(BF16)
        g = jnp.dot(xx, wgs[...], preferred_element_type=F32)
        u = jnp.dot(xx, wus[...], preferred_element_type=F32)
        hid = (g * jax.nn.sigmoid(g)) * u
        y = jnp.dot(hid.astype(BF16), wds[...], preferred_element_type=F32)
        rows = lax.broadcasted_iota(I32, (y.shape[0], 1), 0)
        y = jnp.where((rows >= lo) & (rows < hi), y, 0.0)

        @pl.when(first_of_tile)
        def _():
            o_ref[...] = y

        @pl.when(jnp.logical_not(first_of_tile))
        def _():
            o_ref[...] += y


def _gmm(meta, xs, w_gate, w_up, w_down, tm):
    tile_id, expert_id, row_lo, row_hi = meta
    n, _ = xs.shape
    e, d, f = w_gate.shape
    nw = tile_id.shape[0]
    return pl.pallas_call(
        _gmm_body,
        grid_spec=pltpu.PrefetchScalarGridSpec(
            num_scalar_prefetch=4, grid=(nw,),
            in_specs=[pl.BlockSpec((tm, d), lambda w, ti, ex, lo, hi: (ti[w], 0)),
                      pl.BlockSpec((1, d, f), lambda w, ti, ex, lo, hi: (ex[w], 0, 0)),
                      pl.BlockSpec((1, d, f), lambda w, ti, ex, lo, hi: (ex[w], 0, 0)),
                      pl.BlockSpec((1, f, d), lambda w, ti, ex, lo, hi: (ex[w], 0, 0))],
            out_specs=pl.BlockSpec((tm, d), lambda w, ti, ex, lo, hi: (ti[w], 0)),
            scratch_shapes=[pltpu.VMEM((d, f), BF16), pltpu.VMEM((d, f), BF16), pltpu.VMEM((f, d), BF16)]),
        out_shape=jax.ShapeDtypeStruct((n, d), F32),
        compiler_params=_cparams(("arbitrary",)),
        name="expert_gmm",
    )(tile_id, expert_id, row_lo, row_hi, xs, w_gate, w_up, w_down)


def _gmm_metadata(counts, n_rows, tm):
    nt = n_rows // tm
    nw = nt + N_EXPERTS - 1
    ends = jnp.cumsum(counts)
    starts = ends - counts
    first_tile = starts // tm
    n_items = jnp.where(counts > 0, (ends - 1) // tm - first_tile + 1, 0)
    item_end = jnp.cumsum(n_items)
    item_start = item_end - n_items
    total = item_end[-1]
    w = jnp.arange(nw, dtype=I32)
    wc = jnp.minimum(w, total - 1)
    ex = jnp.sum((item_end[None, :] <= wc[:, None]).astype(I32), axis=1)
    tile = first_tile[ex] + (wc - item_start[ex])
    lo = jnp.maximum(starts[ex], tile * tm) - tile * tm
    hi = jnp.minimum(ends[ex], (tile + 1) * tm) - tile * tm
    valid = w < total
    lo = jnp.where(valid, lo, 0)
    hi = jnp.where(valid, hi, 0)
    return tile.astype(I32), ex, lo.astype(I32), hi.astype(I32)


def _combine_body(pos_ref, x1_ref, rr_ref, ys_ref, o_ref, a0, a1, b0, b1, sem, *, n_tok):
    i = pl.program_id(0)
    nsteps = pl.num_programs(0)
    tm = x1_ref.shape[0]

    def gather(tile, bufs, sems):
        base = tile * tm

        def issue(jj, carry):
            for k in range(DMA_UNROLL):
                j = jj * DMA_UNROLL + k
                pltpu.make_async_copy(ys_ref.at[pl.ds(pos_ref[base + j], 1), :],
                                      bufs[0].at[pl.ds(j, 1), :], sem.at[sems[0]]).start()
                pltpu.make_async_copy(ys_ref.at[pl.ds(pos_ref[n_tok + base + j], 1), :],
                                      bufs[1].at[pl.ds(j, 1), :], sem.at[sems[1]]).start()
            return carry

        lax.fori_loop(0, tm // DMA_UNROLL, issue, 0)

    def step(cur, cur_sems, nxt, nxt_sems):
        @pl.when(i == 0)
        def _():
            gather(i, cur, cur_sems)

        @pl.when(i + 1 < nsteps)
        def _():
            gather(i + 1, nxt, nxt_sems)

        for buf, s in zip(cur, cur_sems):
            pltpu.make_async_copy(ys_ref.at[pl.ds(0, tm), :], buf, sem.at[s]).wait()
        rr = rr_ref[...]
        o_ref[...] = x1_ref[...] + rr[:, 2:3] * cur[0][...] + rr[:, 3:4] * cur[1][...]

    @pl.when(i % 2 == 0)
    def _():
        step((a0, a1), (0, 1), (b0, b1), (2, 3))

    @pl.when(i % 2 == 1)
    def _():
        step((b0, b1), (2, 3), (a0, a1), (0, 1))


def _combine(pos_flat, x1, route_rows, ys, tm=256):
    t, d = x1.shape
    assert tm % DMA_UNROLL == 0
    buf = pltpu.VMEM((tm, d), F32)
    return pl.pallas_call(
        functools.partial(_combine_body, n_tok=t),
        grid_spec=pltpu.PrefetchScalarGridSpec(
            num_scalar_prefetch=1, grid=(t // tm,),
            in_specs=[pl.BlockSpec((tm, d), lambda i, p: (i, 0)),
                      pl.BlockSpec((tm, ROUTE_LANES), lambda i, p: (i, 0)),
                      pl.BlockSpec(memory_space=pl.ANY)],
            out_specs=pl.BlockSpec((tm, d), lambda i, p: (i, 0)),
            scratch_shapes=[buf, buf, buf, buf, pltpu.SemaphoreType.DMA((4,))]),
        out_shape=jax.ShapeDtypeStruct((t, d), F32),
        compiler_params=_cparams(("arbitrary",)),
        name="combine",
    )(pos_flat, x1, route_rows, ys)


def _moe(x1, hp, route_rows, route_t, w_gate, w_up, w_down, gmm_tm=256, sort_chunk=512):
    t = x1.shape[0]
    n = 2 * t
    ids3 = route_t[0:2].astype(I32).reshape(n // sort_chunk, 1, sort_chunk)
    counts = _expert_counts(ids3)[:, 0]
    starts = jnp.cumsum(counts) - counts
    pos_flat = _sorted_positions(ids3, starts).reshape(n)
    xs = _dispatch(pos_flat, hp)
    ys = _gmm(_gmm_metadata(counts, n, gmm_tm), xs, w_gate, w_up, w_down, gmm_tm)
    return _combine(pos_flat, x1, route_rows, ys)


def _layer(x, attn_norm_g, w_in, lambda_re, lambda_im, log_dt, ssm_b_re, ssm_b_im, ssm_c_re, ssm_c_im,
           ssm_d, w_glu, q_norm_g, k_norm_g, w_branch_ssm, w_branch_att, w_out, ffn_norm_g,
           router_group_w, router_group_b, router_expert_w, router_expert_b,
           expert_w_gate, expert_w_up, expert_w_down):
    nb, seq, d = x.shape
    t = nb * seq
    ds = w_glu.shape[0]
    da = N_HEADS * HEAD_DIM
    x2 = x.reshape(t, d)

    h = _rmsnorm(x2, attn_norm_g)
    wb = w_in.astype(BF16)
    o = 0
    u = _proj(h, wb[:, o:o + ds], "plain", F32, name="proj_u"); o += ds
    qg = q_norm_g.astype(F32) * (HEAD_DIM ** -0.5 * LOG2_E)
    q = _proj(h, wb[:, o:o + da], "headnorm", BF16, gain=qg, name="proj_q"); o += da
    k = _proj(h, wb[:, o:o + da], "headnorm", BF16, gain=k_norm_g, name="proj_k"); o += da
    v = _proj(h, wb[:, o:o + da], "plain", BF16, name="proj_v"); o += da
    gates = _proj(h, wb[:, o:], "sigmoid", BF16, name="proj_gates")

    a_re, a_im, wb_re, wb_im, wc = _ssm_params(lambda_re, lambda_im, log_dt, ssm_b_re, ssm_b_im, ssm_c_re, ssm_c_im)
    y_ssm = _ssm(u.reshape(nb, seq, ds), a_re, a_im, wb_re, wb_im, wc, ssm_d.astype(F32), w_glu.astype(BF16))
    y_att = _attention(q.reshape(nb, seq, da), k.reshape(nb, seq, da), v.reshape(nb, seq, da))

    merged = _merge(y_ssm.reshape(t, ds), y_att.reshape(t, da), gates,
                    w_branch_ssm.astype(BF16), w_branch_att.astype(BF16))

    wr = jnp.zeros((d, ROUTE_LANES), F32)
    wr = wr.at[:, :N_EXPERTS].set(router_expert_w.astype(F32))
    wr = wr.at[:, N_EXPERTS:N_EXPERTS + N_EXPERT_GROUPS].set(router_group_w.astype(F32))
    br = jnp.zeros((1, ROUTE_LANES), F32)
    br = br.at[0, :N_EXPERTS].set(router_expert_b.astype(F32))
    br = br.at[0, N_EXPERTS:N_EXPERTS + N_EXPERT_GROUPS].set(router_group_b.astype(F32))
    wr_hi = wr.astype(BF16)
    wr_lo = (wr - wr_hi.astype(F32)).astype(BF16)
    x1, hp, route_rows, route_t = _outproj(x2, merged, w_out.astype(BF16), ffn_norm_g, wr_hi, wr_lo, br)

    out = _moe(x1, hp, route_rows, route_t, expert_w_gate, expert_w_up, expert_w_down)
    return out.reshape(nb, seq, d)


def kernel(x, attn_norm_g, w_in, lambda_re, lambda_im, log_dt, ssm_b_re, ssm_b_im, ssm_c_re, ssm_c_im, ssm_d, w_glu, q_norm_g, k_norm_g, w_branch_ssm, w_branch_att, w_out, ffn_norm_g, router_group_w, router_group_b, router_expert_w, router_expert_b, expert_w_gate, expert_w_up, expert_w_down):
    depth = attn_norm_g.shape[0]
    for l in range(depth):
        x = _layer(x, attn_norm_g[l], w_in[l], lambda_re[l], lambda_im[l], log_dt[l], ssm_b_re[l], ssm_b_im[l],
                   ssm_c_re[l], ssm_c_im[l], ssm_d[l], w_glu[l], q_norm_g[l], k_norm_g[l], w_branch_ssm[l],
                   w_branch_att[l], w_out[l], ffn_norm_g[l], router_group_w[l], router_group_b[l],
                   router_expert_w[l], router_expert_b[l], expert_w_gate[l], expert_w_up[l], expert_w_down[l])
    return x
```

```python
import functools

import jax
import jax.numpy as jnp
from jax import lax
from jax.experimental import pallas as pl
from jax.experimental.pallas import tpu as pltpu

F32 = jnp.float32
BF16 = jnp.bfloat16
I32 = jnp.int32
U32 = jnp.uint32

EPS = 1e-6
N_HEADS = 8
HEAD_DIM = 128
SSM_GROUP = 16
SSM_STATE = 64
GROUPS_PER_BLOCK = 16
N_EXPERT_GROUPS = 4
EXPERTS_PER_GROUP = 8
N_EXPERTS = N_EXPERT_GROUPS * EXPERTS_PER_GROUP
LANES = 128
ROUTE_LANES = 128
NEG_BIG = -1e30
DMA_UNROLL = 8
SCAN_UNROLL = 4
EXP_ZERO_BELOW = -104.0
LOG2_E = 1.4426950408889634

VMEM_LIMIT = 56 * 1024 * 1024


def _cparams(sem):
    return pltpu.CompilerParams(dimension_semantics=sem, vmem_limit_bytes=VMEM_LIMIT)


def _rmsnorm_body(x_ref, g_ref, o_ref):
    x = x_ref[...]
    ms = jnp.mean(x * x, axis=-1, keepdims=True)
    o_ref[...] = (x * lax.rsqrt(ms + EPS) * g_ref[...]).astype(o_ref.dtype)


def _rmsnorm(x2, g, tm=512):
    t, d = x2.shape
    return pl.pallas_call(
        _rmsnorm_body,
        grid=(t // tm,),
        in_specs=[pl.BlockSpec((tm, d), lambda i: (i, 0)), pl.BlockSpec((1, d), lambda i: (0, 0))],
        out_specs=pl.BlockSpec((tm, d), lambda i: (i, 0)),
        out_shape=jax.ShapeDtypeStruct((t, d), BF16),
        compiler_params=_cparams(("arbitrary",)),
        name="rmsnorm",
    )(x2, g.reshape(1, d))


def _proj_body(h_ref, w_ref, g_ref, o_ref, wb_ref, *, mode):
    @pl.when(pl.program_id(1) == 0)
    def _():
        wb_ref[...] = w_ref[...].astype(BF16)

    acc = jnp.dot(h_ref[...], wb_ref[...], preferred_element_type=F32)
    if mode == "plain":
        o_ref[...] = acc.astype(o_ref.dtype)
    elif mode == "sigmoid":
        o_ref[...] = jax.nn.sigmoid(acc).astype(o_ref.dtype)
    else:
        n = acc.shape[1] // HEAD_DIM
        for hh in range(n):
            blk = acc[:, hh * HEAD_DIM:(hh + 1) * HEAD_DIM]
            ms = jnp.mean(blk * blk, axis=-1, keepdims=True)
            o_ref[:, hh * HEAD_DIM:(hh + 1) * HEAD_DIM] = (blk * lax.rsqrt(ms + EPS) * g_ref[...]).astype(o_ref.dtype)


def _proj(h, w, col0, n, mode, out_dtype, gain=None, tm=1024, tn=1024, name="proj"):
    t, k = h.shape
    assert col0 % tn == 0 and n % tn == 0
    j0 = col0 // tn
    if gain is None:
        gain = jnp.ones((HEAD_DIM,), F32)
    return pl.pallas_call(
        functools.partial(_proj_body, mode=mode),
        grid=(n // tn, t // tm),
        in_specs=[pl.BlockSpec((tm, k), lambda j, i: (i, 0)),
                  pl.BlockSpec((k, tn), lambda j, i: (0, j0 + j)),
                  pl.BlockSpec((1, HEAD_DIM), lambda j, i: (0, 0))],
        out_specs=pl.BlockSpec((tm, tn), lambda j, i: (i, j)),
        out_shape=jax.ShapeDtypeStruct((t, n), out_dtype),
        scratch_shapes=[pltpu.VMEM((k, tn), BF16)],
        compiler_params=_cparams(("arbitrary", "arbitrary")),
        name=name,
    )(h, w, gain.reshape(1, HEAD_DIM).astype(F32))


def _ssm_body(u_ref, are_ref, aim_ref, wbre_ref, wbim_ref, wc_ref, d_ref, wglu_ref,
              o_ref, tm_ref, bure_ref, buim_ref, sre_ref, sim_ref, *, tc, lane_chunk):
    nb = u_ref.shape[0]
    ds = u_ref.shape[2]
    rows = nb * tc
    nslab = ds // LANES
    nblk = wbre_ref.shape[0]
    cb = wbre_ref.shape[1]
    sb = wbre_ref.shape[2]
    n_state = nblk * sb

    @pl.when(pl.program_id(0) == 0)
    def _():
        sre_ref[...] = jnp.zeros_like(sre_ref)
        sim_ref[...] = jnp.zeros_like(sim_ref)

    for b in range(nb):
        for j in range(nslab):
            tm_ref[j, pl.ds(b, tc, stride=nb), :] = u_ref[b, :, j * LANES:(j + 1) * LANES]
    u_tm = jnp.concatenate([tm_ref[j] for j in range(nslab)], axis=1).astype(BF16)
    for gb in range(nblk):
        lhs = u_tm[:, gb * cb:(gb + 1) * cb]
        bure_ref[:, gb * sb:(gb + 1) * sb] = jnp.dot(lhs, wbre_ref[gb], preferred_element_type=F32)
        buim_ref[:, gb * sb:(gb + 1) * sb] = jnp.dot(lhs, wbim_ref[gb], preferred_element_type=F32)

    for lc in range(n_state // lane_chunk):
        sl = slice(lc * lane_chunk, (lc + 1) * lane_chunk)
        ar = jnp.broadcast_to(are_ref[:, sl], (nb, lane_chunk))
        ai = jnp.broadcast_to(aim_ref[:, sl], (nb, lane_chunk))

        def steps(tt, carry, sl=sl, ar=ar, ai=ai):
            xr, xi = carry
            for k in range(SCAN_UNROLL):
                r0 = pl.multiple_of((tt * SCAN_UNROLL + k) * nb, nb)
                br = bure_ref[pl.ds(r0, nb), sl]
                bi = buim_ref[pl.ds(r0, nb), sl]
                xr, xi = ar * xr - ai * xi + br, ar * xi + ai * xr + bi
                bure_ref[pl.ds(r0, nb), sl] = xr
                buim_ref[pl.ds(r0, nb), sl] = xi
            return xr, xi

        xr, xi = lax.fori_loop(0, tc // SCAN_UNROLL, steps, (sre_ref[:, sl], sim_ref[:, sl]))
        sre_ref[:, sl] = xr
        sim_ref[:, sl] = xi

    for gb in range(nblk):
        xs = jnp.concatenate([bure_ref[:, gb * sb:(gb + 1) * sb], buim_ref[:, gb * sb:(gb + 1) * sb]], axis=1)
        y_blk = jnp.dot(xs.astype(BF16), wc_ref[gb], preferred_element_type=F32)
        for jj in range(cb // LANES):
            tm_ref[gb * (cb // LANES) + jj] = y_blk[:, jj * LANES:(jj + 1) * LANES]
    y_bm = jnp.concatenate(
        [jnp.concatenate([tm_ref[j, pl.ds(b, tc, stride=nb), :] for j in range(nslab)], axis=1) for b in range(nb)],
        axis=0)
    y = y_bm + d_ref[...] * u_ref[...].reshape(rows, ds)
    z = jax.nn.gelu(y)
    gate = jax.nn.sigmoid(jnp.dot(z.astype(BF16), wglu_ref[...], preferred_element_type=F32))
    o_ref[...] = (z * gate).reshape(nb, tc, ds).astype(o_ref.dtype)


def _ssm(u, a_re, a_im, wb_re, wb_im, wc, d_skip, w_glu, tc=64, lane_chunk=512):
    nb, seq, ds = u.shape
    rows = nb * tc
    n_state = a_re.shape[1]
    assert tc % SCAN_UNROLL == 0
    const = lambda *shape: pl.BlockSpec(shape, lambda c: (0,) * len(shape))
    return pl.pallas_call(
        functools.partial(_ssm_body, tc=tc, lane_chunk=lane_chunk),
        grid=(seq // tc,),
        in_specs=[pl.BlockSpec((nb, tc, ds), lambda c: (0, c, 0)),
                  const(1, n_state), const(1, n_state),
                  const(*wb_re.shape), const(*wb_im.shape), const(*wc.shape),
                  const(1, ds), const(ds, ds)],
        out_specs=pl.BlockSpec((nb, tc, ds), lambda c: (0, c, 0)),
        out_shape=jax.ShapeDtypeStruct((nb, seq, ds), BF16),
        scratch_shapes=[pltpu.VMEM((ds // LANES, rows, LANES), F32),
                        pltpu.VMEM((rows, n_state), F32), pltpu.VMEM((rows, n_state), F32),
                        pltpu.VMEM((nb, n_state), F32), pltpu.VMEM((nb, n_state), F32)],
        compiler_params=_cparams(("arbitrary",)),
        name="s5_mixer",
    )(u, a_re, a_im, wb_re, wb_im, wc, d_skip.reshape(1, ds), w_glu)


def _ssm_params(lambda_re, lambda_im, log_dt, b_re, b_im, c_re, c_im):
    g, p = lambda_re.shape
    h = b_re.shape[2]
    nblk = g // GROUPS_PER_BLOCK
    dt = jnp.exp(log_dt.astype(F32))[:, None]
    lr = lambda_re.astype(F32)
    li = lambda_im.astype(F32)
    mag = jnp.exp(lr * dt)
    abar_re, abar_im = mag * jnp.cos(li * dt), mag * jnp.sin(li * dt)
    nr, ni = abar_re - 1.0, abar_im
    den = lr * lr + li * li
    coef_re = (nr * lr + ni * li) / den
    coef_im = (ni * lr - nr * li) / den
    bbar_re = coef_re[..., None] * b_re - coef_im[..., None] * b_im
    bbar_im = coef_re[..., None] * b_im + coef_im[..., None] * b_re
    eye = jnp.eye(GROUPS_PER_BLOCK, dtype=F32)

    def pack_b(bb):
        bb = bb.reshape(nblk, GROUPS_PER_BLOCK, p, h)
        return jnp.einsum("bgph,gk->bghkp", bb, eye).reshape(nblk, GROUPS_PER_BLOCK * h, GROUPS_PER_BLOCK * p)

    def pack_c(cc):
        cc = cc.reshape(nblk, GROUPS_PER_BLOCK, h, p)
        return jnp.einsum("bghp,gk->bgpkh", cc, eye).reshape(nblk, GROUPS_PER_BLOCK * p, GROUPS_PER_BLOCK * h)

    wc = jnp.concatenate([pack_c(c_re.astype(F32)), -pack_c(c_im.astype(F32))], axis=1)
    return (abar_re.reshape(1, g * p), abar_im.reshape(1, g * p),
            pack_b(bbar_re).astype(BF16), pack_b(bbar_im).astype(BF16), wc.astype(BF16))


def _attn_body(q_ref, k_ref, v_ref, tri_ref, o_ref, acc_ref, rs_ref, *, tq, nhs):
    qi = pl.program_id(2)
    tri = tri_ref[...]
    row = lax.broadcasted_iota(I32, (nhs * tq, tq), 0) % tq
    col = lax.broadcasted_iota(I32, (nhs * tq, tq), 1)
    causal = col < row

    def sweep(j, masked):
        k0 = pl.multiple_of(j * tq, tq)
        heads = [slice(h * HEAD_DIM, (h + 1) * HEAD_DIM) for h in range(nhs)]
        z = jnp.concatenate(
            [lax.dot_general(q_ref[0, :, hs], k_ref[0, pl.ds(k0, tq), hs], (((1,), (1,)), ((), ())),
                             preferred_element_type=F32) for hs in heads], axis=0)
        sp = jnp.maximum(z, 0.0) + jnp.log(1.0 + jnp.exp2(-jnp.abs(z))) * LOG2_E
        spm = jnp.where(causal, sp, 0.0) if masked else sp
        sums = jnp.dot(spm.astype(BF16), tri, preferred_element_type=F32)
        rsum = rs_ref[...]
        w = jnp.exp2(z - (sp + sums[:, :tq] + jnp.concatenate([rsum] * (tq // LANES), axis=1)))
        if masked:
            w = jnp.where(causal, w, 0.0)
        wb = w.astype(BF16)
        for h, hs in enumerate(heads):
            acc_ref[h] += jnp.dot(wb[h * tq:(h + 1) * tq], v_ref[0, pl.ds(k0, tq), hs], preferred_element_type=F32)
        rsum = rsum + sums[:, tq:]
        rs_ref[...] = rsum
        return (jnp.min(rsum) < -EXP_ZERO_BELOW * LOG2_E).astype(I32)

    acc_ref[...] = jnp.zeros_like(acc_ref)
    rs_ref[...] = jnp.zeros_like(rs_ref)
    live = sweep(qi, True)

    def cond(c):
        return (c[0] >= 0) & (c[1] > 0)

    def body(c):
        return c[0] - 1, sweep(c[0], False)

    lax.while_loop(cond, body, (qi - 1, live))
    for h in range(nhs):
        o_ref[0, :, h * HEAD_DIM:(h + 1) * HEAD_DIM] = acc_ref[h].astype(o_ref.dtype)


def _attention(q, k, v, tq=256, nhs=4):
    nb, seq, da = q.shape
    nh = da // HEAD_DIM
    wd = nhs * HEAD_DIM
    r = jnp.arange(tq)
    tri = jnp.concatenate([(r[:, None] > r[None, :]).astype(BF16), jnp.ones((tq, LANES), BF16)], axis=1)
    return pl.pallas_call(
        functools.partial(_attn_body, tq=tq, nhs=nhs),
        grid=(nb, nh // nhs, seq // tq),
        in_specs=[pl.BlockSpec((1, tq, wd), lambda b, h, i: (b, i, h)),
                  pl.BlockSpec((1, seq, wd), lambda b, h, i: (b, 0, h)),
                  pl.BlockSpec((1, seq, wd), lambda b, h, i: (b, 0, h)),
                  pl.BlockSpec((tq, tq + LANES), lambda b, h, i: (0, 0))],
        out_specs=pl.BlockSpec((1, tq, wd), lambda b, h, i: (b, i, h)),
        out_shape=jax.ShapeDtypeStruct((nb, seq, da), BF16),
        scratch_shapes=[pltpu.VMEM((nhs, tq, HEAD_DIM), F32), pltpu.VMEM((nhs * tq, LANES), F32)],
        compiler_params=_cparams(("arbitrary", "arbitrary", "arbitrary")),
        name="stick_attention",
    )(q, k, v, tri)


def _merge_body(ys_ref, ya_ref, gs_ref, ga_ref, ws_ref, wa_ref, o_ref, wsb_ref, wab_ref):
    @pl.when(pl.program_id(1) == 0)
    def _():
        wsb_ref[...] = ws_ref[...].astype(BF16)
        wab_ref[...] = wa_ref[...].astype(BF16)

    ps = jnp.dot(ys_ref[...], wsb_ref[...], preferred_element_type=F32)
    pa = jnp.dot(ya_ref[...], wab_ref[...], preferred_element_type=F32)
    o_ref[...] = (gs_ref[...].astype(F32) * ps + ga_ref[...].astype(F32) * pa).astype(o_ref.dtype)


def _merge(y_ssm, y_att, gates, w_s, w_a, tm=1024, tn=1024):
    t, ks = y_ssm.shape
    d = w_s.shape[1]
    nj = d // tn
    return pl.pallas_call(
        _merge_body,
        grid=(nj, t // tm),
        in_specs=[pl.BlockSpec((tm, ks), lambda j, i: (i, 0)),
                  pl.BlockSpec((tm, y_att.shape[1]), lambda j, i: (i, 0)),
                  pl.BlockSpec((tm, tn), lambda j, i: (i, j)),
                  pl.BlockSpec((tm, tn), lambda j, i: (i, nj + j)),
                  pl.BlockSpec((ks, tn), lambda j, i: (0, j)),
                  pl.BlockSpec((w_a.shape[0], tn), lambda j, i: (0, j))],
        out_specs=pl.BlockSpec((tm, tn), lambda j, i: (i, j)),
        out_shape=jax.ShapeDtypeStruct((t, d), BF16),
        scratch_shapes=[pltpu.VMEM((ks, tn), BF16), pltpu.VMEM((w_a.shape[0], tn), BF16)],
        compiler_params=_cparams(("arbitrary", "arbitrary")),
        name="branch_merge",
    )(y_ssm, y_att, gates, gates, w_s, w_a)


def _lane_min_index(mask, lane):
    return jnp.min(jnp.where(mask, lane, float(ROUTE_LANES)), axis=1, keepdims=True)


def _outproj_body(x_ref, m_ref, w_ref, g_ref, wrc_ref, br_ref,
                  x1_ref, hp_ref, rr_ref, rt_ref):
    x1 = x_ref[...] + jnp.dot(m_ref[...], w_ref[...], preferred_element_type=F32)
    x1_ref[...] = x1
    ms = jnp.mean(x1 * x1, axis=-1, keepdims=True)
    h2 = x1 * lax.rsqrt(ms + EPS) * g_ref[...]
    hp_ref[...] = h2
    hb = h2.astype(BF16)

    hl = (h2 - hb.astype(F32)).astype(BF16)
    p = jnp.dot(hb, wrc_ref[...], preferred_element_type=F32)
    logits = (p[:, :ROUTE_LANES] + p[:, ROUTE_LANES:]
              + jnp.dot(hl, wrc_ref[:, :ROUTE_LANES], preferred_element_type=F32)) + br_ref[...]
    tm = logits.shape[0]
    lane = lax.broadcasted_iota(I32, (tm, ROUTE_LANES), 1).astype(F32)
    is_group = (lane >= N_EXPERTS) & (lane < N_EXPERTS + N_EXPERT_GROUPS)
    gl = jnp.where(is_group, logits, NEG_BIG)
    gmax = jnp.max(gl, axis=1, keepdims=True)
    gidx = _lane_min_index(gl == gmax, lane) - N_EXPERTS
    g_top = 1.0 / jnp.sum(jnp.exp(gl - gmax), axis=1, keepdims=True)
    lo = gidx * EXPERTS_PER_GROUP
    in_grp = (lane >= lo) & (lane < lo + EXPERTS_PER_GROUP)
    el = jnp.where(in_grp, logits, NEG_BIG)
    m1 = jnp.max(el, axis=1, keepdims=True)
    i1 = _lane_min_index(el == m1, lane)
    el2 = jnp.where(lane == i1, NEG_BIG, el)
    m2 = jnp.max(el2, axis=1, keepdims=True)
    i2 = _lane_min_index(el2 == m2, lane)
    dlt = jnp.exp(m2 - m1)
    w1 = 1.0 / (1.0 + dlt)
    w2 = dlt * w1
    route = jnp.where(lane == 0, i1,
                      jnp.where(lane == 1, i2,
                                jnp.where(lane == 2, g_top * w1,
                                          jnp.where(lane == 3, g_top * w2, 0.0))))
    rr_ref[...] = route
    rt_ref[...] = jnp.transpose(route)[0:8, :]


def _outproj(x2, merged, w_out, g, wr, br, tm=512):
    t, d = x2.shape
    wr_hi = wr.astype(BF16)
    wr_cat = jnp.concatenate([wr_hi, (wr - wr_hi.astype(F32)).astype(BF16)], axis=1)
    const = lambda *shape: pl.BlockSpec(shape, lambda i: (0,) * len(shape))
    return pl.pallas_call(
        _outproj_body,
        grid=(t // tm,),
        in_specs=[pl.BlockSpec((tm, d), lambda i: (i, 0)), pl.BlockSpec((tm, d), lambda i: (i, 0)),
                  const(d, d), const(1, d), const(d, 2 * ROUTE_LANES), const(1, ROUTE_LANES)],
        out_specs=[pl.BlockSpec((tm, d), lambda i: (i, 0)),
                   pl.BlockSpec((tm, d), lambda i: (i, 0)),
                   pl.BlockSpec((tm, ROUTE_LANES), lambda i: (i, 0)),
                   pl.BlockSpec((8, tm), lambda i: (0, i))],
        out_shape=[jax.ShapeDtypeStruct((t, d), F32),
                   jax.ShapeDtypeStruct((t, d), F32),
                   jax.ShapeDtypeStruct((t, ROUTE_LANES), F32),
                   jax.ShapeDtypeStruct((8, t), F32)],
        compiler_params=_cparams(("arbitrary",)),
        name="outproj_router",
    )(x2, merged, w_out, g.reshape(1, d), wr_cat, br)


def _count_body(ids_ref, cnt_ref):
    nc, _, c = ids_ref.shape
    eidx = lax.broadcasted_iota(I32, (N_EXPERTS, c), 0)

    def step(i, acc):
        return acc + jnp.where(ids_ref[i] == eidx, 1.0, 0.0)

    acc = lax.fori_loop(0, nc, step, jnp.zeros((N_EXPERTS, c), F32))
    tot = jnp.sum(acc, axis=1, keepdims=True)
    cnt_ref[...] = jnp.broadcast_to(tot, cnt_ref.shape).astype(I32)


def _expert_counts(ids3):
    return pl.pallas_call(
        _count_body,
        out_shape=jax.ShapeDtypeStruct((N_EXPERTS, 128), I32),
        name="expert_counts",
    )(ids3)


def _pos_body(ids_ref, start_ref, incl_ref, pos_ref):
    nc, _, c = ids_ref.shape
    eidx = lax.broadcasted_iota(I32, (N_EXPERTS, c), 0)
    incl = incl_ref[...]

    def step(i, carry):
        onehot = ids_ref[i] == eidx
        cum = jnp.dot(jnp.where(onehot, 1.0, 0.0).astype(BF16), incl, preferred_element_type=F32)
        val = jnp.where(onehot, cum - 1.0 + carry, 0.0)
        pos_ref[i] = jnp.sum(val, axis=0, keepdims=True).astype(I32)
        return carry + cum[:, c - 1:c]

    lax.fori_loop(0, nc, step, start_ref[:, 0:1].astype(F32))


def _sorted_positions(ids3, starts):
    nc, _, c = ids3.shape
    r = jnp.arange(c)
    incl = (r[:, None] <= r[None, :]).astype(BF16)
    start_b = jnp.broadcast_to(starts.astype(I32)[:, None], (N_EXPERTS, 128))
    return pl.pallas_call(
        _pos_body,
        out_shape=jax.ShapeDtypeStruct((nc, 1, c), I32),
        name="sorted_positions",
    )(ids3, start_b, incl)


def _dispatch_body(pos_ref, src_ref, dst_ref, sem, *, n_tok):
    tm = src_ref.shape[0]
    base = pl.program_id(0) * tm

    def issue(i, carry):
        for k in range(DMA_UNROLL):
            j = i * DMA_UNROLL + k
            row = src_ref.at[pl.ds(j, 1), :]
            pltpu.make_async_copy(row, dst_ref.at[pl.ds(pos_ref[base + j], 1), :], sem.at[0]).start()
            pltpu.make_async_copy(row, dst_ref.at[pl.ds(pos_ref[n_tok + base + j], 1), :], sem.at[1]).start()
        return carry

    lax.fori_loop(0, tm // DMA_UNROLL, issue, 0)
    for s in range(2):
        pltpu.make_async_copy(src_ref, dst_ref.at[pl.ds(0, tm), :], sem.at[s]).wait()


def _dispatch(pos_flat, hp, tm=256):
    n = pos_flat.shape[0]
    t, d = hp.shape
    assert t % tm == 0 and tm % DMA_UNROLL == 0
    return pl.pallas_call(
        functools.partial(_dispatch_body, n_tok=t),
        grid_spec=pltpu.PrefetchScalarGridSpec(
            num_scalar_prefetch=1, grid=(t // tm,),
            in_specs=[pl.BlockSpec((tm, d), lambda i, p: (i, 0))],
            out_specs=pl.BlockSpec(memory_space=pl.ANY),
            scratch_shapes=[pltpu.SemaphoreType.DMA((2,))]),
        out_shape=jax.ShapeDtypeStruct((n, d), hp.dtype),
        compiler_params=_cparams(("arbitrary",)),
        name="dispatch",
    )(pos_flat, hp)


def _gmm_body(tile_ref, exp_ref, nxt_ref, lo_ref, hi_ref, x_ref, wg_ref, wu_ref, wd_ref, o_ref,
              sg, su, sd, wgs, wus, wds, sem):
    w = pl.program_id(0)
    prev = jnp.maximum(w - 1, 0)
    new_expert = (w == 0) | (exp_ref[w] != exp_ref[prev])
    first_of_tile = (w == 0) | (tile_ref[w] != tile_ref[prev])
    lo = lo_ref[w]
    hi = hi_ref[w]

    def stage(e):
        return (pltpu.make_async_copy(wg_ref.at[e], sg, sem.at[0]),
                pltpu.make_async_copy(wu_ref.at[e], su, sem.at[1]),
                pltpu.make_async_copy(wd_ref.at[e], sd, sem.at[2]))

    @pl.when(w == 0)
    def _():
        for cp in stage(exp_ref[0]):
            cp.start()

    @pl.when(new_expert)
    def _():
        for cp in stage(exp_ref[w]):
            cp.wait()
        wgs[...] = sg[...].astype(BF16)
        wus[...] = su[...].astype(BF16)
        wds[...] = sd[...].astype(BF16)

        @pl.when(nxt_ref[w] >= 0)
        def _():
            for cp in stage(nxt_ref[w]):
                cp.start()

    @pl.when(hi > lo)
    def _():
        xx = x_ref[...].astype(BF16)
        g = jnp.dot(xx, wgs[...], preferred_element_type=F32)
        u = jnp.dot(xx, wus[...], preferred_element_type=F32)
        hid = (g * jax.nn.sigmoid(g)) * u
        y = jnp.dot(hid.astype(BF16), wds[...], preferred_element_type=F32)
        rows = lax.broadcasted_iota(I32, (y.shape[0], 1), 0)
        y = jnp.where((rows >= lo) & (rows < hi), y, 0.0)

        @pl.when(first_of_tile)
        def _():
            o_ref[...] = y

        @pl.when(jnp.logical_not(first_of_tile))
        def _():
            o_ref[...] += y


def _gmm(meta, xs, w_gate, w_up, w_down, tm):
    tile_id, expert_id, next_expert, row_lo, row_hi = meta
    n, _ = xs.shape
    e, d, f = w_gate.shape
    nw = tile_id.shape[0]
    hbm = pl.BlockSpec(memory_space=pl.ANY)
    return pl.pallas_call(
        _gmm_body,
        grid_spec=pltpu.PrefetchScalarGridSpec(
            num_scalar_prefetch=5, grid=(nw,),
            in_specs=[pl.BlockSpec((tm, d), lambda w, ti, ex, nx, lo, hi: (ti[w], 0)), hbm, hbm, hbm],
            out_specs=pl.BlockSpec((tm, d), lambda w, ti, ex, nx, lo, hi: (ti[w], 0)),
            scratch_shapes=[pltpu.VMEM((d, f), F32), pltpu.VMEM((d, f), F32), pltpu.VMEM((f, d), F32),
                            pltpu.VMEM((d, f), BF16), pltpu.VMEM((d, f), BF16), pltpu.VMEM((f, d), BF16),
                            pltpu.SemaphoreType.DMA((3,))]),
        out_shape=jax.ShapeDtypeStruct((n, d), F32),
        compiler_params=_cparams(("arbitrary",)),
        name="expert_gmm",
    )(tile_id, expert_id, next_expert, row_lo, row_hi, xs, w_gate, w_up, w_down)


def _gmm_metadata(counts, n_rows, tm):
    nt = n_rows // tm
    nw = nt + N_EXPERTS - 1
    ends = jnp.cumsum(counts)
    starts = ends - counts
    first_tile = starts // tm
    n_items = jnp.where(counts > 0, (ends - 1) // tm - first_tile + 1, 0)
    item_end = jnp.cumsum(n_items)
    item_start = item_end - n_items
    total = item_end[-1]
    w = jnp.arange(nw, dtype=I32)
    wc = jnp.minimum(w, total - 1)
    ex = jnp.sum((item_end[None, :] <= wc[:, None]).astype(I32), axis=1)
    tile = first_tile[ex] + (wc - item_start[ex])
    lo = jnp.maximum(starts[ex], tile * tm) - tile * tm
    hi = jnp.minimum(ends[ex], (tile + 1) * tm) - tile * tm
    valid = w < total
    lo = jnp.where(valid, lo, 0)
    hi = jnp.where(valid, hi, 0)
    run_end = item_end[ex]
    nxt = jnp.where(run_end < total, ex[jnp.minimum(run_end, nw - 1)], -1)
    return tile.astype(I32), ex, nxt.astype(I32), lo.astype(I32), hi.astype(I32)


def _combine_body(pos_ref, x1_ref, rr_ref, ys_ref, o_ref, a0, a1, b0, b1, sem, *, n_tok):
    i = pl.program_id(0)
    nsteps = pl.num_programs(0)
    tm = x1_ref.shape[0]

    def gather(tile, bufs, sems):
        base = tile * tm

        def issue(jj, carry):
            for k in range(DMA_UNROLL):
                j = jj * DMA_UNROLL + k
                pltpu.make_async_copy(ys_ref.at[pl.ds(pos_ref[base + j], 1), :],
                                      bufs[0].at[pl.ds(j, 1), :], sem.at[sems[0]]).start()
                pltpu.make_async_copy(ys_ref.at[pl.ds(pos_ref[n_tok + base + j], 1), :],
                                      bufs[1].at[pl.ds(j, 1), :], sem.at[sems[1]]).start()
            return carry

        lax.fori_loop(0, tm // DMA_UNROLL, issue, 0)

    def step(cur, cur_sems, nxt, nxt_sems):
        @pl.when(i == 0)
        def _():
            gather(i, cur, cur_sems)

        @pl.when(i + 1 < nsteps)
        def _():
            gather(i + 1, nxt, nxt_sems)

        for buf, s in zip(cur, cur_sems):
            pltpu.make_async_copy(ys_ref.at[pl.ds(0, tm), :], buf, sem.at[s]).wait()
        rr = rr_ref[...]
        o_ref[...] = x1_ref[...] + rr[:, 2:3] * cur[0][...] + rr[:, 3:4] * cur[1][...]

    @pl.when(i % 2 == 0)
    def _():
        step((a0, a1), (0, 1), (b0, b1), (2, 3))

    @pl.when(i % 2 == 1)
    def _():
        step((b0, b1), (2, 3), (a0, a1), (0, 1))


def _combine(pos_flat, x1, route_rows, ys, tm=256):
    t, d = x1.shape
    assert tm % DMA_UNROLL == 0
    buf = pltpu.VMEM((tm, d), F32)
    return pl.pallas_call(
        functools.partial(_combine_body, n_tok=t),
        grid_spec=pltpu.PrefetchScalarGridSpec(
            num_scalar_prefetch=1, grid=(t // tm,),
            in_specs=[pl.BlockSpec((tm, d), lambda i, p: (i, 0)),
                      pl.BlockSpec((tm, ROUTE_LANES), lambda i, p: (i, 0)),
                      pl.BlockSpec(memory_space=pl.ANY)],
            out_specs=pl.BlockSpec((tm, d), lambda i, p: (i, 0)),
            scratch_shapes=[buf, buf, buf, buf, pltpu.SemaphoreType.DMA((4,))]),
        out_shape=jax.ShapeDtypeStruct((t, d), F32),
        compiler_params=_cparams(("arbitrary",)),
        name="combine",
    )(pos_flat, x1, route_rows, ys)


def _moe(x1, hp, route_rows, route_t, w_gate, w_up, w_down, gmm_tm=256, sort_chunk=512):
    t = x1.shape[0]
    n = 2 * t
    ids3 = route_t[0:2].astype(I32).reshape(n // sort_chunk, 1, sort_chunk)
    counts = _expert_counts(ids3)[:, 0]
    starts = jnp.cumsum(counts) - counts
    pos_flat = _sorted_positions(ids3, starts).reshape(n)
    xs = _dispatch(pos_flat, hp)
    ys = _gmm(_gmm_metadata(counts, n, gmm_tm), xs, w_gate, w_up, w_down, gmm_tm)
    return _combine(pos_flat, x1, route_rows, ys)


def _layer(x, attn_norm_g, w_in, lambda_re, lambda_im, log_dt, ssm_b_re, ssm_b_im, ssm_c_re, ssm_c_im,
           ssm_d, w_glu, q_norm_g, k_norm_g, w_branch_ssm, w_branch_att, w_out, ffn_norm_g,
           router_group_w, router_group_b, router_expert_w, router_expert_b,
           expert_w_gate, expert_w_up, expert_w_down):
    nb, seq, d = x.shape
    t = nb * seq
    ds = w_glu.shape[0]
    da = N_HEADS * HEAD_DIM
    x2 = x.reshape(t, d)

    h = _rmsnorm(x2, attn_norm_g)
    o = 0
    u = _proj(h, w_in, o, ds, "plain", F32, name="proj_u"); o += ds
    qg = q_norm_g.astype(F32) * (HEAD_DIM ** -0.5 * LOG2_E)
    q = _proj(h, w_in, o, da, "headnorm", BF16, gain=qg, name="proj_q"); o += da
    k = _proj(h, w_in, o, da, "headnorm", BF16, gain=k_norm_g, name="proj_k"); o += da
    v = _proj(h, w_in, o, da, "plain", BF16, name="proj_v"); o += da
    gates = _proj(h, w_in, o, w_in.shape[1] - o, "sigmoid", BF16, name="proj_gates")

    a_re, a_im, wb_re, wb_im, wc = _ssm_params(lambda_re, lambda_im, log_dt, ssm_b_re, ssm_b_im, ssm_c_re, ssm_c_im)
    y_ssm = _ssm(u.reshape(nb, seq, ds), a_re, a_im, wb_re, wb_im, wc, ssm_d.astype(F32), w_glu.astype(BF16))
    y_att = _attention(q.reshape(nb, seq, da), k.reshape(nb, seq, da), v.reshape(nb, seq, da))

    merged = _merge(y_ssm.reshape(t, ds), y_att.reshape(t, da), gates, w_branch_ssm, w_branch_att)

    n_pad = ROUTE_LANES - N_EXPERTS - N_EXPERT_GROUPS
    wr = jnp.concatenate([router_expert_w.astype(F32), router_group_w.astype(F32), jnp.zeros((d, n_pad), F32)], axis=1)
    br = jnp.concatenate([router_expert_b.astype(F32), router_group_b.astype(F32), jnp.zeros((n_pad,), F32)])[None, :]
    x1, hp, route_rows, route_t = _outproj(x2, merged, w_out.astype(BF16), ffn_norm_g, wr, br)

    out = _moe(x1, hp, route_rows, route_t, expert_w_gate, expert_w_up, expert_w_down)
    return out.reshape(nb, seq, d)


def kernel(x, attn_norm_g, w_in, lambda_re, lambda_im, log_dt, ssm_b_re, ssm_b_im, ssm_c_re, ssm_c_im, ssm_d, w_glu, q_norm_g, k_norm_g, w_branch_ssm, w_branch_att, w_out, ffn_norm_g, router_group_w, router_group_b, router_expert_w, router_expert_b, expert_w_gate, expert_w_up, expert_w_down):
    depth = attn_norm_g.shape[0]
    for l in range(depth):
        x = _layer(x, attn_norm_g[l], w_in[l], lambda_re[l], lambda_im[l], log_dt[l], ssm_b_re[l], ssm_b_im[l],
                   ssm_c_re[l], ssm_c_im[l], ssm_d[l], w_glu[l], q_norm_g[l], k_norm_g[l], w_branch_ssm[l],
                   w_branch_att[l], w_out[l], ffn_norm_g[l], router_group_w[l], router_group_b[l],
                   router_expert_w[l], router_expert_b[l], expert_w_gate[l], expert_w_up[l], expert_w_down[l])
    return x
```

```python
import functools

import jax
import jax.numpy as jnp
from jax import lax
from jax.experimental import pallas as pl
from jax.experimental.pallas import tpu as pltpu

F32 = jnp.float32
BF16 = jnp.bfloat16
I32 = jnp.int32
U32 = jnp.uint32

EPS = 1e-6
N_HEADS = 8
HEAD_DIM = 128
SSM_GROUP = 16
SSM_STATE = 64
GROUPS_PER_BLOCK = 16
N_EXPERT_GROUPS = 4
EXPERTS_PER_GROUP = 8
N_EXPERTS = N_EXPERT_GROUPS * EXPERTS_PER_GROUP
LANES = 128
ROUTE_LANES = 128
NEG_BIG = -1e30
DMA_UNROLL = 8
SCAN_UNROLL = 4
EXP_ZERO_BELOW = -104.0
LOG2_E = 1.4426950408889634

VMEM_LIMIT = 56 * 1024 * 1024


def _cparams(sem):
    return pltpu.CompilerParams(dimension_semantics=sem, vmem_limit_bytes=VMEM_LIMIT)


def _proj_body(x_ref, gn_ref, w_ref, g_ref, o_ref, wb_ref, *, mode):
    @pl.when(pl.program_id(1) == 0)
    def _():
        wb_ref[...] = w_ref[...].astype(BF16)

    x = x_ref[...]
    ms = jnp.mean(x * x, axis=-1, keepdims=True)
    h = (x * lax.rsqrt(ms + EPS) * gn_ref[...]).astype(BF16)
    acc = jnp.dot(h, wb_ref[...], preferred_element_type=F32)
    if mode == "plain":
        o_ref[...] = acc.astype(o_ref.dtype)
    elif mode == "sigmoid":
        o_ref[...] = jax.nn.sigmoid(acc).astype(o_ref.dtype)
    else:
        n = acc.shape[1] // HEAD_DIM
        for hh in range(n):
            blk = acc[:, hh * HEAD_DIM:(hh + 1) * HEAD_DIM]
            ms = jnp.mean(blk * blk, axis=-1, keepdims=True)
            o_ref[:, hh * HEAD_DIM:(hh + 1) * HEAD_DIM] = (blk * lax.rsqrt(ms + EPS) * g_ref[...]).astype(o_ref.dtype)


def _proj(x2, norm_g, w, col0, n, mode, out_dtype, gain=None, tm=1024, tn=1024, name="proj"):
    t, k = x2.shape
    assert col0 % tn == 0 and n % tn == 0
    j0 = col0 // tn
    if gain is None:
        gain = jnp.ones((HEAD_DIM,), F32)
    return pl.pallas_call(
        functools.partial(_proj_body, mode=mode),
        grid=(n // tn, t // tm),
        in_specs=[pl.BlockSpec((tm, k), lambda j, i: (i, 0)),
                  pl.BlockSpec((1, k), lambda j, i: (0, 0)),
                  pl.BlockSpec((k, tn), lambda j, i: (0, j0 + j)),
                  pl.BlockSpec((1, HEAD_DIM), lambda j, i: (0, 0))],
        out_specs=pl.BlockSpec((tm, tn), lambda j, i: (i, j)),
        out_shape=jax.ShapeDtypeStruct((t, n), out_dtype),
        scratch_shapes=[pltpu.VMEM((k, tn), BF16)],
        compiler_params=_cparams(("arbitrary", "arbitrary")),
        name=name,
    )(x2, norm_g.reshape(1, k).astype(F32), w, gain.reshape(1, HEAD_DIM).astype(F32))


def _ssm_body(u_ref, are_ref, aim_ref, wbre_ref, wbim_ref, wc_ref, d_ref, wglu_ref,
              o_ref, tm_ref, bure_ref, buim_ref, sre_ref, sim_ref, *, tc, lane_chunk):
    nb = u_ref.shape[0]
    ds = u_ref.shape[2]
    rows = nb * tc
    nslab = ds // LANES
    nblk = wbre_ref.shape[0]
    cb = wbre_ref.shape[1]
    sb = wbre_ref.shape[2]
    n_state = nblk * sb

    @pl.when(pl.program_id(0) == 0)
    def _():
        sre_ref[...] = jnp.zeros_like(sre_ref)
        sim_ref[...] = jnp.zeros_like(sim_ref)

    for b in range(nb):
        for j in range(nslab):
            tm_ref[j, pl.ds(b, tc, stride=nb), :] = u_ref[b, :, j * LANES:(j + 1) * LANES]
    u_tm = jnp.concatenate([tm_ref[j] for j in range(nslab)], axis=1).astype(BF16)
    for gb in range(nblk):
        lhs = u_tm[:, gb * cb:(gb + 1) * cb]
        bure_ref[:, gb * sb:(gb + 1) * sb] = jnp.dot(lhs, wbre_ref[gb], preferred_element_type=F32)
        buim_ref[:, gb * sb:(gb + 1) * sb] = jnp.dot(lhs, wbim_ref[gb], preferred_element_type=F32)

    for lc in range(n_state // lane_chunk):
        sl = slice(lc * lane_chunk, (lc + 1) * lane_chunk)
        ar = jnp.broadcast_to(are_ref[:, sl], (nb, lane_chunk))
        ai = jnp.broadcast_to(aim_ref[:, sl], (nb, lane_chunk))

        def steps(tt, carry, sl=sl, ar=ar, ai=ai):
            xr, xi = carry
            for k in range(SCAN_UNROLL):
                r0 = pl.multiple_of((tt * SCAN_UNROLL + k) * nb, nb)
                br = bure_ref[pl.ds(r0, nb), sl]
                bi = buim_ref[pl.ds(r0, nb), sl]
                xr, xi = ar * xr - ai * xi + br, ar * xi + ai * xr + bi
                bure_ref[pl.ds(r0, nb), sl] = xr
                buim_ref[pl.ds(r0, nb), sl] = xi
            return xr, xi

        xr, xi = lax.fori_loop(0, tc // SCAN_UNROLL, steps, (sre_ref[:, sl], sim_ref[:, sl]))
        sre_ref[:, sl] = xr
        sim_ref[:, sl] = xi

    for gb in range(nblk):
        xs = jnp.concatenate([bure_ref[:, gb * sb:(gb + 1) * sb], buim_ref[:, gb * sb:(gb + 1) * sb]], axis=1)
        y_blk = jnp.dot(xs.astype(BF16), wc_ref[gb], preferred_element_type=F32)
        for jj in range(cb // LANES):
            tm_ref[gb * (cb // LANES) + jj] = y_blk[:, jj * LANES:(jj + 1) * LANES]
    y_bm = jnp.concatenate(
        [jnp.concatenate([tm_ref[j, pl.ds(b, tc, stride=nb), :] for j in range(nslab)], axis=1) for b in range(nb)],
        axis=0)
    y = y_bm + d_ref[...] * u_ref[...].reshape(rows, ds)
    z = jax.nn.gelu(y)
    gate = jax.nn.sigmoid(jnp.dot(z.astype(BF16), wglu_ref[...], preferred_element_type=F32))
    o_ref[...] = (z * gate).reshape(nb, tc, ds).astype(o_ref.dtype)


def _ssm(u, a_re, a_im, wb_re, wb_im, wc, d_skip, w_glu, tc=64, lane_chunk=512):
    nb, seq, ds = u.shape
    rows = nb * tc
    n_state = a_re.shape[1]
    assert tc % SCAN_UNROLL == 0
    const = lambda *shape: pl.BlockSpec(shape, lambda c: (0,) * len(shape))
    return pl.pallas_call(
        functools.partial(_ssm_body, tc=tc, lane_chunk=lane_chunk),
        grid=(seq // tc,),
        in_specs=[pl.BlockSpec((nb, tc, ds), lambda c: (0, c, 0)),
                  const(1, n_state), const(1, n_state),
                  const(*wb_re.shape), const(*wb_im.shape), const(*wc.shape),
                  const(1, ds), const(ds, ds)],
        out_specs=pl.BlockSpec((nb, tc, ds), lambda c: (0, c, 0)),
        out_shape=jax.ShapeDtypeStruct((nb, seq, ds), BF16),
        scratch_shapes=[pltpu.VMEM((ds // LANES, rows, LANES), F32),
                        pltpu.VMEM((rows, n_state), F32), pltpu.VMEM((rows, n_state), F32),
                        pltpu.VMEM((nb, n_state), F32), pltpu.VMEM((nb, n_state), F32)],
        compiler_params=_cparams(("arbitrary",)),
        name="s5_mixer",
    )(u, a_re, a_im, wb_re, wb_im, wc, d_skip.reshape(1, ds), w_glu)


def _ssm_params(lambda_re, lambda_im, log_dt, b_re, b_im, c_re, c_im):
    g, p = lambda_re.shape
    h = b_re.shape[2]
    nblk = g // GROUPS_PER_BLOCK
    dt = jnp.exp(log_dt.astype(F32))[:, None]
    lr = lambda_re.astype(F32)
    li = lambda_im.astype(F32)
    mag = jnp.exp(lr * dt)
    abar_re, abar_im = mag * jnp.cos(li * dt), mag * jnp.sin(li * dt)
    nr, ni = abar_re - 1.0, abar_im
    den = lr * lr + li * li
    coef_re = (nr * lr + ni * li) / den
    coef_im = (ni * lr - nr * li) / den
    bbar_re = coef_re[..., None] * b_re - coef_im[..., None] * b_im
    bbar_im = coef_re[..., None] * b_im + coef_im[..., None] * b_re
    gpb = GROUPS_PER_BLOCK

    def block_diag(m, rows_per_group, cols_per_group):
        r = jnp.arange(gpb * rows_per_group)[:, None] // rows_per_group
        c = jnp.arange(gpb * cols_per_group)[None, :] // cols_per_group
        return jnp.where(r == c, jnp.tile(m, (1, gpb, 1)), 0.0)

    def pack_b(bb):
        m = bb.reshape(nblk, gpb, p, h).transpose(0, 3, 1, 2).reshape(nblk, h, gpb * p)
        return block_diag(m, h, p)

    def pack_c(cc):
        m = cc.reshape(nblk, gpb, h, p).transpose(0, 3, 1, 2).reshape(nblk, p, gpb * h)
        return block_diag(m, p, h)

    wc = jnp.concatenate([pack_c(c_re.astype(F32)), -pack_c(c_im.astype(F32))], axis=1)
    return (abar_re.reshape(1, g * p), abar_im.reshape(1, g * p),
            pack_b(bbar_re).astype(BF16), pack_b(bbar_im).astype(BF16), wc.astype(BF16))


def _attn_body(q_ref, k_ref, v_ref, tri_ref, o_ref, acc_ref, rs_ref, *, tq, nhs):
    qi = pl.program_id(2)
    tri = tri_ref[...]
    row = lax.broadcasted_iota(I32, (nhs * tq, tq), 0) % tq
    col = lax.broadcasted_iota(I32, (nhs * tq, tq), 1)
    causal = col < row

    def sweep(j, masked):
        k0 = pl.multiple_of(j * tq, tq)
        heads = [slice(h * HEAD_DIM, (h + 1) * HEAD_DIM) for h in range(nhs)]
        z = jnp.concatenate(
            [lax.dot_general(q_ref[0, :, hs], k_ref[0, pl.ds(k0, tq), hs], (((1,), (1,)), ((), ())),
                             preferred_element_type=F32) for hs in heads], axis=0)
        sp = jnp.maximum(z, 0.0) + jnp.log(1.0 + jnp.exp2(-jnp.abs(z))) * LOG2_E
        spm = jnp.where(causal, sp, 0.0) if masked else sp
        spb = spm.astype(BF16)
        tail = jnp.dot(spb, tri, preferred_element_type=F32)
        rsum = rs_ref[...]
        w = jnp.exp2(z - (sp + tail + jnp.concatenate([rsum] * (tq // LANES), axis=1)))
        if masked:
            w = jnp.where(causal, w, 0.0)
        wb = w.astype(BF16)
        for h, hs in enumerate(heads):
            acc_ref[h] += jnp.dot(wb[h * tq:(h + 1) * tq], v_ref[0, pl.ds(k0, tq), hs], preferred_element_type=F32)
        total = tail[:, 0:1] + spb[:, 0:1].astype(F32)
        rsum = rsum + jnp.broadcast_to(total, rsum.shape)
        rs_ref[...] = rsum
        return (jnp.min(rsum) < -EXP_ZERO_BELOW * LOG2_E).astype(I32)

    acc_ref[...] = jnp.zeros_like(acc_ref)
    rs_ref[...] = jnp.zeros_like(rs_ref)
    live = sweep(qi, True)

    def cond(c):
        return (c[0] >= 0) & (c[1] > 0)

    def body(c):
        return c[0] - 1, sweep(c[0], False)

    lax.while_loop(cond, body, (qi - 1, live))
    for h in range(nhs):
        o_ref[0, :, h * HEAD_DIM:(h + 1) * HEAD_DIM] = acc_ref[h].astype(o_ref.dtype)


def _attention(q, k, v, tq=256, nhs=8):
    nb, seq, da = q.shape
    nh = da // HEAD_DIM
    wd = nhs * HEAD_DIM
    r = jnp.arange(tq)
    tri = (r[:, None] > r[None, :]).astype(BF16)
    return pl.pallas_call(
        functools.partial(_attn_body, tq=tq, nhs=nhs),
        grid=(nb, nh // nhs, seq // tq),
        in_specs=[pl.BlockSpec((1, tq, wd), lambda b, h, i: (b, i, h)),
                  pl.BlockSpec((1, seq, wd), lambda b, h, i: (b, 0, h)),
                  pl.BlockSpec((1, seq, wd), lambda b, h, i: (b, 0, h)),
                  pl.BlockSpec((tq, tq), lambda b, h, i: (0, 0))],
        out_specs=pl.BlockSpec((1, tq, wd), lambda b, h, i: (b, i, h)),
        out_shape=jax.ShapeDtypeStruct((nb, seq, da), BF16),
        scratch_shapes=[pltpu.VMEM((nhs, tq, HEAD_DIM), F32), pltpu.VMEM((nhs * tq, LANES), F32)],
        compiler_params=_cparams(("arbitrary", "arbitrary", "arbitrary")),
        name="stick_attention",
    )(q, k, v, tri)


def _merge_body(ys_ref, ya_ref, gs_ref, ga_ref, ws_ref, wa_ref, o_ref, wsb_ref, wab_ref):
    @pl.when(pl.program_id(1) == 0)
    def _():
        wsb_ref[...] = ws_ref[...].astype(BF16)
        wab_ref[...] = wa_ref[...].astype(BF16)

    ps = jnp.dot(ys_ref[...], wsb_ref[...], preferred_element_type=F32)
    pa = jnp.dot(ya_ref[...], wab_ref[...], preferred_element_type=F32)
    o_ref[...] = (gs_ref[...].astype(F32) * ps + ga_ref[...].astype(F32) * pa).astype(o_ref.dtype)


def _merge(y_ssm, y_att, gates, w_s, w_a, tm=1024, tn=1024):
    t, ks = y_ssm.shape
    d = w_s.shape[1]
    nj = d // tn
    return pl.pallas_call(
        _merge_body,
        grid=(nj, t // tm),
        in_specs=[pl.BlockSpec((tm, ks), lambda j, i: (i, 0)),
                  pl.BlockSpec((tm, y_att.shape[1]), lambda j, i: (i, 0)),
                  pl.BlockSpec((tm, tn), lambda j, i: (i, j)),
                  pl.BlockSpec((tm, tn), lambda j, i: (i, nj + j)),
                  pl.BlockSpec((ks, tn), lambda j, i: (0, j)),
                  pl.BlockSpec((w_a.shape[0], tn), lambda j, i: (0, j))],
        out_specs=pl.BlockSpec((tm, tn), lambda j, i: (i, j)),
        out_shape=jax.ShapeDtypeStruct((t, d), BF16),
        scratch_shapes=[pltpu.VMEM((ks, tn), BF16), pltpu.VMEM((w_a.shape[0], tn), BF16)],
        compiler_params=_cparams(("arbitrary", "arbitrary")),
        name="branch_merge",
    )(y_ssm, y_att, gates, gates, w_s, w_a)


def _lane_min_index(mask, lane):
    return jnp.min(jnp.where(mask, lane, float(ROUTE_LANES)), axis=1, keepdims=True)


def _outproj_body(x_ref, m_ref, w_ref, g_ref, wrc_ref, br_ref,
                  x1_ref, hp_ref, rr_ref, rt_ref):
    x1 = x_ref[...] + jnp.dot(m_ref[...], w_ref[...], preferred_element_type=F32)
    x1_ref[...] = x1
    ms = jnp.mean(x1 * x1, axis=-1, keepdims=True)
    h2 = x1 * lax.rsqrt(ms + EPS) * g_ref[...]
    hp_ref[...] = h2
    hb = h2.astype(BF16)

    hl = (h2 - hb.astype(F32)).astype(BF16)
    p = jnp.dot(hb, wrc_ref[...], preferred_element_type=F32)
    logits = (p[:, :ROUTE_LANES] + p[:, ROUTE_LANES:]
              + jnp.dot(hl, wrc_ref[:, :ROUTE_LANES], preferred_element_type=F32)) + br_ref[...]
    tm = logits.shape[0]
    lane = lax.broadcasted_iota(I32, (tm, ROUTE_LANES), 1).astype(F32)
    is_group = (lane >= N_EXPERTS) & (lane < N_EXPERTS + N_EXPERT_GROUPS)
    gl = jnp.where(is_group, logits, NEG_BIG)
    gmax = jnp.max(gl, axis=1, keepdims=True)
    gidx = _lane_min_index(gl == gmax, lane) - N_EXPERTS
    g_top = 1.0 / jnp.sum(jnp.exp(gl - gmax), axis=1, keepdims=True)
    lo = gidx * EXPERTS_PER_GROUP
    in_grp = (lane >= lo) & (lane < lo + EXPERTS_PER_GROUP)
    el = jnp.where(in_grp, logits, NEG_BIG)
    m1 = jnp.max(el, axis=1, keepdims=True)
    i1 = _lane_min_index(el == m1, lane)
    el2 = jnp.where(lane == i1, NEG_BIG, el)
    m2 = jnp.max(el2, axis=1, keepdims=True)
    i2 = _lane_min_index(el2 == m2, lane)
    dlt = jnp.exp(m2 - m1)
    w1 = 1.0 / (1.0 + dlt)
    w2 = dlt * w1
    route = jnp.where(lane == 0, i1,
                      jnp.where(lane == 1, i2,
                                jnp.where(lane == 2, g_top * w1,
                                          jnp.where(lane == 3, g_top * w2, 0.0))))
    rr_ref[...] = route
    rt_ref[...] = jnp.transpose(route)[0:8, :]


def _outproj(x2, merged, w_out, g, wr, br, tm=512):
    t, d = x2.shape
    wr_hi = wr.astype(BF16)
    wr_cat = jnp.concatenate([wr_hi, (wr - wr_hi.astype(F32)).astype(BF16)], axis=1)
    const = lambda *shape: pl.BlockSpec(shape, lambda i: (0,) * len(shape))
    return pl.pallas_call(
        _outproj_body,
        grid=(t // tm,),
        in_specs=[pl.BlockSpec((tm, d), lambda i: (i, 0)), pl.BlockSpec((tm, d), lambda i: (i, 0)),
                  const(d, d), const(1, d), const(d, 2 * ROUTE_LANES), const(1, ROUTE_LANES)],
        out_specs=[pl.BlockSpec((tm, d), lambda i: (i, 0)),
                   pl.BlockSpec((tm, d), lambda i: (i, 0)),
                   pl.BlockSpec((tm, ROUTE_LANES), lambda i: (i, 0)),
                   pl.BlockSpec((8, tm), lambda i: (0, i))],
        out_shape=[jax.ShapeDtypeStruct((t, d), F32),
                   jax.ShapeDtypeStruct((t, d), F32),
                   jax.ShapeDtypeStruct((t, ROUTE_LANES), F32),
                   jax.ShapeDtypeStruct((8, t), F32)],
        compiler_params=_cparams(("arbitrary",)),
        name="outproj_router",
    )(x2, merged, w_out, g.reshape(1, d), wr_cat, br)


def _count_body(ids_ref, cnt_ref):
    nc, _, c = ids_ref.shape
    eidx = lax.broadcasted_iota(I32, (N_EXPERTS, c), 0)

    def step(i, acc):
        return acc + jnp.where(ids_ref[i] == eidx, 1.0, 0.0)

    acc = lax.fori_loop(0, nc, step, jnp.zeros((N_EXPERTS, c), F32))
    tot = jnp.sum(acc, axis=1, keepdims=True)
    cnt_ref[...] = jnp.broadcast_to(tot, cnt_ref.shape).astype(I32)


def _expert_counts(ids3):
    return pl.pallas_call(
        _count_body,
        out_shape=jax.ShapeDtypeStruct((N_EXPERTS, 128), I32),
        name="expert_counts",
    )(ids3)


def _pos_body(ids_ref, start_ref, incl_ref, pos_ref):
    nc, _, c = ids_ref.shape
    eidx = lax.broadcasted_iota(I32, (N_EXPERTS, c), 0)
    incl = incl_ref[...]

    def step(i, carry):
        onehot = ids_ref[i] == eidx
        cum = jnp.dot(jnp.where(onehot, 1.0, 0.0).astype(BF16), incl, preferred_element_type=F32)
        val = jnp.where(onehot, cum - 1.0 + carry, 0.0)
        pos_ref[i] = jnp.sum(val, axis=0, keepdims=True).astype(I32)
        return carry + cum[:, c - 1:c]

    lax.fori_loop(0, nc, step, start_ref[:, 0:1].astype(F32))


def _sorted_positions(ids3, starts):
    nc, _, c = ids3.shape
    r = jnp.arange(c)
    incl = (r[:, None] <= r[None, :]).astype(BF16)
    start_b = jnp.broadcast_to(starts.astype(I32)[:, None], (N_EXPERTS, 128))
    return pl.pallas_call(
        _pos_body,
        out_shape=jax.ShapeDtypeStruct((nc, 1, c), I32),
        name="sorted_positions",
    )(ids3, start_b, incl)


def _dispatch_body(pos_ref, src_ref, dst_ref, sem, *, n_tok):
    tm = src_ref.shape[0]
    base = pl.program_id(0) * tm

    def issue(i, carry):
        for k in range(DMA_UNROLL):
            j = i * DMA_UNROLL + k
            row = src_ref.at[pl.ds(j, 1), :]
            pltpu.make_async_copy(row, dst_ref.at[pl.ds(pos_ref[base + j], 1), :], sem.at[0]).start()
            pltpu.make_async_copy(row, dst_ref.at[pl.ds(pos_ref[n_tok + base + j], 1), :], sem.at[1]).start()
        return carry

    lax.fori_loop(0, tm // DMA_UNROLL, issue, 0)
    for s in range(2):
        pltpu.make_async_copy(src_ref, dst_ref.at[pl.ds(0, tm), :], sem.at[s]).wait()


def _dispatch(pos_flat, hp, tm=256):
    n = pos_flat.shape[0]
    t, d = hp.shape
    assert t % tm == 0 and tm % DMA_UNROLL == 0
    return pl.pallas_call(
        functools.partial(_dispatch_body, n_tok=t),
        grid_spec=pltpu.PrefetchScalarGridSpec(
            num_scalar_prefetch=1, grid=(t // tm,),
            in_specs=[pl.BlockSpec((tm, d), lambda i, p: (i, 0))],
            out_specs=pl.BlockSpec(memory_space=pl.ANY),
            scratch_shapes=[pltpu.SemaphoreType.DMA((2,))]),
        out_shape=jax.ShapeDtypeStruct((n, d), hp.dtype),
        compiler_params=_cparams(("arbitrary",)),
        name="dispatch",
    )(pos_flat, hp)


def _gmm_body(tile_ref, exp_ref, nxt_ref, lo_ref, hi_ref, x_ref, wg_ref, wu_ref, wd_ref, o_ref,
              sg, su, sd, wgs, wus, wds, sem):
    w = pl.program_id(0)
    prev = jnp.maximum(w - 1, 0)
    new_expert = (w == 0) | (exp_ref[w] != exp_ref[prev])
    first_of_tile = (w == 0) | (tile_ref[w] != tile_ref[prev])
    lo = lo_ref[w]
    hi = hi_ref[w]

    def stage(e):
        return (pltpu.make_async_copy(wg_ref.at[e], sg, sem.at[0]),
                pltpu.make_async_copy(wu_ref.at[e], su, sem.at[1]),
                pltpu.make_async_copy(wd_ref.at[e], sd, sem.at[2]))

    @pl.when(w == 0)
    def _():
        for cp in stage(exp_ref[0]):
            cp.start()

    @pl.when(new_expert)
    def _():
        for cp in stage(exp_ref[w]):
            cp.wait()
        wgs[...] = sg[...].astype(BF16)
        wus[...] = su[...].astype(BF16)
        wds[...] = sd[...].astype(BF16)

        @pl.when(nxt_ref[w] >= 0)
        def _():
            for cp in stage(nxt_ref[w]):
                cp.start()

    @pl.when(hi > lo)
    def _():
        xx = x_ref[...].astype(BF16)
        g = jnp.dot(xx, wgs[...], preferred_element_type=F32)
        u = jnp.dot(xx, wus[...], preferred_element_type=F32)
        hid = (g * jax.nn.sigmoid(g)) * u
        y = jnp.dot(hid.astype(BF16), wds[...], preferred_element_type=F32)
        rows = lax.broadcasted_iota(I32, (y.shape[0], 1), 0)
        y = jnp.where((rows >= lo) & (rows < hi), y, 0.0)

        @pl.when(first_of_tile)
        def _():
            o_ref[...] = y

        @pl.when(jnp.logical_not(first_of_tile))
        def _():
            o_ref[...] += y


def _gmm(meta, xs, w_gate, w_up, w_down, tm):
    tile_id, expert_id, next_expert, row_lo, row_hi = meta
    n, _ = xs.shape
    e, d, f = w_gate.shape
    nw = tile_id.shape[0]
    hbm = pl.BlockSpec(memory_space=pl.ANY)
    return pl.pallas_call(
        _gmm_body,
        grid_spec=pltpu.PrefetchScalarGridSpec(
            num_scalar_prefetch=5, grid=(nw,),
            in_specs=[pl.BlockSpec((tm, d), lambda w, ti, ex, nx, lo, hi: (ti[w], 0)), hbm, hbm, hbm],
            out_specs=pl.BlockSpec((tm, d), lambda w, ti, ex, nx, lo, hi: (ti[w], 0)),
            scratch_shapes=[pltpu.VMEM((d, f), F32), pltpu.VMEM((d, f), F32), pltpu.VMEM((f, d), F32),
                            pltpu.VMEM((d, f), BF16), pltpu.VMEM((d, f), BF16), pltpu.VMEM((f, d), BF16),
                            pltpu.SemaphoreType.DMA((3,))]),
        out_shape=jax.ShapeDtypeStruct((n, d), F32),
        compiler_params=_cparams(("arbitrary",)),
        name="expert_gmm",
    )(tile_id, expert_id, next_expert, row_lo, row_hi, xs, w_gate, w_up, w_down)


def _gmm_metadata(counts, n_rows, tm):
    nt = n_rows // tm
    nw = nt + N_EXPERTS - 1
    ends = jnp.cumsum(counts)
    starts = ends - counts
    first_tile = starts // tm
    n_items = jnp.where(counts > 0, (ends - 1) // tm - first_tile + 1, 0)
    item_end = jnp.cumsum(n_items)
    item_start = item_end - n_items
    total = item_end[-1]
    w = jnp.arange(nw, dtype=I32)
    wc = jnp.minimum(w, total - 1)
    ex = jnp.sum((item_end[None, :] <= wc[:, None]).astype(I32), axis=1)
    tile = first_tile[ex] + (wc - item_start[ex])
    lo = jnp.maximum(starts[ex], tile * tm) - tile * tm
    hi = jnp.minimum(ends[ex], (tile + 1) * tm) - tile * tm
    valid = w < total
    lo = jnp.where(valid, lo, 0)
    hi = jnp.where(valid, hi, 0)
    run_end = item_end[ex]
    nxt = jnp.where(run_end < total, ex[jnp.minimum(run_end, nw - 1)], -1)
    return tile.astype(I32), ex, nxt.astype(I32), lo.astype(I32), hi.astype(I32)


def _combine_body(pos_ref, x1_ref, rr_ref, ys_ref, o_ref, a0, a1, b0, b1, sem, *, n_tok):
    i = pl.program_id(0)
    nsteps = pl.num_programs(0)
    tm = x1_ref.shape[0]

    def gather(tile, bufs, sems):
        base = tile * tm

        def issue(jj, carry):
            for k in range(DMA_UNROLL):
                j = jj * DMA_UNROLL + k
                pltpu.make_async_copy(ys_ref.at[pl.ds(pos_ref[base + j], 1), :],
                                      bufs[0].at[pl.ds(j, 1), :], sem.at[sems[0]]).start()
                pltpu.make_async_copy(ys_ref.at[pl.ds(pos_ref[n_tok + base + j], 1), :],
                                      bufs[1].at[pl.ds(j, 1), :], sem.at[sems[1]]).start()
            return carry

        lax.fori_loop(0, tm // DMA_UNROLL, issue, 0)

    def step(cur, cur_sems, nxt, nxt_sems):
        @pl.when(i == 0)
        def _():
            gather(i, cur, cur_sems)

        @pl.when(i + 1 < nsteps)
        def _():
            gather(i + 1, nxt, nxt_sems)

        for buf, s in zip(cur, cur_sems):
            pltpu.make_async_copy(ys_ref.at[pl.ds(0, tm), :], buf, sem.at[s]).wait()
        rr = rr_ref[...]
        o_ref[...] = x1_ref[...] + rr[:, 2:3] * cur[0][...] + rr[:, 3:4] * cur[1][...]

    @pl.when(i % 2 == 0)
    def _():
        step((a0, a1), (0, 1), (b0, b1), (2, 3))

    @pl.when(i % 2 == 1)
    def _():
        step((b0, b1), (2, 3), (a0, a1), (0, 1))


def _combine(pos_flat, x1, route_rows, ys, tm=256):
    t, d = x1.shape
    assert tm % DMA_UNROLL == 0
    buf = pltpu.VMEM((tm, d), F32)
    return pl.pallas_call(
        functools.partial(_combine_body, n_tok=t),
        grid_spec=pltpu.PrefetchScalarGridSpec(
            num_scalar_prefetch=1, grid=(t // tm,),
            in_specs=[pl.BlockSpec((tm, d), lambda i, p: (i, 0)),
                      pl.BlockSpec((tm, ROUTE_LANES), lambda i, p: (i, 0)),
                      pl.BlockSpec(memory_space=pl.ANY)],
            out_specs=pl.BlockSpec((tm, d), lambda i, p: (i, 0)),
            scratch_shapes=[buf, buf, buf, buf, pltpu.SemaphoreType.DMA((4,))]),
        out_shape=jax.ShapeDtypeStruct((t, d), F32),
        compiler_params=_cparams(("arbitrary",)),
        name="combine",
    )(pos_flat, x1, route_rows, ys)


def _moe(x1, hp, route_rows, route_t, w_gate, w_up, w_down, gmm_tm=256, sort_chunk=512):
    t = x1.shape[0]
    n = 2 * t
    ids3 = route_t[0:2].astype(I32).reshape(n // sort_chunk, 1, sort_chunk)
    counts = _expert_counts(ids3)[:, 0]
    starts = jnp.cumsum(counts) - counts
    pos_flat = _sorted_positions(ids3, starts).reshape(n)
    xs = _dispatch(pos_flat, hp)
    ys = _gmm(_gmm_metadata(counts, n, gmm_tm), xs, w_gate, w_up, w_down, gmm_tm)
    return _combine(pos_flat, x1, route_rows, ys)


def _layer(x, attn_norm_g, w_in, lambda_re, lambda_im, log_dt, ssm_b_re, ssm_b_im, ssm_c_re, ssm_c_im,
           ssm_d, w_glu, q_norm_g, k_norm_g, w_branch_ssm, w_branch_att, w_out, ffn_norm_g,
           router_group_w, router_group_b, router_expert_w, router_expert_b,
           expert_w_gate, expert_w_up, expert_w_down):
    nb, seq, d = x.shape
    t = nb * seq
    ds = w_glu.shape[0]
    da = N_HEADS * HEAD_DIM
    x2 = x.reshape(t, d)

    o = 0
    u = _proj(x2, attn_norm_g, w_in, o, ds, "plain", F32, name="proj_u"); o += ds
    qg = q_norm_g.astype(F32) * (HEAD_DIM ** -0.5 * LOG2_E)
    q = _proj(x2, attn_norm_g, w_in, o, da, "headnorm", BF16, gain=qg, name="proj_q"); o += da
    k = _proj(x2, attn_norm_g, w_in, o, da, "headnorm", BF16, gain=k_norm_g, name="proj_k"); o += da
    v = _proj(x2, attn_norm_g, w_in, o, da, "plain", BF16, name="proj_v"); o += da
    gates = _proj(x2, attn_norm_g, w_in, o, w_in.shape[1] - o, "sigmoid", BF16, name="proj_gates")

    a_re, a_im, wb_re, wb_im, wc = _ssm_params(lambda_re, lambda_im, log_dt, ssm_b_re, ssm_b_im, ssm_c_re, ssm_c_im)
    y_ssm = _ssm(u.reshape(nb, seq, ds), a_re, a_im, wb_re, wb_im, wc, ssm_d.astype(F32), w_glu.astype(BF16))
    y_att = _attention(q.reshape(nb, seq, da), k.reshape(nb, seq, da), v.reshape(nb, seq, da))

    merged = _merge(y_ssm.reshape(t, ds), y_att.reshape(t, da), gates, w_branch_ssm, w_branch_att)

    n_pad = ROUTE_LANES - N_EXPERTS - N_EXPERT_GROUPS
    wr = jnp.concatenate([router_expert_w.astype(F32), router_group_w.astype(F32), jnp.zeros((d, n_pad), F32)], axis=1)
    br = jnp.concatenate([router_expert_b.astype(F32), router_group_b.astype(F32), jnp.zeros((n_pad,), F32)])[None, :]
    x1, hp, route_rows, route_t = _outproj(x2, merged, w_out.astype(BF16), ffn_norm_g, wr, br)

    out = _moe(x1, hp, route_rows, route_t, expert_w_gate, expert_w_up, expert_w_down)
    return out.reshape(nb, seq, d)


def kernel(x, attn_norm_g, w_in, lambda_re, lambda_im, log_dt, ssm_b_re, ssm_b_im, ssm_c_re, ssm_c_im, ssm_d, w_glu, q_norm_g, k_norm_g, w_branch_ssm, w_branch_att, w_out, ffn_norm_g, router_group_w, router_group_b, router_expert_w, router_expert_b, expert_w_gate, expert_w_up, expert_w_down):
    depth = attn_norm_g.shape[0]
    for l in range(depth):
        x = _layer(x, attn_norm_g[l], w_in[l], lambda_re[l], lambda_im[l], log_dt[l], ssm_b_re[l], ssm_b_im[l],
                   ssm_c_re[l], ssm_c_im[l], ssm_d[l], w_glu[l], q_norm_g[l], k_norm_g[l], w_branch_ssm[l],
                   w_branch_att[l], w_out[l], ffn_norm_g[l], router_group_w[l], router_group_b[l],
                   router_expert_w[l], router_expert_b[l], expert_w_gate[l], expert_w_up[l], expert_w_down[l])
    return x
```

```python
import functools

import jax
import jax.numpy as jnp
from jax import lax
from jax.experimental import pallas as pl
from jax.experimental.pallas import tpu as pltpu

F32 = jnp.float32
BF16 = jnp.bfloat16
I32 = jnp.int32
U32 = jnp.uint32

EPS = 1e-6
N_HEADS = 8
HEAD_DIM = 128
SSM_GROUP = 16
SSM_STATE = 64
GROUPS_PER_BLOCK = 16
N_EXPERT_GROUPS = 4
EXPERTS_PER_GROUP = 8
N_EXPERTS = N_EXPERT_GROUPS * EXPERTS_PER_GROUP
LANES = 128
ROUTE_LANES = 128
NEG_BIG = -1e30
DMA_UNROLL = 8
SCAN_UNROLL = 4
DISPATCH_BUFS = 3
EXP_ZERO_BELOW = -104.0
LOG2_E = 1.4426950408889634

VMEM_LIMIT = 56 * 1024 * 1024


def _cparams(sem):
    return pltpu.CompilerParams(dimension_semantics=sem, vmem_limit_bytes=VMEM_LIMIT)


def _proj_body(x_ref, gn_ref, w_ref, g_ref, o_ref, wb_ref, *, mode):
    @pl.when(pl.program_id(1) == 0)
    def _():
        wb_ref[...] = (w_ref[...] * gn_ref[...]).astype(BF16)

    x = x_ref[...]
    acc = jnp.dot(x.astype(BF16), wb_ref[...], preferred_element_type=F32)
    acc = acc * lax.rsqrt(jnp.mean(x * x, axis=-1, keepdims=True) + EPS)
    if mode == "plain":
        o_ref[...] = acc.astype(o_ref.dtype)
    elif mode == "sigmoid":
        o_ref[...] = jax.nn.sigmoid(acc).astype(o_ref.dtype)
    else:
        n = acc.shape[1] // HEAD_DIM
        for hh in range(n):
            blk = acc[:, hh * HEAD_DIM:(hh + 1) * HEAD_DIM]
            ms = jnp.mean(blk * blk, axis=-1, keepdims=True)
            o_ref[:, hh * HEAD_DIM:(hh + 1) * HEAD_DIM] = (blk * lax.rsqrt(ms + EPS) * g_ref[...]).astype(o_ref.dtype)


def _proj(x2, norm_g, w, col0, n, mode, out_dtype, gain=None, tm=1024, tn=1024, name="proj"):
    t, k = x2.shape
    assert col0 % tn == 0 and n % tn == 0
    j0 = col0 // tn
    if gain is None:
        gain = jnp.ones((HEAD_DIM,), F32)
    return pl.pallas_call(
        functools.partial(_proj_body, mode=mode),
        grid=(n // tn, t // tm),
        in_specs=[pl.BlockSpec((tm, k), lambda j, i: (i, 0)),
                  pl.BlockSpec((k, 1), lambda j, i: (0, 0)),
                  pl.BlockSpec((k, tn), lambda j, i: (0, j0 + j)),
                  pl.BlockSpec((1, HEAD_DIM), lambda j, i: (0, 0))],
        out_specs=pl.BlockSpec((tm, tn), lambda j, i: (i, j)),
        out_shape=jax.ShapeDtypeStruct((t, n), out_dtype),
        scratch_shapes=[pltpu.VMEM((k, tn), BF16)],
        compiler_params=_cparams(("arbitrary", "arbitrary")),
        name=name,
    )(x2, norm_g.reshape(k, 1).astype(F32), w, gain.reshape(1, HEAD_DIM).astype(F32))


def _ssm_body(u_ref, are_ref, aim_ref, wbre_ref, wbim_ref, wc_ref, d_ref, wglu_ref,
              o_ref, tm_ref, bure_ref, buim_ref, sre_ref, sim_ref, *, tc, lane_chunk):
    nb = u_ref.shape[0]
    ds = u_ref.shape[2]
    rows = nb * tc
    nslab = ds // LANES
    nblk = wbre_ref.shape[0]
    cb = wbre_ref.shape[1]
    sb = wbre_ref.shape[2]
    n_state = nblk * sb

    @pl.when(pl.program_id(0) == 0)
    def _():
        sre_ref[...] = jnp.zeros_like(sre_ref)
        sim_ref[...] = jnp.zeros_like(sim_ref)

    for b in range(nb):
        for j in range(nslab):
            tm_ref[j, pl.ds(b, tc, stride=nb), :] = u_ref[b, :, j * LANES:(j + 1) * LANES]
    u_tm = jnp.concatenate([tm_ref[j] for j in range(nslab)], axis=1).astype(BF16)
    for gb in range(nblk):
        lhs = u_tm[:, gb * cb:(gb + 1) * cb]
        bure_ref[:, gb * sb:(gb + 1) * sb] = jnp.dot(lhs, wbre_ref[gb], preferred_element_type=F32)
        buim_ref[:, gb * sb:(gb + 1) * sb] = jnp.dot(lhs, wbim_ref[gb], preferred_element_type=F32)

    for lc in range(n_state // lane_chunk):
        sl = slice(lc * lane_chunk, (lc + 1) * lane_chunk)
        ar = jnp.broadcast_to(are_ref[:, sl], (nb, lane_chunk))
        ai = jnp.broadcast_to(aim_ref[:, sl], (nb, lane_chunk))

        def steps(tt, carry, sl=sl, ar=ar, ai=ai):
            xr, xi = carry
            for k in range(SCAN_UNROLL):
                r0 = pl.multiple_of((tt * SCAN_UNROLL + k) * nb, nb)
                br = bure_ref[pl.ds(r0, nb), sl]
                bi = buim_ref[pl.ds(r0, nb), sl]
                xr, xi = ar * xr - ai * xi + br, ar * xi + ai * xr + bi
                bure_ref[pl.ds(r0, nb), sl] = xr
                buim_ref[pl.ds(r0, nb), sl] = xi
            return xr, xi

        xr, xi = lax.fori_loop(0, tc // SCAN_UNROLL, steps, (sre_ref[:, sl], sim_ref[:, sl]))
        sre_ref[:, sl] = xr
        sim_ref[:, sl] = xi

    for gb in range(nblk):
        xs = jnp.concatenate([bure_ref[:, gb * sb:(gb + 1) * sb], buim_ref[:, gb * sb:(gb + 1) * sb]], axis=1)
        y_blk = jnp.dot(xs.astype(BF16), wc_ref[gb], preferred_element_type=F32)
        for jj in range(cb // LANES):
            tm_ref[gb * (cb // LANES) + jj] = y_blk[:, jj * LANES:(jj + 1) * LANES]
    y_bm = jnp.concatenate(
        [jnp.concatenate([tm_ref[j, pl.ds(b, tc, stride=nb), :] for j in range(nslab)], axis=1) for b in range(nb)],
        axis=0)
    y = y_bm + d_ref[...] * u_ref[...].reshape(rows, ds)
    z = jax.nn.gelu(y)
    gate = jax.nn.sigmoid(jnp.dot(z.astype(BF16), wglu_ref[...], preferred_element_type=F32))
    o_ref[...] = (z * gate).reshape(nb, tc, ds).astype(o_ref.dtype)


def _ssm(u, a_re, a_im, wb_re, wb_im, wc, d_skip, w_glu, tc=64, lane_chunk=512):
    nb, seq, ds = u.shape
    rows = nb * tc
    n_state = a_re.shape[1]
    assert tc % SCAN_UNROLL == 0
    const = lambda *shape: pl.BlockSpec(shape, lambda c: (0,) * len(shape))
    return pl.pallas_call(
        functools.partial(_ssm_body, tc=tc, lane_chunk=lane_chunk),
        grid=(seq // tc,),
        in_specs=[pl.BlockSpec((nb, tc, ds), lambda c: (0, c, 0)),
                  const(1, n_state), const(1, n_state),
                  const(*wb_re.shape), const(*wb_im.shape), const(*wc.shape),
                  const(1, ds), const(ds, ds)],
        out_specs=pl.BlockSpec((nb, tc, ds), lambda c: (0, c, 0)),
        out_shape=jax.ShapeDtypeStruct((nb, seq, ds), BF16),
        scratch_shapes=[pltpu.VMEM((ds // LANES, rows, LANES), F32),
                        pltpu.VMEM((rows, n_state), F32), pltpu.VMEM((rows, n_state), F32),
                        pltpu.VMEM((nb, n_state), F32), pltpu.VMEM((nb, n_state), F32)],
        compiler_params=_cparams(("arbitrary",)),
        name="s5_mixer",
    )(u, a_re, a_im, wb_re, wb_im, wc, d_skip.reshape(1, ds), w_glu)


def _ssm_params(lambda_re, lambda_im, log_dt, b_re, b_im, c_re, c_im):
    g, p = lambda_re.shape
    h = b_re.shape[2]
    nblk = g // GROUPS_PER_BLOCK
    dt = jnp.exp(log_dt.astype(F32))[:, None]
    lr = lambda_re.astype(F32)
    li = lambda_im.astype(F32)
    mag = jnp.exp(lr * dt)
    abar_re, abar_im = mag * jnp.cos(li * dt), mag * jnp.sin(li * dt)
    nr, ni = abar_re - 1.0, abar_im
    den = lr * lr + li * li
    coef_re = (nr * lr + ni * li) / den
    coef_im = (ni * lr - nr * li) / den
    bbar_re = coef_re[..., None] * b_re - coef_im[..., None] * b_im
    bbar_im = coef_re[..., None] * b_im + coef_im[..., None] * b_re
    gpb = GROUPS_PER_BLOCK

    def block_diag(m, rows_per_group, cols_per_group):
        r = jnp.arange(gpb * rows_per_group)[:, None] // rows_per_group
        c = jnp.arange(gpb * cols_per_group)[None, :] // cols_per_group
        return jnp.where(r == c, jnp.tile(m, (1, gpb, 1)), 0.0)

    def pack_b(bb):
        m = bb.reshape(nblk, gpb, p, h).transpose(0, 3, 1, 2).reshape(nblk, h, gpb * p)
        return block_diag(m, h, p)

    def pack_c(cc):
        m = cc.reshape(nblk, gpb, h, p).transpose(0, 3, 1, 2).reshape(nblk, p, gpb * h)
        return block_diag(m, p, h)

    wc = jnp.concatenate([pack_c(c_re.astype(F32)), -pack_c(c_im.astype(F32))], axis=1)
    return (abar_re.reshape(1, g * p), abar_im.reshape(1, g * p),
            pack_b(bbar_re).astype(BF16), pack_b(bbar_im).astype(BF16), wc.astype(BF16))


def _attn_body(q_ref, k_ref, v_ref, tri_ref, o_ref, acc_ref, rs_ref, *, tq, nhs):
    qi = pl.program_id(2)
    tri = tri_ref[...]
    row = lax.broadcasted_iota(I32, (nhs * tq, tq), 0) % tq
    col = lax.broadcasted_iota(I32, (nhs * tq, tq), 1)
    causal = col < row

    def sweep(j, masked):
        k0 = pl.multiple_of(j * tq, tq)
        heads = [slice(h * HEAD_DIM, (h + 1) * HEAD_DIM) for h in range(nhs)]
        z = jnp.concatenate(
            [lax.dot_general(q_ref[0, :, hs], k_ref[0, pl.ds(k0, tq), hs], (((1,), (1,)), ((), ())),
                             preferred_element_type=F32) for hs in heads], axis=0)
        sp = jnp.maximum(z, 0.0) + jnp.log(1.0 + jnp.exp2(-jnp.abs(z))) * LOG2_E
        spm = jnp.where(causal, sp, 0.0) if masked else sp
        spb = spm.astype(BF16)
        tail = jnp.dot(spb, tri, preferred_element_type=F32)
        rsum = rs_ref[...]
        w = jnp.exp2(z - (sp + tail + jnp.concatenate([rsum] * (tq // LANES), axis=1)))
        if masked:
            w = jnp.where(causal, w, 0.0)
        wb = w.astype(BF16)
        for h, hs in enumerate(heads):
            acc_ref[h] += jnp.dot(wb[h * tq:(h + 1) * tq], v_ref[0, pl.ds(k0, tq), hs], preferred_element_type=F32)
        total = tail[:, 0:1] + spb[:, 0:1].astype(F32)
        rsum = rsum + jnp.broadcast_to(total, rsum.shape)
        rs_ref[...] = rsum
        return (jnp.min(rsum) < -EXP_ZERO_BELOW * LOG2_E).astype(I32)

    acc_ref[...] = jnp.zeros_like(acc_ref)
    rs_ref[...] = jnp.zeros_like(rs_ref)
    live = sweep(qi, True)

    def cond(c):
        return (c[0] >= 0) & (c[1] > 0)

    def body(c):
        return c[0] - 1, sweep(c[0], False)

    lax.while_loop(cond, body, (qi - 1, live))
    for h in range(nhs):
        o_ref[0, :, h * HEAD_DIM:(h + 1) * HEAD_DIM] = acc_ref[h].astype(o_ref.dtype)


def _attention(q, k, v, tq=256, nhs=8):
    nb, seq, da = q.shape
    nh = da // HEAD_DIM
    wd = nhs * HEAD_DIM
    r = jnp.arange(tq)
    tri = (r[:, None] > r[None, :]).astype(BF16)
    return pl.pallas_call(
        functools.partial(_attn_body, tq=tq, nhs=nhs),
        grid=(nb, nh // nhs, seq // tq),
        in_specs=[pl.BlockSpec((1, tq, wd), lambda b, h, i: (b, i, h)),
                  pl.BlockSpec((1, seq, wd), lambda b, h, i: (b, 0, h)),
                  pl.BlockSpec((1, seq, wd), lambda b, h, i: (b, 0, h)),
                  pl.BlockSpec((tq, tq), lambda b, h, i: (0, 0))],
        out_specs=pl.BlockSpec((1, tq, wd), lambda b, h, i: (b, i, h)),
        out_shape=jax.ShapeDtypeStruct((nb, seq, da), BF16),
        scratch_shapes=[pltpu.VMEM((nhs, tq, HEAD_DIM), F32), pltpu.VMEM((nhs * tq, LANES), F32)],
        compiler_params=_cparams(("arbitrary", "arbitrary", "arbitrary")),
        name="stick_attention",
    )(q, k, v, tri)


def _merge_body(ys_ref, ya_ref, gs_ref, ga_ref, ws_ref, wa_ref, o_ref, wsb_ref, wab_ref):
    @pl.when(pl.program_id(1) == 0)
    def _():
        wsb_ref[...] = ws_ref[...].astype(BF16)
        wab_ref[...] = wa_ref[...].astype(BF16)

    ps = jnp.dot(ys_ref[...], wsb_ref[...], preferred_element_type=F32)
    pa = jnp.dot(ya_ref[...], wab_ref[...], preferred_element_type=F32)
    o_ref[...] = (gs_ref[...].astype(F32) * ps + ga_ref[...].astype(F32) * pa).astype(o_ref.dtype)


def _merge(y_ssm, y_att, gates, w_s, w_a, tm=1024, tn=1024):
    t, ks = y_ssm.shape
    d = w_s.shape[1]
    nj = d // tn
    return pl.pallas_call(
        _merge_body,
        grid=(nj, t // tm),
        in_specs=[pl.BlockSpec((tm, ks), lambda j, i: (i, 0)),
                  pl.BlockSpec((tm, y_att.shape[1]), lambda j, i: (i, 0)),
                  pl.BlockSpec((tm, tn), lambda j, i: (i, j)),
                  pl.BlockSpec((tm, tn), lambda j, i: (i, nj + j)),
                  pl.BlockSpec((ks, tn), lambda j, i: (0, j)),
                  pl.BlockSpec((w_a.shape[0], tn), lambda j, i: (0, j))],
        out_specs=pl.BlockSpec((tm, tn), lambda j, i: (i, j)),
        out_shape=jax.ShapeDtypeStruct((t, d), BF16),
        scratch_shapes=[pltpu.VMEM((ks, tn), BF16), pltpu.VMEM((w_a.shape[0], tn), BF16)],
        compiler_params=_cparams(("arbitrary", "arbitrary")),
        name="branch_merge",
    )(y_ssm, y_att, gates, gates, w_s, w_a)


def _lane_min_index(mask, lane):
    return jnp.min(jnp.where(mask, lane, float(ROUTE_LANES)), axis=1, keepdims=True)


def _outproj_body(x_ref, m_ref, w_ref, g_ref, wrc_ref, br_ref,
                  x1_ref, hp_ref, rr_ref, rt_ref):
    x1 = x_ref[...] + jnp.dot(m_ref[...], w_ref[...], preferred_element_type=F32)
    x1_ref[...] = x1
    ms = jnp.mean(x1 * x1, axis=-1, keepdims=True)
    h2 = x1 * lax.rsqrt(ms + EPS) * g_ref[...]
    hp_ref[...] = h2
    hb = h2.astype(BF16)

    hl = (h2 - hb.astype(F32)).astype(BF16)
    p = jnp.dot(hb, wrc_ref[...], preferred_element_type=F32)
    logits = (p[:, :ROUTE_LANES] + p[:, ROUTE_LANES:]
              + jnp.dot(hl, wrc_ref[:, :ROUTE_LANES], preferred_element_type=F32)) + br_ref[...]
    tm = logits.shape[0]
    lane = lax.broadcasted_iota(I32, (tm, ROUTE_LANES), 1).astype(F32)
    is_group = (lane >= N_EXPERTS) & (lane < N_EXPERTS + N_EXPERT_GROUPS)
    gl = jnp.where(is_group, logits, NEG_BIG)
    gmax = jnp.max(gl, axis=1, keepdims=True)
    gidx = _lane_min_index(gl == gmax, lane) - N_EXPERTS
    g_top = 1.0 / jnp.sum(jnp.exp(gl - gmax), axis=1, keepdims=True)
    lo = gidx * EXPERTS_PER_GROUP
    in_grp = (lane >= lo) & (lane < lo + EXPERTS_PER_GROUP)
    el = jnp.where(in_grp, logits, NEG_BIG)
    m1 = jnp.max(el, axis=1, keepdims=True)
    i1 = _lane_min_index(el == m1, lane)
    el2 = jnp.where(lane == i1, NEG_BIG, el)
    m2 = jnp.max(el2, axis=1, keepdims=True)
    i2 = _lane_min_index(el2 == m2, lane)
    dlt = jnp.exp(m2 - m1)
    w1 = 1.0 / (1.0 + dlt)
    w2 = dlt * w1
    route = jnp.where(lane == 0, i1,
                      jnp.where(lane == 1, i2,
                                jnp.where(lane == 2, g_top * w1,
                                          jnp.where(lane == 3, g_top * w2, 0.0))))
    rr_ref[...] = route
    rt_ref[...] = jnp.transpose(route)[0:8, :]


def _outproj(x2, merged, w_out, g, wr, br, tm=512):
    t, d = x2.shape
    wr_hi = wr.astype(BF16)
    wr_cat = jnp.concatenate([wr_hi, (wr - wr_hi.astype(F32)).astype(BF16)], axis=1)
    const = lambda *shape: pl.BlockSpec(shape, lambda i: (0,) * len(shape))
    return pl.pallas_call(
        _outproj_body,
        grid=(t // tm,),
        in_specs=[pl.BlockSpec((tm, d), lambda i: (i, 0)), pl.BlockSpec((tm, d), lambda i: (i, 0)),
                  const(d, d), const(1, d), const(d, 2 * ROUTE_LANES), const(1, ROUTE_LANES)],
        out_specs=[pl.BlockSpec((tm, d), lambda i: (i, 0)),
                   pl.BlockSpec((tm, d), lambda i: (i, 0)),
                   pl.BlockSpec((tm, ROUTE_LANES), lambda i: (i, 0)),
                   pl.BlockSpec((8, tm), lambda i: (0, i))],
        out_shape=[jax.ShapeDtypeStruct((t, d), F32),
                   jax.ShapeDtypeStruct((t, d), F32),
                   jax.ShapeDtypeStruct((t, ROUTE_LANES), F32),
                   jax.ShapeDtypeStruct((8, t), F32)],
        compiler_params=_cparams(("arbitrary",)),
        name="outproj_router",
    )(x2, merged, w_out, g.reshape(1, d), wr_cat, br)


def _count_body(ids_ref, cnt_ref):
    nc, _, c = ids_ref.shape
    eidx = lax.broadcasted_iota(I32, (N_EXPERTS, c), 0)

    def step(i, acc):
        return acc + jnp.where(ids_ref[i] == eidx, 1.0, 0.0)

    acc = lax.fori_loop(0, nc, step, jnp.zeros((N_EXPERTS, c), F32))
    tot = jnp.sum(acc, axis=1, keepdims=True)
    cnt_ref[...] = jnp.broadcast_to(tot, cnt_ref.shape).astype(I32)


def _expert_counts(ids3):
    return pl.pallas_call(
        _count_body,
        out_shape=jax.ShapeDtypeStruct((N_EXPERTS, 128), I32),
        name="expert_counts",
    )(ids3)


def _pos_body(ids_ref, start_ref, incl_ref, pos_ref):
    nc, _, c = ids_ref.shape
    eidx = lax.broadcasted_iota(I32, (N_EXPERTS, c), 0)
    incl = incl_ref[...]

    def step(i, carry):
        onehot = ids_ref[i] == eidx
        cum = jnp.dot(jnp.where(onehot, 1.0, 0.0).astype(BF16), incl, preferred_element_type=F32)
        val = jnp.where(onehot, cum - 1.0 + carry, 0.0)
        pos_ref[i] = jnp.sum(val, axis=0, keepdims=True).astype(I32)
        return carry + cum[:, c - 1:c]

    lax.fori_loop(0, nc, step, start_ref[:, 0:1].astype(F32))


def _sorted_positions(ids3, starts):
    nc, _, c = ids3.shape
    r = jnp.arange(c)
    incl = (r[:, None] <= r[None, :]).astype(BF16)
    start_b = jnp.broadcast_to(starts.astype(I32)[:, None], (N_EXPERTS, 128))
    return pl.pallas_call(
        _pos_body,
        out_shape=jax.ShapeDtypeStruct((nc, 1, c), I32),
        name="sorted_positions",
    )(ids3, start_b, incl)


def _dispatch_body(pos_ref, src_ref, dst_ref, buf, lsem, ssem, *, n_tok, tm):
    i = pl.program_id(0)
    nsteps = pl.num_programs(0)
    base = i * tm

    def load(tile):
        b = tile % DISPATCH_BUFS
        return pltpu.make_async_copy(src_ref.at[pl.ds(tile * tm, tm), :], buf.at[b], lsem.at[b])

    def drain(tile):
        b = tile % DISPATCH_BUFS
        for s in range(2):
            pltpu.make_async_copy(buf.at[b], dst_ref.at[pl.ds(0, tm), :], ssem.at[2 * b + s]).wait()

    @pl.when(i == 0)
    def _():
        load(0).start()

        @pl.when(nsteps > 1)
        def _():
            load(1).start()

    load(i).wait()
    b = i % DISPATCH_BUFS

    def issue(ii, carry):
        for k in range(DMA_UNROLL):
            j = ii * DMA_UNROLL + k
            row = buf.at[b, pl.ds(j, 1), :]
            pltpu.make_async_copy(row, dst_ref.at[pl.ds(pos_ref[base + j], 1), :], ssem.at[2 * b]).start()
            pltpu.make_async_copy(row, dst_ref.at[pl.ds(pos_ref[n_tok + base + j], 1), :], ssem.at[2 * b + 1]).start()
        return carry

    lax.fori_loop(0, tm // DMA_UNROLL, issue, 0)

    @pl.when(i >= 1)
    def _():
        drain(i - 1)

    @pl.when(i + 2 < nsteps)
    def _():
        load(i + 2).start()

    @pl.when(i == nsteps - 1)
    def _():
        drain(i)


def _dispatch(pos_flat, hp, tm=256):
    n = pos_flat.shape[0]
    t, d = hp.shape
    assert t % tm == 0 and tm % DMA_UNROLL == 0
    return pl.pallas_call(
        functools.partial(_dispatch_body, n_tok=t, tm=tm),
        grid_spec=pltpu.PrefetchScalarGridSpec(
            num_scalar_prefetch=1, grid=(t // tm,),
            in_specs=[pl.BlockSpec(memory_space=pl.ANY)],
            out_specs=pl.BlockSpec(memory_space=pl.ANY),
            scratch_shapes=[pltpu.VMEM((DISPATCH_BUFS, tm, d), hp.dtype),
                            pltpu.SemaphoreType.DMA((DISPATCH_BUFS,)),
                            pltpu.SemaphoreType.DMA((2 * DISPATCH_BUFS,))]),
        out_shape=jax.ShapeDtypeStruct((n, d), hp.dtype),
        compiler_params=_cparams(("arbitrary",)),
        name="dispatch",
    )(pos_flat, hp)


def _gmm_body(tile_ref, exp_ref, nxt_ref, lo_ref, hi_ref, x_ref, wg_ref, wu_ref, wd_ref, o_ref,
              sg, su, sd, wgs, wus, wds, sem):
    w = pl.program_id(0)
    prev = jnp.maximum(w - 1, 0)
    new_expert = (w == 0) | (exp_ref[w] != exp_ref[prev])
    first_of_tile = (w == 0) | (tile_ref[w] != tile_ref[prev])
    lo = lo_ref[w]
    hi = hi_ref[w]

    def stage(e):
        return (pltpu.make_async_copy(wg_ref.at[e], sg, sem.at[0]),
                pltpu.make_async_copy(wu_ref.at[e], su, sem.at[1]),
                pltpu.make_async_copy(wd_ref.at[e], sd, sem.at[2]))

    @pl.when(w == 0)
    def _():
        for cp in stage(exp_ref[0]):
            cp.start()

    @pl.when(new_expert)
    def _():
        for cp in stage(exp_ref[w]):
            cp.wait()
        wgs[...] = sg[...].astype(BF16)
        wus[...] = su[...].astype(BF16)
        wds[...] = sd[...].astype(BF16)

        @pl.when(nxt_ref[w] >= 0)
        def _():
            for cp in stage(nxt_ref[w]):
                cp.start()

    @pl.when(hi > lo)
    def _():
        xx = x_ref[...].astype(BF16)
        g = jnp.dot(xx, wgs[...], preferred_element_type=F32)
        u = jnp.dot(xx, wus[...], preferred_element_type=F32)
        hid = (g * jax.nn.sigmoid(g)) * u
        y = jnp.dot(hid.astype(BF16), wds[...], preferred_element_type=F32)
        rows = lax.broadcasted_iota(I32, (y.shape[0], 1), 0)
        y = jnp.where((rows >= lo) & (rows < hi), y, 0.0)

        @pl.when(first_of_tile)
        def _():
            o_ref[...] = y

        @pl.when(jnp.logical_not(first_of_tile))
        def _():
            o_ref[...] += y


def _gmm(meta, xs, w_gate, w_up, w_down, tm):
    tile_id, expert_id, next_expert, row_lo, row_hi = meta
    n, _ = xs.shape
    e, d, f = w_gate.shape
    nw = tile_id.shape[0]
    hbm = pl.BlockSpec(memory_space=pl.ANY)
    return pl.pallas_call(
        _gmm_body,
        grid_spec=pltpu.PrefetchScalarGridSpec(
            num_scalar_prefetch=5, grid=(nw,),
            in_specs=[pl.BlockSpec((tm, d), lambda w, ti, ex, nx, lo, hi: (ti[w], 0)), hbm, hbm, hbm],
            out_specs=pl.BlockSpec((tm, d), lambda w, ti, ex, nx, lo, hi: (ti[w], 0)),
            scratch_shapes=[pltpu.VMEM((d, f), F32), pltpu.VMEM((d, f), F32), pltpu.VMEM((f, d), F32),
                            pltpu.VMEM((d, f), BF16), pltpu.VMEM((d, f), BF16), pltpu.VMEM((f, d), BF16),
                            pltpu.SemaphoreType.DMA((3,))]),
        out_shape=jax.ShapeDtypeStruct((n, d), F32),
        compiler_params=_cparams(("arbitrary",)),
        name="expert_gmm",
    )(tile_id, expert_id, next_expert, row_lo, row_hi, xs, w_gate, w_up, w_down)


def _gmm_metadata(counts, n_rows, tm):
    nt = n_rows // tm
    nw = nt + N_EXPERTS - 1
    ends = jnp.cumsum(counts)
    starts = ends - counts
    first_tile = starts // tm
    n_items = jnp.where(counts > 0, (ends - 1) // tm - first_tile + 1, 0)
    item_end = jnp.cumsum(n_items)
    item_start = item_end - n_items
    total = item_end[-1]
    w = jnp.arange(nw, dtype=I32)
    wc = jnp.minimum(w, total - 1)
    ex = jnp.sum((item_end[None, :] <= wc[:, None]).astype(I32), axis=1)
    tile = first_tile[ex] + (wc - item_start[ex])
    lo = jnp.maximum(starts[ex], tile * tm) - tile * tm
    hi = jnp.minimum(ends[ex], (tile + 1) * tm) - tile * tm
    valid = w < total
    lo = jnp.where(valid, lo, 0)
    hi = jnp.where(valid, hi, 0)
    run_end = item_end[ex]
    nxt = jnp.where(run_end < total, ex[jnp.minimum(run_end, nw - 1)], -1)
    return tile.astype(I32), ex, nxt.astype(I32), lo.astype(I32), hi.astype(I32)


def _combine_body(pos_ref, x1_ref, rr_ref, ys_ref, o_ref, a0, a1, b0, b1, sem, *, n_tok):
    i = pl.program_id(0)
    nsteps = pl.num_programs(0)
    tm = x1_ref.shape[0]

    def gather(tile, bufs, sems):
        base = tile * tm

        def issue(jj, carry):
            for k in range(DMA_UNROLL):
                j = jj * DMA_UNROLL + k
                pltpu.make_async_copy(ys_ref.at[pl.ds(pos_ref[base + j], 1), :],
                                      bufs[0].at[pl.ds(j, 1), :], sem.at[sems[0]]).start()
                pltpu.make_async_copy(ys_ref.at[pl.ds(pos_ref[n_tok + base + j], 1), :],
                                      bufs[1].at[pl.ds(j, 1), :], sem.at[sems[1]]).start()
            return carry

        lax.fori_loop(0, tm // DMA_UNROLL, issue, 0)

    def step(cur, cur_sems, nxt, nxt_sems):
        @pl.when(i == 0)
        def _():
            gather(i, cur, cur_sems)

        @pl.when(i + 1 < nsteps)
        def _():
            gather(i + 1, nxt, nxt_sems)

        for buf, s in zip(cur, cur_sems):
            pltpu.make_async_copy(ys_ref.at[pl.ds(0, tm), :], buf, sem.at[s]).wait()
        rr = rr_ref[...]
        o_ref[...] = x1_ref[...] + rr[:, 2:3] * cur[0][...] + rr[:, 3:4] * cur[1][...]

    @pl.when(i % 2 == 0)
    def _():
        step((a0, a1), (0, 1), (b0, b1), (2, 3))

    @pl.when(i % 2 == 1)
    def _():
        step((b0, b1), (2, 3), (a0, a1), (0, 1))


def _combine(pos_flat, x1, route_rows, ys, tm=256):
    t, d = x1.shape
    assert tm % DMA_UNROLL == 0
    buf = pltpu.VMEM((tm, d), F32)
    return pl.pallas_call(
        functools.partial(_combine_body, n_tok=t),
        grid_spec=pltpu.PrefetchScalarGridSpec(
            num_scalar_prefetch=1, grid=(t // tm,),
            in_specs=[pl.BlockSpec((tm, d), lambda i, p: (i, 0)),
                      pl.BlockSpec((tm, ROUTE_LANES), lambda i, p: (i, 0)),
                      pl.BlockSpec(memory_space=pl.ANY)],
            out_specs=pl.BlockSpec((tm, d), lambda i, p: (i, 0)),
            scratch_shapes=[buf, buf, buf, buf, pltpu.SemaphoreType.DMA((4,))]),
        out_shape=jax.ShapeDtypeStruct((t, d), F32),
        compiler_params=_cparams(("arbitrary",)),
        name="combine",
    )(pos_flat, x1, route_rows, ys)


def _moe(x1, hp, route_rows, route_t, w_gate, w_up, w_down, gmm_tm=256, sort_chunk=512):
    t = x1.shape[0]
    n = 2 * t
    ids3 = route_t[0:2].astype(I32).reshape(n // sort_chunk, 1, sort_chunk)
    counts = _expert_counts(ids3)[:, 0]
    starts = jnp.cumsum(counts) - counts
    pos_flat = _sorted_positions(ids3, starts).reshape(n)
    xs = _dispatch(pos_flat, hp)
    ys = _gmm(_gmm_metadata(counts, n, gmm_tm), xs, w_gate, w_up, w_down, gmm_tm)
    return _combine(pos_flat, x1, route_rows, ys)


def _layer(x, attn_norm_g, w_in, lambda_re, lambda_im, log_dt, ssm_b_re, ssm_b_im, ssm_c_re, ssm_c_im,
           ssm_d, w_glu, q_norm_g, k_norm_g, w_branch_ssm, w_branch_att, w_out, ffn_norm_g,
           router_group_w, router_group_b, router_expert_w, router_expert_b,
           expert_w_gate, expert_w_up, expert_w_down):
    nb, seq, d = x.shape
    t = nb * seq
    ds = w_glu.shape[0]
    da = N_HEADS * HEAD_DIM
    x2 = x.reshape(t, d)

    o = 0
    u = _proj(x2, attn_norm_g, w_in, o, ds, "plain", F32, name="proj_u"); o += ds
    qg = q_norm_g.astype(F32) * (HEAD_DIM ** -0.5 * LOG2_E)
    q = _proj(x2, attn_norm_g, w_in, o, da, "headnorm", BF16, gain=qg, name="proj_q"); o += da
    k = _proj(x2, attn_norm_g, w_in, o, da, "headnorm", BF16, gain=k_norm_g, name="proj_k"); o += da
    v = _proj(x2, attn_norm_g, w_in, o, da, "plain", BF16, name="proj_v"); o += da
    gates = _proj(x2, attn_norm_g, w_in, o, w_in.shape[1] - o, "sigmoid", BF16, name="proj_gates")

    a_re, a_im, wb_re, wb_im, wc = _ssm_params(lambda_re, lambda_im, log_dt, ssm_b_re, ssm_b_im, ssm_c_re, ssm_c_im)
    y_ssm = _ssm(u.reshape(nb, seq, ds), a_re, a_im, wb_re, wb_im, wc, ssm_d.astype(F32), w_glu.astype(BF16))
    y_att = _attention(q.reshape(nb, seq, da), k.reshape(nb, seq, da), v.reshape(nb, seq, da))

    merged = _merge(y_ssm.reshape(t, ds), y_att.reshape(t, da), gates, w_branch_ssm, w_branch_att)

    n_pad = ROUTE_LANES - N_EXPERTS - N_EXPERT_GROUPS
    wr = jnp.concatenate([router_expert_w.astype(F32), router_group_w.astype(F32), jnp.zeros((d, n_pad), F32)], axis=1)
    br = jnp.concatenate([router_expert_b.astype(F32), router_group_b.astype(F32), jnp.zeros((n_pad,), F32)])[None, :]
    x1, hp, route_rows, route_t = _outproj(x2, merged, w_out.astype(BF16), ffn_norm_g, wr, br)

    out = _moe(x1, hp, route_rows, route_t, expert_w_gate, expert_w_up, expert_w_down)
    return out.reshape(nb, seq, d)


def kernel(x, attn_norm_g, w_in, lambda_re, lambda_im, log_dt, ssm_b_re, ssm_b_im, ssm_c_re, ssm_c_im, ssm_d, w_glu, q_norm_g, k_norm_g, w_branch_ssm, w_branch_att, w_out, ffn_norm_g, router_group_w, router_group_b, router_expert_w, router_expert_b, expert_w_gate, expert_w_up, expert_w_down):
    depth = attn_norm_g.shape[0]
    for l in range(depth):
        x = _layer(x, attn_norm_g[l], w_in[l], lambda_re[l], lambda_im[l], log_dt[l], ssm_b_re[l], ssm_b_im[l],
                   ssm_c_re[l], ssm_c_im[l], ssm_d[l], w_glu[l], q_norm_g[l], k_norm_g[l], w_branch_ssm[l],
                   w_branch_att[l], w_out[l], ffn_norm_g[l], router_group_w[l], router_group_b[l],
                   router_expert_w[l], router_expert_b[l], expert_w_gate[l], expert_w_up[l], expert_w_down[l])
    return x
```

```python
import functools

import jax
import jax.numpy as jnp
from jax import lax
from jax.experimental import pallas as pl
from jax.experimental.pallas import tpu as pltpu

F32 = jnp.float32
BF16 = jnp.bfloat16
I32 = jnp.int32
U32 = jnp.uint32

EPS = 1e-6
N_HEADS = 8
HEAD_DIM = 128
N_EXPERT_GROUPS = 4
EXPERTS_PER_GROUP = 8
N_EXPERTS = N_EXPERT_GROUPS * EXPERTS_PER_GROUP

LANES = 128
V7X_VMEM_BYTES = 64 * 1024 * 1024
VMEM_LIMIT = V7X_VMEM_BYTES - 8 * 1024 * 1024

PROJ_TM, PROJ_TN = 1024, 1024
MERGE_TM, MERGE_TN = 1024, 1024
OUTPROJ_TM = 512
SSM_TC = 64
SSM_LANE_CHUNK = 512
SSM_GROUPS_PER_BLOCK = 16
ATTN_TQ = 256
ATTN_HEADS_PER_STEP = 8
SORT_CHUNK = 512
MOE_TM = 256
DMA_UNROLL = 8
SCAN_UNROLL = 4
DISPATCH_BUFS = 3

ROUTE_LANES = LANES
NEG_BIG = -1e30
EXP_ZERO_BELOW = -104.0
LOG2_E = 1.4426950408889634
SOFTPLUS2_LINEAR_ABOVE = 30.0


def _cparams(sem):
    return pltpu.CompilerParams(dimension_semantics=sem, vmem_limit_bytes=VMEM_LIMIT)


def _proj_body(x_ref, gn_ref, w_ref, g_ref, o_ref, wb_ref, *, mode):
    @pl.when(pl.program_id(1) == 0)
    def _():
        wb_ref[...] = (w_ref[...] * gn_ref[...]).astype(BF16)

    x = x_ref[...]
    acc = jnp.dot(x.astype(BF16), wb_ref[...], preferred_element_type=F32)
    acc = acc * lax.rsqrt(jnp.mean(x * x, axis=-1, keepdims=True) + EPS)
    if mode == "plain":
        o_ref[...] = acc.astype(o_ref.dtype)
    elif mode == "sigmoid":
        o_ref[...] = jax.nn.sigmoid(acc).astype(o_ref.dtype)
    else:
        n = acc.shape[1] // HEAD_DIM
        for hh in range(n):
            blk = acc[:, hh * HEAD_DIM:(hh + 1) * HEAD_DIM]
            ms = jnp.mean(blk * blk, axis=-1, keepdims=True)
            o_ref[:, hh * HEAD_DIM:(hh + 1) * HEAD_DIM] = (blk * lax.rsqrt(ms + EPS) * g_ref[...]).astype(o_ref.dtype)


def _proj(x2, norm_g, w, col0, n, mode, out_dtype, gain=None, tm=PROJ_TM, tn=PROJ_TN, name="proj"):
    t, k = x2.shape
    assert col0 % tn == 0 and n % tn == 0
    j0 = col0 // tn
    if gain is None:
        gain = jnp.ones((HEAD_DIM,), F32)
    return pl.pallas_call(
        functools.partial(_proj_body, mode=mode),
        grid=(n // tn, t // tm),
        in_specs=[pl.BlockSpec((tm, k), lambda j, i: (i, 0)),
                  pl.BlockSpec((k, 1), lambda j, i: (0, 0)),
                  pl.BlockSpec((k, tn), lambda j, i: (0, j0 + j)),
                  pl.BlockSpec((1, HEAD_DIM), lambda j, i: (0, 0))],
        out_specs=pl.BlockSpec((tm, tn), lambda j, i: (i, j)),
        out_shape=jax.ShapeDtypeStruct((t, n), out_dtype),
        scratch_shapes=[pltpu.VMEM((k, tn), BF16)],
        compiler_params=_cparams(("arbitrary", "arbitrary")),
        name=name,
    )(x2, norm_g.reshape(k, 1).astype(F32), w, gain.reshape(1, HEAD_DIM).astype(F32))


def _ssm_body(u_ref, are_ref, aim_ref, wbre_ref, wbim_ref, wc_ref, d_ref, wglu_ref,
              o_ref, tm_ref, bure_ref, buim_ref, sre_ref, sim_ref, *, tc, lane_chunk):
    nb = u_ref.shape[0]
    ds = u_ref.shape[2]
    rows = nb * tc
    nslab = ds // LANES
    nblk = wbre_ref.shape[0]
    cb = wbre_ref.shape[1]
    sb = wbre_ref.shape[2]
    n_state = nblk * sb

    @pl.when(pl.program_id(0) == 0)
    def _():
        sre_ref[...] = jnp.zeros_like(sre_ref)
        sim_ref[...] = jnp.zeros_like(sim_ref)

    for b in range(nb):
        for j in range(nslab):
            tm_ref[j, pl.ds(b, tc, stride=nb), :] = u_ref[b, :, j * LANES:(j + 1) * LANES]
    u_tm = jnp.concatenate([tm_ref[j] for j in range(nslab)], axis=1).astype(BF16)
    for gb in range(nblk):
        lhs = u_tm[:, gb * cb:(gb + 1) * cb]
        bure_ref[:, gb * sb:(gb + 1) * sb] = jnp.dot(lhs, wbre_ref[gb], preferred_element_type=F32)
        buim_ref[:, gb * sb:(gb + 1) * sb] = jnp.dot(lhs, wbim_ref[gb], preferred_element_type=F32)

    for lc in range(n_state // lane_chunk):
        sl = slice(lc * lane_chunk, (lc + 1) * lane_chunk)
        ar = jnp.broadcast_to(are_ref[:, sl], (nb, lane_chunk))
        ai = jnp.broadcast_to(aim_ref[:, sl], (nb, lane_chunk))

        def steps(tt, carry, sl=sl, ar=ar, ai=ai):
            xr, xi = carry
            for k in range(SCAN_UNROLL):
                r0 = pl.multiple_of((tt * SCAN_UNROLL + k) * nb, nb)
                br = bure_ref[pl.ds(r0, nb), sl]
                bi = buim_ref[pl.ds(r0, nb), sl]
                xr, xi = ar * xr - ai * xi + br, ar * xi + ai * xr + bi
                bure_ref[pl.ds(r0, nb), sl] = xr
                buim_ref[pl.ds(r0, nb), sl] = xi
            return xr, xi

        xr, xi = lax.fori_loop(0, tc // SCAN_UNROLL, steps, (sre_ref[:, sl], sim_ref[:, sl]))
        sre_ref[:, sl] = xr
        sim_ref[:, sl] = xi

    for gb in range(nblk):
        xs = jnp.concatenate([bure_ref[:, gb * sb:(gb + 1) * sb], buim_ref[:, gb * sb:(gb + 1) * sb]], axis=1)
        y_blk = jnp.dot(xs.astype(BF16), wc_ref[gb], preferred_element_type=F32)
        for jj in range(cb // LANES):
            tm_ref[gb * (cb // LANES) + jj] = y_blk[:, jj * LANES:(jj + 1) * LANES]
    y_bm = jnp.concatenate(
        [jnp.concatenate([tm_ref[j, pl.ds(b, tc, stride=nb), :] for j in range(nslab)], axis=1) for b in range(nb)],
        axis=0)
    y = y_bm + d_ref[...] * u_ref[...].reshape(rows, ds)
    z = jax.nn.gelu(y)
    gate = jax.nn.sigmoid(jnp.dot(z.astype(BF16), wglu_ref[...], preferred_element_type=F32))
    o_ref[...] = (z * gate).reshape(nb, tc, ds).astype(o_ref.dtype)


def _ssm(u, a_re, a_im, wb_re, wb_im, wc, d_skip, w_glu, tc=SSM_TC, lane_chunk=SSM_LANE_CHUNK):
    nb, seq, ds = u.shape
    rows = nb * tc
    n_state = a_re.shape[1]
    assert tc % SCAN_UNROLL == 0
    const = lambda *shape: pl.BlockSpec(shape, lambda c: (0,) * len(shape))
    return pl.pallas_call(
        functools.partial(_ssm_body, tc=tc, lane_chunk=lane_chunk),
        grid=(seq // tc,),
        in_specs=[pl.BlockSpec((nb, tc, ds), lambda c: (0, c, 0)),
                  const(1, n_state), const(1, n_state),
                  const(*wb_re.shape), const(*wb_im.shape), const(*wc.shape),
                  const(1, ds), const(ds, ds)],
        out_specs=pl.BlockSpec((nb, tc, ds), lambda c: (0, c, 0)),
        out_shape=jax.ShapeDtypeStruct((nb, seq, ds), BF16),
        scratch_shapes=[pltpu.VMEM((ds // LANES, rows, LANES), F32),
                        pltpu.VMEM((rows, n_state), F32), pltpu.VMEM((rows, n_state), F32),
                        pltpu.VMEM((nb, n_state), F32), pltpu.VMEM((nb, n_state), F32)],
        compiler_params=_cparams(("arbitrary",)),
        name="s5_mixer",
    )(u, a_re, a_im, wb_re, wb_im, wc, d_skip.reshape(1, ds), w_glu)


def _ssm_params(lambda_re, lambda_im, log_dt, b_re, b_im, c_re, c_im):
    g, p = lambda_re.shape
    h = b_re.shape[2]
    nblk = g // SSM_GROUPS_PER_BLOCK
    dt = jnp.exp(log_dt.astype(F32))[:, None]
    lr = lambda_re.astype(F32)
    li = lambda_im.astype(F32)
    mag = jnp.exp(lr * dt)
    abar_re, abar_im = mag * jnp.cos(li * dt), mag * jnp.sin(li * dt)
    nr, ni = abar_re - 1.0, abar_im
    den = lr * lr + li * li
    coef_re = (nr * lr + ni * li) / den
    coef_im = (ni * lr - nr * li) / den
    bbar_re = coef_re[..., None] * b_re - coef_im[..., None] * b_im
    bbar_im = coef_re[..., None] * b_im + coef_im[..., None] * b_re
    gpb = SSM_GROUPS_PER_BLOCK

    def block_diag(m, rows_per_group, cols_per_group):
        r = jnp.arange(gpb * rows_per_group)[:, None] // rows_per_group
        c = jnp.arange(gpb * cols_per_group)[None, :] // cols_per_group
        return jnp.where(r == c, jnp.tile(m, (1, gpb, 1)), 0.0)

    def pack_b(bb):
        m = bb.reshape(nblk, gpb, p, h).transpose(0, 3, 1, 2).reshape(nblk, h, gpb * p)
        return block_diag(m, h, p)

    def pack_c(cc):
        m = cc.reshape(nblk, gpb, h, p).transpose(0, 3, 1, 2).reshape(nblk, p, gpb * h)
        return block_diag(m, p, h)

    wc = jnp.concatenate([pack_c(c_re.astype(F32)), -pack_c(c_im.astype(F32))], axis=1)
    return (abar_re.reshape(1, g * p), abar_im.reshape(1, g * p),
            pack_b(bbar_re).astype(BF16), pack_b(bbar_im).astype(BF16), wc.astype(BF16))


def _attn_body(q_ref, k_ref, v_ref, tri_ref, o_ref, acc_ref, rs_ref, *, tq, nhs):
    qi = pl.program_id(2)
    tri = tri_ref[...]
    row = lax.broadcasted_iota(I32, (nhs * tq, tq), 0) % tq
    col = lax.broadcasted_iota(I32, (nhs * tq, tq), 1)
    causal = col < row

    def sweep(j, masked):
        k0 = pl.multiple_of(j * tq, tq)
        heads = [slice(h * HEAD_DIM, (h + 1) * HEAD_DIM) for h in range(nhs)]
        z = jnp.concatenate(
            [lax.dot_general(q_ref[0, :, hs], k_ref[0, pl.ds(k0, tq), hs], (((1,), (1,)), ((), ())),
                             preferred_element_type=F32) for hs in heads], axis=0)
        sp = jnp.where(z > SOFTPLUS2_LINEAR_ABOVE, z, jnp.log(1.0 + jnp.exp2(z)) * LOG2_E)
        spm = jnp.where(causal, sp, 0.0) if masked else sp
        spb = spm.astype(BF16)
        tail = jnp.dot(spb, tri, preferred_element_type=F32)
        rsum = rs_ref[...]
        w = jnp.exp2(z - (sp + tail + jnp.concatenate([rsum] * (tq // LANES), axis=1)))
        if masked:
            w = jnp.where(causal, w, 0.0)
        wb = w.astype(BF16)
        for h, hs in enumerate(heads):
            acc_ref[h] += jnp.dot(wb[h * tq:(h + 1) * tq], v_ref[0, pl.ds(k0, tq), hs], preferred_element_type=F32)
        total = tail[:, 0:1] + spb[:, 0:1].astype(F32)
        rsum = rsum + jnp.broadcast_to(total, rsum.shape)
        rs_ref[...] = rsum
        return (jnp.min(rsum) < -EXP_ZERO_BELOW * LOG2_E).astype(I32)

    acc_ref[...] = jnp.zeros_like(acc_ref)
    rs_ref[...] = jnp.zeros_like(rs_ref)
    live = sweep(qi, True)

    def cond(c):
        return (c[0] >= 0) & (c[1] > 0)

    def body(c):
        return c[0] - 1, sweep(c[0], False)

    lax.while_loop(cond, body, (qi - 1, live))
    for h in range(nhs):
        o_ref[0, :, h * HEAD_DIM:(h + 1) * HEAD_DIM] = acc_ref[h].astype(o_ref.dtype)


def _attention(q, k, v, tq=ATTN_TQ, nhs=ATTN_HEADS_PER_STEP):
    nb, seq, da = q.shape
    nh = da // HEAD_DIM
    wd = nhs * HEAD_DIM
    r = jnp.arange(tq)
    tri = (r[:, None] > r[None, :]).astype(BF16)
    return pl.pallas_call(
        functools.partial(_attn_body, tq=tq, nhs=nhs),
        grid=(nb, nh // nhs, seq // tq),
        in_specs=[pl.BlockSpec((1, tq, wd), lambda b, h, i: (b, i, h)),
                  pl.BlockSpec((1, seq, wd), lambda b, h, i: (b, 0, h)),
                  pl.BlockSpec((1, seq, wd), lambda b, h, i: (b, 0, h)),
                  pl.BlockSpec((tq, tq), lambda b, h, i: (0, 0))],
        out_specs=pl.BlockSpec((1, tq, wd), lambda b, h, i: (b, i, h)),
        out_shape=jax.ShapeDtypeStruct((nb, seq, da), BF16),
        scratch_shapes=[pltpu.VMEM((nhs, tq, HEAD_DIM), F32), pltpu.VMEM((nhs * tq, LANES), F32)],
        compiler_params=_cparams(("arbitrary", "arbitrary", "arbitrary")),
        name="stick_attention",
    )(q, k, v, tri)


def _merge_body(ys_ref, ya_ref, gs_ref, ga_ref, ws_ref, wa_ref, o_ref, wsb_ref, wab_ref):
    @pl.when(pl.program_id(1) == 0)
    def _():
        wsb_ref[...] = ws_ref[...].astype(BF16)
        wab_ref[...] = wa_ref[...].astype(BF16)

    ps = jnp.dot(ys_ref[...], wsb_ref[...], preferred_element_type=F32)
    pa = jnp.dot(ya_ref[...], wab_ref[...], preferred_element_type=F32)
    o_ref[...] = (gs_ref[...].astype(F32) * ps + ga_ref[...].astype(F32) * pa).astype(o_ref.dtype)


def _merge(y_ssm, y_att, gates, w_s, w_a, tm=MERGE_TM, tn=MERGE_TN):
    t, ks = y_ssm.shape
    d = w_s.shape[1]
    nj = d // tn
    return pl.pallas_call(
        _merge_body,
        grid=(nj, t // tm),
        in_specs=[pl.BlockSpec((tm, ks), lambda j, i: (i, 0)),
                  pl.BlockSpec((tm, y_att.shape[1]), lambda j, i: (i, 0)),
                  pl.BlockSpec((tm, tn), lambda j, i: (i, j)),
                  pl.BlockSpec((tm, tn), lambda j, i: (i, nj + j)),
                  pl.BlockSpec((ks, tn), lambda j, i: (0, j)),
                  pl.BlockSpec((w_a.shape[0], tn), lambda j, i: (0, j))],
        out_specs=pl.BlockSpec((tm, tn), lambda j, i: (i, j)),
        out_shape=jax.ShapeDtypeStruct((t, d), BF16),
        scratch_shapes=[pltpu.VMEM((ks, tn), BF16), pltpu.VMEM((w_a.shape[0], tn), BF16)],
        compiler_params=_cparams(("arbitrary", "arbitrary")),
        name="branch_merge",
    )(y_ssm, y_att, gates, gates, w_s, w_a)


def _lane_min_index(mask, lane):
    return jnp.min(jnp.where(mask, lane, float(ROUTE_LANES)), axis=1, keepdims=True)


def _outproj_body(x_ref, m_ref, w_ref, g_ref, wrc_ref, br_ref,
                  x1_ref, hp_ref, rr_ref, rt_ref):
    x1 = x_ref[...] + jnp.dot(m_ref[...], w_ref[...], preferred_element_type=F32)
    x1_ref[...] = x1
    ms = jnp.mean(x1 * x1, axis=-1, keepdims=True)
    h2 = x1 * lax.rsqrt(ms + EPS) * g_ref[...]
    hp_ref[...] = h2
    hb = h2.astype(BF16)

    hl = (h2 - hb.astype(F32)).astype(BF16)
    p = jnp.dot(hb, wrc_ref[...], preferred_element_type=F32)
    logits = (p[:, :ROUTE_LANES] + p[:, ROUTE_LANES:]
              + jnp.dot(hl, wrc_ref[:, :ROUTE_LANES], preferred_element_type=F32)) + br_ref[...]
    tm = logits.shape[0]
    lane = lax.broadcasted_iota(I32, (tm, ROUTE_LANES), 1).astype(F32)
    is_group = (lane >= N_EXPERTS) & (lane < N_EXPERTS + N_EXPERT_GROUPS)
    gl = jnp.where(is_group, logits, NEG_BIG)
    gmax = jnp.max(gl, axis=1, keepdims=True)
    gidx = _lane_min_index(gl == gmax, lane) - N_EXPERTS
    g_top = 1.0 / jnp.sum(jnp.exp(gl - gmax), axis=1, keepdims=True)
    lo = gidx * EXPERTS_PER_GROUP
    in_grp = (lane >= lo) & (lane < lo + EXPERTS_PER_GROUP)
    el = jnp.where(in_grp, logits, NEG_BIG)
    m1 = jnp.max(el, axis=1, keepdims=True)
    i1 = _lane_min_index(el == m1, lane)
    el2 = jnp.where(lane == i1, NEG_BIG, el)
    m2 = jnp.max(el2, axis=1, keepdims=True)
    i2 = _lane_min_index(el2 == m2, lane)
    dlt = jnp.exp(m2 - m1)
    w1 = 1.0 / (1.0 + dlt)
    w2 = dlt * w1
    route = jnp.where(lane == 0, i1,
                      jnp.where(lane == 1, i2,
                                jnp.where(lane == 2, g_top * w1,
                                          jnp.where(lane == 3, g_top * w2, 0.0))))
    rr_ref[...] = route
    rt_ref[...] = jnp.transpose(route)[0:8, :]


def _outproj(x2, merged, w_out, g, wr, br, tm=OUTPROJ_TM):
    t, d = x2.shape
    wr_hi = wr.astype(BF16)
    wr_cat = jnp.concatenate([wr_hi, (wr - wr_hi.astype(F32)).astype(BF16)], axis=1)
    const = lambda *shape: pl.BlockSpec(shape, lambda i: (0,) * len(shape))
    return pl.pallas_call(
        _outproj_body,
        grid=(t // tm,),
        in_specs=[pl.BlockSpec((tm, d), lambda i: (i, 0)), pl.BlockSpec((tm, d), lambda i: (i, 0)),
                  const(d, d), const(1, d), const(d, 2 * ROUTE_LANES), const(1, ROUTE_LANES)],
        out_specs=[pl.BlockSpec((tm, d), lambda i: (i, 0)),
                   pl.BlockSpec((tm, d), lambda i: (i, 0)),
                   pl.BlockSpec((tm, ROUTE_LANES), lambda i: (i, 0)),
                   pl.BlockSpec((8, tm), lambda i: (0, i))],
        out_shape=[jax.ShapeDtypeStruct((t, d), F32),
                   jax.ShapeDtypeStruct((t, d), F32),
                   jax.ShapeDtypeStruct((t, ROUTE_LANES), F32),
                   jax.ShapeDtypeStruct((8, t), F32)],
        compiler_params=_cparams(("arbitrary",)),
        name="outproj_router",
    )(x2, merged, w_out, g.reshape(1, d), wr_cat, br)


def _count_body(ids_ref, cnt_ref):
    nc, _, c = ids_ref.shape
    eidx = lax.broadcasted_iota(I32, (N_EXPERTS, c), 0)

    def step(i, acc):
        return acc + jnp.where(ids_ref[i] == eidx, 1.0, 0.0)

    acc = lax.fori_loop(0, nc, step, jnp.zeros((N_EXPERTS, c), F32))
    tot = jnp.sum(acc, axis=1, keepdims=True)
    cnt_ref[...] = jnp.broadcast_to(tot, cnt_ref.shape).astype(I32)


def _expert_counts(ids3):
    return pl.pallas_call(
        _count_body,
        out_shape=jax.ShapeDtypeStruct((N_EXPERTS, 128), I32),
        name="expert_counts",
    )(ids3)


def _pos_body(ids_ref, start_ref, incl_ref, pos_ref):
    nc, _, c = ids_ref.shape
    eidx = lax.broadcasted_iota(I32, (N_EXPERTS, c), 0)
    incl = incl_ref[...]

    def step(i, carry):
        onehot = ids_ref[i] == eidx
        cum = jnp.dot(jnp.where(onehot, 1.0, 0.0).astype(BF16), incl, preferred_element_type=F32)
        val = jnp.where(onehot, cum - 1.0 + carry, 0.0)
        pos_ref[i] = jnp.sum(val, axis=0, keepdims=True).astype(I32)
        return carry + cum[:, c - 1:c]

    lax.fori_loop(0, nc, step, start_ref[:, 0:1].astype(F32))


def _sorted_positions(ids3, starts):
    nc, _, c = ids3.shape
    r = jnp.arange(c)
    incl = (r[:, None] <= r[None, :]).astype(BF16)
    start_b = jnp.broadcast_to(starts.astype(I32)[:, None], (N_EXPERTS, 128))
    return pl.pallas_call(
        _pos_body,
        out_shape=jax.ShapeDtypeStruct((nc, 1, c), I32),
        name="sorted_positions",
    )(ids3, start_b, incl)


def _dispatch_body(pos_ref, src_ref, dst_ref, buf, lsem, ssem, *, n_tok, tm):
    i = pl.program_id(0)
    nsteps = pl.num_programs(0)
    base = i * tm

    def load(tile):
        b = tile % DISPATCH_BUFS
        return pltpu.make_async_copy(src_ref.at[pl.ds(tile * tm, tm), :], buf.at[b], lsem.at[b])

    def drain(tile):
        b = tile % DISPATCH_BUFS
        for s in range(2):
            pltpu.make_async_copy(buf.at[b], dst_ref.at[pl.ds(0, tm), :], ssem.at[2 * b + s]).wait()

    @pl.when(i == 0)
    def _():
        load(0).start()

        @pl.when(nsteps > 1)
        def _():
            load(1).start()

    load(i).wait()
    b = i % DISPATCH_BUFS

    def issue(ii, carry):
        for k in range(DMA_UNROLL):
            j = ii * DMA_UNROLL + k
            row = buf.at[b, pl.ds(j, 1), :]
            pltpu.make_async_copy(row, dst_ref.at[pl.ds(pos_ref[base + j], 1), :], ssem.at[2 * b]).start()
            pltpu.make_async_copy(row, dst_ref.at[pl.ds(pos_ref[n_tok + base + j], 1), :], ssem.at[2 * b + 1]).start()
        return carry

    lax.fori_loop(0, tm // DMA_UNROLL, issue, 0)

    @pl.when(i >= 1)
    def _():
        drain(i - 1)

    @pl.when(i + 2 < nsteps)
    def _():
        load(i + 2).start()

    @pl.when(i == nsteps - 1)
    def _():
        drain(i)


def _dispatch(pos_flat, hp, tm=MOE_TM):
    n = pos_flat.shape[0]
    t, d = hp.shape
    assert t % tm == 0 and tm % DMA_UNROLL == 0
    return pl.pallas_call(
        functools.partial(_dispatch_body, n_tok=t, tm=tm),
        grid_spec=pltpu.PrefetchScalarGridSpec(
            num_scalar_prefetch=1, grid=(t // tm,),
            in_specs=[pl.BlockSpec(memory_space=pl.ANY)],
            out_specs=pl.BlockSpec(memory_space=pl.ANY),
            scratch_shapes=[pltpu.VMEM((DISPATCH_BUFS, tm, d), hp.dtype),
                            pltpu.SemaphoreType.DMA((DISPATCH_BUFS,)),
                            pltpu.SemaphoreType.DMA((2 * DISPATCH_BUFS,))]),
        out_shape=jax.ShapeDtypeStruct((n, d), hp.dtype),
        compiler_params=_cparams(("arbitrary",)),
        name="dispatch",
    )(pos_flat, hp)


def _gmm_body(tile_ref, exp_ref, nxt_ref, lo_ref, hi_ref, x_ref, wg_ref, wu_ref, wd_ref, o_ref,
              sg, su, sd, wgs, wus, wds, sem):
    w = pl.program_id(0)
    prev = jnp.maximum(w - 1, 0)
    new_expert = (w == 0) | (exp_ref[w] != exp_ref[prev])
    first_of_tile = (w == 0) | (tile_ref[w] != tile_ref[prev])
    lo = lo_ref[w]
    hi = hi_ref[w]

    def stage(e):
        return (pltpu.make_async_copy(wg_ref.at[e], sg, sem.at[0]),
                pltpu.make_async_copy(wu_ref.at[e], su, sem.at[1]),
                pltpu.make_async_copy(wd_ref.at[e], sd, sem.at[2]))

    @pl.when(w == 0)
    def _():
        for cp in stage(exp_ref[0]):
            cp.start()

    @pl.when(new_expert)
    def _():
        for cp in stage(exp_ref[w]):
            cp.wait()
        wgs[...] = sg[...].astype(BF16)
        wus[...] = su[...].astype(BF16)
        wds[...] = sd[...].astype(BF16)

        @pl.when(nxt_ref[w] >= 0)
        def _():
            for cp in stage(nxt_ref[w]):
                cp.start()

    @pl.when(hi > lo)
    def _():
        xx = x_ref[...].astype(BF16)
        g = jnp.dot(xx, wgs[...], preferred_element_type=F32)
        u = jnp.dot(xx, wus[...], preferred_element_type=F32)
        hid = (g * jax.nn.sigmoid(g)) * u
        y = jnp.dot(hid.astype(BF16), wds[...], preferred_element_type=F32)
        rows = lax.broadcasted_iota(I32, (y.shape[0], 1), 0)
        y = jnp.where((rows >= lo) & (rows < hi), y, 0.0)

        @pl.when(first_of_tile)
        def _():
            o_ref[...] = y

        @pl.when(jnp.logical_not(first_of_tile))
        def _():
            o_ref[...] += y


def _gmm(meta, xs, w_gate, w_up, w_down, tm):
    tile_id, expert_id, next_expert, row_lo, row_hi = meta
    n, _ = xs.shape
    e, d, f = w_gate.shape
    nw = tile_id.shape[0]
    hbm = pl.BlockSpec(memory_space=pl.ANY)
    row_tile = lambda w, ti, *_: (ti[w], 0)
    return pl.pallas_call(
        _gmm_body,
        grid_spec=pltpu.PrefetchScalarGridSpec(
            num_scalar_prefetch=5, grid=(nw,),
            in_specs=[pl.BlockSpec((tm, d), row_tile), hbm, hbm, hbm],
            out_specs=pl.BlockSpec((tm, d), row_tile),
            scratch_shapes=[pltpu.VMEM((d, f), F32), pltpu.VMEM((d, f), F32), pltpu.VMEM((f, d), F32),
                            pltpu.VMEM((d, f), BF16), pltpu.VMEM((d, f), BF16), pltpu.VMEM((f, d), BF16),
                            pltpu.SemaphoreType.DMA((3,))]),
        out_shape=jax.ShapeDtypeStruct((n, d), F32),
        compiler_params=_cparams(("arbitrary",)),
        name="expert_gmm",
    )(tile_id, expert_id, next_expert, row_lo, row_hi, xs, w_gate, w_up, w_down)


def _gmm_metadata(counts, n_rows, tm):
    nt = n_rows // tm
    nw = nt + N_EXPERTS - 1
    ends = jnp.cumsum(counts)
    starts = ends - counts
    first_tile = starts // tm
    n_items = jnp.where(counts > 0, (ends - 1) // tm - first_tile + 1, 0)
    item_end = jnp.cumsum(n_items)
    item_start = item_end - n_items
    total = item_end[-1]
    w = jnp.arange(nw, dtype=I32)
    wc = jnp.minimum(w, total - 1)
    ex = jnp.sum((item_end[None, :] <= wc[:, None]).astype(I32), axis=1)
    tile = first_tile[ex] + (wc - item_start[ex])
    lo = jnp.maximum(starts[ex], tile * tm) - tile * tm
    hi = jnp.minimum(ends[ex], (tile + 1) * tm) - tile * tm
    valid = w < total
    lo = jnp.where(valid, lo, 0)
    hi = jnp.where(valid, hi, 0)
    run_end = item_end[ex]
    nxt = jnp.where(run_end < total, ex[jnp.minimum(run_end, nw - 1)], -1)
    return tile.astype(I32), ex, nxt.astype(I32), lo.astype(I32), hi.astype(I32)


def _combine_body(pos_ref, x1_ref, rr_ref, ys_ref, o_ref, a0, a1, b0, b1, sem, *, n_tok):
    i = pl.program_id(0)
    nsteps = pl.num_programs(0)
    tm = x1_ref.shape[0]

    def gather(tile, bufs, sems):
        base = tile * tm

        def issue(jj, carry):
            for k in range(DMA_UNROLL):
                j = jj * DMA_UNROLL + k
                pltpu.make_async_copy(ys_ref.at[pl.ds(pos_ref[base + j], 1), :],
                                      bufs[0].at[pl.ds(j, 1), :], sem.at[sems[0]]).start()
                pltpu.make_async_copy(ys_ref.at[pl.ds(pos_ref[n_tok + base + j], 1), :],
                                      bufs[1].at[pl.ds(j, 1), :], sem.at[sems[1]]).start()
            return carry

        lax.fori_loop(0, tm // DMA_UNROLL, issue, 0)

    def step(cur, cur_sems, nxt, nxt_sems):
        @pl.when(i == 0)
        def _():
            gather(i, cur, cur_sems)

        @pl.when(i + 1 < nsteps)
        def _():
            gather(i + 1, nxt, nxt_sems)

        for buf, s in zip(cur, cur_sems):
            pltpu.make_async_copy(ys_ref.at[pl.ds(0, tm), :], buf, sem.at[s]).wait()
        rr = rr_ref[...]
        o_ref[...] = x1_ref[...] + rr[:, 2:3] * cur[0][...] + rr[:, 3:4] * cur[1][...]

    @pl.when(i % 2 == 0)
    def _():
        step((a0, a1), (0, 1), (b0, b1), (2, 3))

    @pl.when(i % 2 == 1)
    def _():
        step((b0, b1), (2, 3), (a0, a1), (0, 1))


def _combine(pos_flat, x1, route_rows, ys, tm=MOE_TM):
    t, d = x1.shape
    assert tm % DMA_UNROLL == 0
    buf = pltpu.VMEM((tm, d), F32)
    return pl.pallas_call(
        functools.partial(_combine_body, n_tok=t),
        grid_spec=pltpu.PrefetchScalarGridSpec(
            num_scalar_prefetch=1, grid=(t // tm,),
            in_specs=[pl.BlockSpec((tm, d), lambda i, p: (i, 0)),
                      pl.BlockSpec((tm, ROUTE_LANES), lambda i, p: (i, 0)),
                      pl.BlockSpec(memory_space=pl.ANY)],
            out_specs=pl.BlockSpec((tm, d), lambda i, p: (i, 0)),
            scratch_shapes=[buf, buf, buf, buf, pltpu.SemaphoreType.DMA((4,))]),
        out_shape=jax.ShapeDtypeStruct((t, d), F32),
        compiler_params=_cparams(("arbitrary",)),
        name="combine",
    )(pos_flat, x1, route_rows, ys)


def _moe(x1, hp, route_rows, route_t, w_gate, w_up, w_down, gmm_tm=MOE_TM, sort_chunk=SORT_CHUNK):
    t = x1.shape[0]
    n = 2 * t
    ids3 = route_t[0:2].astype(I32).reshape(n // sort_chunk, 1, sort_chunk)
    counts = _expert_counts(ids3)[:, 0]
    starts = jnp.cumsum(counts) - counts
    pos_flat = _sorted_positions(ids3, starts).reshape(n)
    xs = _dispatch(pos_flat, hp)
    ys = _gmm(_gmm_metadata(counts, n, gmm_tm), xs, w_gate, w_up, w_down, gmm_tm)
    return _combine(pos_flat, x1, route_rows, ys)


def _layer(x, attn_norm_g, w_in, lambda_re, lambda_im, log_dt, ssm_b_re, ssm_b_im, ssm_c_re, ssm_c_im,
           ssm_d, w_glu, q_norm_g, k_norm_g, w_branch_ssm, w_branch_att, w_out, ffn_norm_g,
           router_group_w, router_group_b, router_expert_w, router_expert_b,
           expert_w_gate, expert_w_up, expert_w_down):
    nb, seq, d = x.shape
    t = nb * seq
    ds = w_glu.shape[0]
    da = N_HEADS * HEAD_DIM
    x2 = x.reshape(t, d)

    o = 0
    u = _proj(x2, attn_norm_g, w_in, o, ds, "plain", F32, name="proj_u"); o += ds
    qg = q_norm_g.astype(F32) * (HEAD_DIM ** -0.5 * LOG2_E)
    q = _proj(x2, attn_norm_g, w_in, o, da, "headnorm", BF16, gain=qg, name="proj_q"); o += da
    k = _proj(x2, attn_norm_g, w_in, o, da, "headnorm", BF16, gain=k_norm_g, name="proj_k"); o += da
    v = _proj(x2, attn_norm_g, w_in, o, da, "plain", BF16, name="proj_v"); o += da
    gates = _proj(x2, attn_norm_g, w_in, o, w_in.shape[1] - o, "sigmoid", BF16, name="proj_gates")

    a_re, a_im, wb_re, wb_im, wc = _ssm_params(lambda_re, lambda_im, log_dt, ssm_b_re, ssm_b_im, ssm_c_re, ssm_c_im)
    y_ssm = _ssm(u.reshape(nb, seq, ds), a_re, a_im, wb_re, wb_im, wc, ssm_d.astype(F32), w_glu.astype(BF16))
    y_att = _attention(q.reshape(nb, seq, da), k.reshape(nb, seq, da), v.reshape(nb, seq, da))

    merged = _merge(y_ssm.reshape(t, ds), y_att.reshape(t, da), gates, w_branch_ssm, w_branch_att)

    n_pad = ROUTE_LANES - N_EXPERTS - N_EXPERT_GROUPS
    wr = jnp.concatenate([router_expert_w.astype(F32), router_group_w.astype(F32), jnp.zeros((d, n_pad), F32)], axis=1)
    br = jnp.concatenate([router_expert_b.astype(F32), router_group_b.astype(F32), jnp.zeros((n_pad,), F32)])[None, :]
    x1, hp, route_rows, route_t = _outproj(x2, merged, w_out.astype(BF16), ffn_norm_g, wr, br)

    out = _moe(x1, hp, route_rows, route_t, expert_w_gate, expert_w_up, expert_w_down)
    return out.reshape(nb, seq, d)


def kernel(x, attn_norm_g, w_in, lambda_re, lambda_im, log_dt, ssm_b_re, ssm_b_im, ssm_c_re, ssm_c_im, ssm_d, w_glu, q_norm_g, k_norm_g, w_branch_ssm, w_branch_att, w_out, ffn_norm_g, router_group_w, router_group_b, router_expert_w, router_expert_b, expert_w_gate, expert_w_up, expert_w_down):
    depth = attn_norm_g.shape[0]
    for l in range(depth):
        x = _layer(x, attn_norm_g[l], w_in[l], lambda_re[l], lambda_im[l], log_dt[l], ssm_b_re[l], ssm_b_im[l],
                   ssm_c_re[l], ssm_c_im[l], ssm_d[l], w_glu[l], q_norm_g[l], k_norm_g[l], w_branch_ssm[l],
                   w_branch_att[l], w_out[l], ffn_norm_g[l], router_group_w[l], router_group_b[l],
                   router_expert_w[l], router_expert_b[l], expert_w_gate[l], expert_w_up[l], expert_w_down[l])
    return x
```

```python
import functools

import jax
import jax.numpy as jnp
from jax import lax
from jax.experimental import pallas as pl
from jax.experimental.pallas import tpu as pltpu

F32 = jnp.float32
BF16 = jnp.bfloat16
I32 = jnp.int32

EPS = 1e-6
N_HEADS = 8
HEAD_DIM = 128
N_EXPERT_GROUPS = 4
EXPERTS_PER_GROUP = 8
N_EXPERTS = N_EXPERT_GROUPS * EXPERTS_PER_GROUP

LANES = 128
V7X_VMEM_BYTES = 64 * 1024 * 1024
V7X_VMEM_RESERVED_BYTES = 8 * 1024 * 1024
VMEM_LIMIT = V7X_VMEM_BYTES - V7X_VMEM_RESERVED_BYTES

PROJ_TM, PROJ_TN = 1024, 1024
MERGE_TM, MERGE_TN = 1024, 1024
OUTPROJ_TM = 512
SSM_TC = 64
SSM_LANE_CHUNK = 1024
SSM_GROUPS_PER_BLOCK = 16
ATTN_TQ = 256
ATTN_HEADS_PER_STEP = 8
SORT_CHUNK = 512
MOE_TM = 256
DMA_UNROLL = 16
SCAN_UNROLL = 4
DISPATCH_BUFS = 3

ROUTE_LANES = LANES
NEG_BIG = -1e30
EXP_ZERO_BELOW = -104.0
LOG2_E = 1.4426950408889634
SOFTPLUS2_LINEAR_ABOVE = 30.0


def _cparams(sem):
    return pltpu.CompilerParams(dimension_semantics=sem, vmem_limit_bytes=VMEM_LIMIT)


def _proj_body(x_ref, gn_ref, w_ref, g_ref, o_ref, wb_ref, *, mode):
    @pl.when(pl.program_id(1) == 0)
    def _():
        wb_ref[...] = (w_ref[...] * gn_ref[...]).astype(BF16)

    x = x_ref[...]
    acc = jnp.dot(x.astype(BF16), wb_ref[...], preferred_element_type=F32)
    acc = acc * lax.rsqrt(jnp.mean(x * x, axis=-1, keepdims=True) + EPS)
    if mode == "plain":
        o_ref[...] = acc.astype(o_ref.dtype)
    elif mode == "sigmoid":
        o_ref[...] = jax.nn.sigmoid(acc).astype(o_ref.dtype)
    else:
        n = acc.shape[1] // HEAD_DIM
        for hh in range(n):
            blk = acc[:, hh * HEAD_DIM:(hh + 1) * HEAD_DIM]
            ms = jnp.mean(blk * blk, axis=-1, keepdims=True)
            o_ref[:, hh * HEAD_DIM:(hh + 1) * HEAD_DIM] = (blk * lax.rsqrt(ms + EPS) * g_ref[...]).astype(o_ref.dtype)


def _proj(x2, norm_g, w, col0, n, mode, out_dtype, gain=None, tm=PROJ_TM, tn=PROJ_TN, name="proj"):
    t, k = x2.shape
    assert col0 % tn == 0 and n % tn == 0
    j0 = col0 // tn
    if gain is None:
        gain = jnp.ones((HEAD_DIM,), F32)
    return pl.pallas_call(
        functools.partial(_proj_body, mode=mode),
        grid=(n // tn, t // tm),
        in_specs=[pl.BlockSpec((tm, k), lambda j, i: (i, 0)),
                  pl.BlockSpec((k, 1), lambda j, i: (0, 0)),
                  pl.BlockSpec((k, tn), lambda j, i: (0, j0 + j)),
                  pl.BlockSpec((1, HEAD_DIM), lambda j, i: (0, 0))],
        out_specs=pl.BlockSpec((tm, tn), lambda j, i: (i, j)),
        out_shape=jax.ShapeDtypeStruct((t, n), out_dtype),
        scratch_shapes=[pltpu.VMEM((k, tn), BF16)],
        compiler_params=_cparams(("arbitrary", "arbitrary")),
        name=name,
    )(x2, norm_g.reshape(k, 1).astype(F32), w, gain.reshape(1, HEAD_DIM).astype(F32))


def _ssm_body(u_ref, are_ref, aim_ref, wbre_ref, wbim_ref, wc_ref, d_ref, wglu_ref,
              o_ref, tm_ref, bure_ref, buim_ref, sre_ref, sim_ref, *, tc, lane_chunk):
    nb = u_ref.shape[0]
    ds = u_ref.shape[2]
    rows = nb * tc
    nslab = ds // LANES
    nblk = wbre_ref.shape[0]
    cb = wbre_ref.shape[1]
    sb = wbre_ref.shape[2]
    n_state = nblk * sb

    @pl.when(pl.program_id(0) == 0)
    def _():
        sre_ref[...] = jnp.zeros_like(sre_ref)
        sim_ref[...] = jnp.zeros_like(sim_ref)

    for b in range(nb):
        for j in range(nslab):
            tm_ref[j, pl.ds(b, tc, stride=nb), :] = u_ref[b, :, j * LANES:(j + 1) * LANES]
    u_tm = jnp.concatenate([tm_ref[j] for j in range(nslab)], axis=1).astype(BF16)
    for gb in range(nblk):
        lhs = u_tm[:, gb * cb:(gb + 1) * cb]
        bure_ref[:, gb * sb:(gb + 1) * sb] = jnp.dot(lhs, wbre_ref[gb], preferred_element_type=F32)
        buim_ref[:, gb * sb:(gb + 1) * sb] = jnp.dot(lhs, wbim_ref[gb], preferred_element_type=F32)

    for lc in range(n_state // lane_chunk):
        sl = slice(lc * lane_chunk, (lc + 1) * lane_chunk)
        ar = jnp.broadcast_to(are_ref[:, sl], (nb, lane_chunk))
        ai = jnp.broadcast_to(aim_ref[:, sl], (nb, lane_chunk))

        def steps(tt, carry, sl=sl, ar=ar, ai=ai):
            xr, xi = carry
            for k in range(SCAN_UNROLL):
                r0 = pl.multiple_of((tt * SCAN_UNROLL + k) * nb, nb)
                br = bure_ref[pl.ds(r0, nb), sl]
                bi = buim_ref[pl.ds(r0, nb), sl]
                xr, xi = ar * xr - ai * xi + br, ar * xi + ai * xr + bi
                bure_ref[pl.ds(r0, nb), sl] = xr
                buim_ref[pl.ds(r0, nb), sl] = xi
            return xr, xi

        xr, xi = lax.fori_loop(0, tc // SCAN_UNROLL, steps, (sre_ref[:, sl], sim_ref[:, sl]))
        sre_ref[:, sl] = xr
        sim_ref[:, sl] = xi

    for gb in range(nblk):
        xs = jnp.concatenate([bure_ref[:, gb * sb:(gb + 1) * sb], buim_ref[:, gb * sb:(gb + 1) * sb]], axis=1)
        y_blk = jnp.dot(xs.astype(BF16), wc_ref[gb], preferred_element_type=F32)
        for jj in range(cb // LANES):
            tm_ref[gb * (cb // LANES) + jj] = y_blk[:, jj * LANES:(jj + 1) * LANES]
    y_bm = jnp.concatenate(
        [jnp.concatenate([tm_ref[j, pl.ds(b, tc, stride=nb), :] for j in range(nslab)], axis=1) for b in range(nb)],
        axis=0)
    y = y_bm + d_ref[...] * u_ref[...].reshape(rows, ds)
    z = jax.nn.gelu(y)
    gate = jax.nn.sigmoid(jnp.dot(z.astype(BF16), wglu_ref[...], preferred_element_type=F32))
    o_ref[...] = (z * gate).reshape(nb, tc, ds).astype(o_ref.dtype)


def _ssm(u, a_re, a_im, wb_re, wb_im, wc, d_skip, w_glu, tc=SSM_TC, lane_chunk=SSM_LANE_CHUNK):
    nb, seq, ds = u.shape
    rows = nb * tc
    n_state = a_re.shape[1]
    assert tc % SCAN_UNROLL == 0
    const = lambda *shape: pl.BlockSpec(shape, lambda c: (0,) * len(shape))
    return pl.pallas_call(
        functools.partial(_ssm_body, tc=tc, lane_chunk=lane_chunk),
        grid=(seq // tc,),
        in_specs=[pl.BlockSpec((nb, tc, ds), lambda c: (0, c, 0)),
                  const(1, n_state), const(1, n_state),
                  const(*wb_re.shape), const(*wb_im.shape), const(*wc.shape),
                  const(1, ds), const(ds, ds)],
        out_specs=pl.BlockSpec((nb, tc, ds), lambda c: (0, c, 0)),
        out_shape=jax.ShapeDtypeStruct((nb, seq, ds), BF16),
        scratch_shapes=[pltpu.VMEM((ds // LANES, rows, LANES), F32),
                        pltpu.VMEM((rows, n_state), F32), pltpu.VMEM((rows, n_state), F32),
                        pltpu.VMEM((nb, n_state), F32), pltpu.VMEM((nb, n_state), F32)],
        compiler_params=_cparams(("arbitrary",)),
        name="s5_mixer",
    )(u, a_re, a_im, wb_re, wb_im, wc, d_skip.reshape(1, ds), w_glu)


def _ssm_params(lambda_re, lambda_im, log_dt, b_re, b_im, c_re, c_im):
    g, p = lambda_re.shape
    h = b_re.shape[2]
    nblk = g // SSM_GROUPS_PER_BLOCK
    dt = jnp.exp(log_dt.astype(F32))[:, None]
    lr = lambda_re.astype(F32)
    li = lambda_im.astype(F32)
    mag = jnp.exp(lr * dt)
    abar_re, abar_im = mag * jnp.cos(li * dt), mag * jnp.sin(li * dt)
    nr, ni = abar_re - 1.0, abar_im
    den = lr * lr + li * li
    coef_re = (nr * lr + ni * li) / den
    coef_im = (ni * lr - nr * li) / den
    bbar_re = coef_re[..., None] * b_re - coef_im[..., None] * b_im
    bbar_im = coef_re[..., None] * b_im + coef_im[..., None] * b_re
    gpb = SSM_GROUPS_PER_BLOCK

    def block_diag(m, rows_per_group, cols_per_group):
        r = jnp.arange(gpb * rows_per_group)[:, None] // rows_per_group
        c = jnp.arange(gpb * cols_per_group)[None, :] // cols_per_group
        return jnp.where(r == c, jnp.tile(m, (1, gpb, 1)), 0.0)

    def pack_b(bb):
        m = bb.reshape(nblk, gpb, p, h).transpose(0, 3, 1, 2).reshape(nblk, h, gpb * p)
        return block_diag(m, h, p)

    def pack_c(cc):
        m = cc.reshape(nblk, gpb, h, p).transpose(0, 3, 1, 2).reshape(nblk, p, gpb * h)
        return block_diag(m, p, h)

    wc = jnp.concatenate([pack_c(c_re.astype(F32)), -pack_c(c_im.astype(F32))], axis=1)
    return (abar_re.reshape(1, g * p), abar_im.reshape(1, g * p),
            pack_b(bbar_re).astype(BF16), pack_b(bbar_im).astype(BF16), wc.astype(BF16))


def _attn_body(q_ref, k_ref, v_ref, tri_ref, o_ref, acc_ref, rs_ref, *, tq, nhs):
    qi = pl.program_id(2)
    tri = tri_ref[...]
    row = lax.broadcasted_iota(I32, (nhs * tq, tq), 0) % tq
    col = lax.broadcasted_iota(I32, (nhs * tq, tq), 1)
    causal = col < row

    def sweep(j, masked):
        k0 = pl.multiple_of(j * tq, tq)
        heads = [slice(h * HEAD_DIM, (h + 1) * HEAD_DIM) for h in range(nhs)]
        z = jnp.concatenate(
            [lax.dot_general(q_ref[0, :, hs], k_ref[0, pl.ds(k0, tq), hs], (((1,), (1,)), ((), ())),
                             preferred_element_type=F32) for hs in heads], axis=0)
        sp = jnp.where(z > SOFTPLUS2_LINEAR_ABOVE, z, jnp.log(1.0 + jnp.exp2(z)) * LOG2_E)
        spm = jnp.where(causal, sp, 0.0) if masked else sp
        spb = spm.astype(BF16)
        tail = jnp.dot(spb, tri, preferred_element_type=F32)
        rsum = rs_ref[...]
        w = jnp.exp2(z - (sp + tail + jnp.concatenate([rsum] * (tq // LANES), axis=1)))
        if masked:
            w = jnp.where(causal, w, 0.0)
        wb = w.astype(BF16)
        for h, hs in enumerate(heads):
            acc_ref[h] += jnp.dot(wb[h * tq:(h + 1) * tq], v_ref[0, pl.ds(k0, tq), hs], preferred_element_type=F32)
        total = tail[:, 0:1] + spb[:, 0:1].astype(F32)
        rsum = rsum + jnp.broadcast_to(total, rsum.shape)
        rs_ref[...] = rsum
        return (jnp.min(rsum) < -EXP_ZERO_BELOW * LOG2_E).astype(I32)

    acc_ref[...] = jnp.zeros_like(acc_ref)
    rs_ref[...] = jnp.zeros_like(rs_ref)
    live = sweep(qi, True)

    def cond(c):
        return (c[0] >= 0) & (c[1] > 0)

    def body(c):
        return c[0] - 1, sweep(c[0], False)

    lax.while_loop(cond, body, (qi - 1, live))
    for h in range(nhs):
        o_ref[0, :, h * HEAD_DIM:(h + 1) * HEAD_DIM] = acc_ref[h].astype(o_ref.dtype)


def _attention(q, k, v, tq=ATTN_TQ, nhs=ATTN_HEADS_PER_STEP):
    nb, seq, da = q.shape
    nh = da // HEAD_DIM
    wd = nhs * HEAD_DIM
    r = jnp.arange(tq)
    tri = (r[:, None] > r[None, :]).astype(BF16)
    return pl.pallas_call(
        functools.partial(_attn_body, tq=tq, nhs=nhs),
        grid=(nb, nh // nhs, seq // tq),
        in_specs=[pl.BlockSpec((1, tq, wd), lambda b, h, i: (b, i, h)),
                  pl.BlockSpec((1, seq, wd), lambda b, h, i: (b, 0, h)),
                  pl.BlockSpec((1, seq, wd), lambda b, h, i: (b, 0, h)),
                  pl.BlockSpec((tq, tq), lambda b, h, i: (0, 0))],
        out_specs=pl.BlockSpec((1, tq, wd), lambda b, h, i: (b, i, h)),
        out_shape=jax.ShapeDtypeStruct((nb, seq, da), BF16),
        scratch_shapes=[pltpu.VMEM((nhs, tq, HEAD_DIM), F32), pltpu.VMEM((nhs * tq, LANES), F32)],
        compiler_params=_cparams(("arbitrary", "arbitrary", "arbitrary")),
        name="stick_attention",
    )(q, k, v, tri)


def _merge_body(ys_ref, ya_ref, gs_ref, ga_ref, ws_ref, wa_ref, o_ref, wsb_ref, wab_ref):
    @pl.when(pl.program_id(1) == 0)
    def _():
        wsb_ref[...] = ws_ref[...].astype(BF16)
        wab_ref[...] = wa_ref[...].astype(BF16)

    ps = jnp.dot(ys_ref[...], wsb_ref[...], preferred_element_type=F32)
    pa = jnp.dot(ya_ref[...], wab_ref[...], preferred_element_type=F32)
    o_ref[...] = (gs_ref[...].astype(F32) * ps + ga_ref[...].astype(F32) * pa).astype(o_ref.dtype)


def _merge(y_ssm, y_att, gates, w_s, w_a, tm=MERGE_TM, tn=MERGE_TN):
    t, ks = y_ssm.shape
    d = w_s.shape[1]
    nj = d // tn
    return pl.pallas_call(
        _merge_body,
        grid=(nj, t // tm),
        in_specs=[pl.BlockSpec((tm, ks), lambda j, i: (i, 0)),
                  pl.BlockSpec((tm, y_att.shape[1]), lambda j, i: (i, 0)),
                  pl.BlockSpec((tm, tn), lambda j, i: (i, j)),
                  pl.BlockSpec((tm, tn), lambda j, i: (i, nj + j)),
                  pl.BlockSpec((ks, tn), lambda j, i: (0, j)),
                  pl.BlockSpec((w_a.shape[0], tn), lambda j, i: (0, j))],
        out_specs=pl.BlockSpec((tm, tn), lambda j, i: (i, j)),
        out_shape=jax.ShapeDtypeStruct((t, d), BF16),
        scratch_shapes=[pltpu.VMEM((ks, tn), BF16), pltpu.VMEM((w_a.shape[0], tn), BF16)],
        compiler_params=_cparams(("arbitrary", "arbitrary")),
        name="branch_merge",
    )(y_ssm, y_att, gates, gates, w_s, w_a)


def _lane_min_index(mask, lane):
    return jnp.min(jnp.where(mask, lane, float(ROUTE_LANES)), axis=1, keepdims=True)


def _outproj_body(x_ref, m_ref, w_ref, g_ref, wrc_ref, br_ref,
                  x1_ref, hp_ref, rr_ref, rt_ref):
    x1 = x_ref[...] + jnp.dot(m_ref[...], w_ref[...], preferred_element_type=F32)
    x1_ref[...] = x1
    ms = jnp.mean(x1 * x1, axis=-1, keepdims=True)
    h2 = x1 * lax.rsqrt(ms + EPS) * g_ref[...]
    hp_ref[...] = h2
    hb = h2.astype(BF16)

    hl = (h2 - hb.astype(F32)).astype(BF16)
    p = jnp.dot(hb, wrc_ref[...], preferred_element_type=F32)
    logits = (p[:, :ROUTE_LANES] + p[:, ROUTE_LANES:]
              + jnp.dot(hl, wrc_ref[:, :ROUTE_LANES], preferred_element_type=F32)) + br_ref[...]
    tm = logits.shape[0]
    lane = lax.broadcasted_iota(I32, (tm, ROUTE_LANES), 1).astype(F32)
    is_group = (lane >= N_EXPERTS) & (lane < N_EXPERTS + N_EXPERT_GROUPS)
    gl = jnp.where(is_group, logits, NEG_BIG)
    gmax = jnp.max(gl, axis=1, keepdims=True)
    gidx = _lane_min_index(gl == gmax, lane) - N_EXPERTS
    g_top = 1.0 / jnp.sum(jnp.exp(gl - gmax), axis=1, keepdims=True)
    lo = gidx * EXPERTS_PER_GROUP
    in_grp = (lane >= lo) & (lane < lo + EXPERTS_PER_GROUP)
    el = jnp.where(in_grp, logits, NEG_BIG)
    m1 = jnp.max(el, axis=1, keepdims=True)
    i1 = _lane_min_index(el == m1, lane)
    el2 = jnp.where(lane == i1, NEG_BIG, el)
    m2 = jnp.max(el2, axis=1, keepdims=True)
    i2 = _lane_min_index(el2 == m2, lane)
    dlt = jnp.exp(m2 - m1)
    w1 = 1.0 / (1.0 + dlt)
    w2 = dlt * w1
    route = jnp.where(lane == 0, i1,
                      jnp.where(lane == 1, i2,
                                jnp.where(lane == 2, g_top * w1,
                                          jnp.where(lane == 3, g_top * w2, 0.0))))
    rr_ref[...] = route
    rt_ref[...] = jnp.transpose(route)[0:8, :]


def _outproj(x2, merged, w_out, g, wr, br, tm=OUTPROJ_TM):
    t, d = x2.shape
    wr_hi = wr.astype(BF16)
    wr_cat = jnp.concatenate([wr_hi, (wr - wr_hi.astype(F32)).astype(BF16)], axis=1)
    const = lambda *shape: pl.BlockSpec(shape, lambda i: (0,) * len(shape))
    return pl.pallas_call(
        _outproj_body,
        grid=(t // tm,),
        in_specs=[pl.BlockSpec((tm, d), lambda i: (i, 0)), pl.BlockSpec((tm, d), lambda i: (i, 0)),
                  const(d, d), const(1, d), const(d, 2 * ROUTE_LANES), const(1, ROUTE_LANES)],
        out_specs=[pl.BlockSpec((tm, d), lambda i: (i, 0)),
                   pl.BlockSpec((tm, d), lambda i: (i, 0)),
                   pl.BlockSpec((tm, ROUTE_LANES), lambda i: (i, 0)),
                   pl.BlockSpec((8, tm), lambda i: (0, i))],
        out_shape=[jax.ShapeDtypeStruct((t, d), F32),
                   jax.ShapeDtypeStruct((t, d), F32),
                   jax.ShapeDtypeStruct((t, ROUTE_LANES), F32),
                   jax.ShapeDtypeStruct((8, t), F32)],
        compiler_params=_cparams(("arbitrary",)),
        name="outproj_router",
    )(x2, merged, w_out, g.reshape(1, d), wr_cat, br)


def _count_body(ids_ref, cnt_ref):
    nc, _, c = ids_ref.shape
    eidx = lax.broadcasted_iota(I32, (N_EXPERTS, c), 0)

    def step(i, acc):
        return acc + jnp.where(ids_ref[i] == eidx, 1.0, 0.0)

    acc = lax.fori_loop(0, nc, step, jnp.zeros((N_EXPERTS, c), F32))
    tot = jnp.sum(acc, axis=1, keepdims=True)
    cnt_ref[...] = jnp.broadcast_to(tot, cnt_ref.shape).astype(I32)


def _expert_counts(ids3):
    return pl.pallas_call(
        _count_body,
        out_shape=jax.ShapeDtypeStruct((N_EXPERTS, 128), I32),
        name="expert_counts",
    )(ids3)


def _pos_body(ids_ref, start_ref, incl_ref, pos_ref):
    nc, _, c = ids_ref.shape
    eidx = lax.broadcasted_iota(I32, (N_EXPERTS, c), 0)
    incl = incl_ref[...]

    def step(i, carry):
        onehot = ids_ref[i] == eidx
        cum = jnp.dot(jnp.where(onehot, 1.0, 0.0).astype(BF16), incl, preferred_element_type=F32)
        val = jnp.where(onehot, cum - 1.0 + carry, 0.0)
        pos_ref[i] = jnp.sum(val, axis=0, keepdims=True).astype(I32)
        return carry + cum[:, c - 1:c]

    lax.fori_loop(0, nc, step, start_ref[:, 0:1].astype(F32))


def _sorted_positions(ids3, starts):
    nc, _, c = ids3.shape
    r = jnp.arange(c)
    incl = (r[:, None] <= r[None, :]).astype(BF16)
    start_b = jnp.broadcast_to(starts.astype(I32)[:, None], (N_EXPERTS, 128))
    return pl.pallas_call(
        _pos_body,
        out_shape=jax.ShapeDtypeStruct((nc, 1, c), I32),
        name="sorted_positions",
    )(ids3, start_b, incl)


def _dispatch_body(pos_ref, src_ref, dst_ref, buf, lsem, ssem, *, n_tok, tm):
    i = pl.program_id(0)
    nsteps = pl.num_programs(0)
    base = i * tm

    def load(tile):
        b = tile % DISPATCH_BUFS
        return pltpu.make_async_copy(src_ref.at[pl.ds(tile * tm, tm), :], buf.at[b], lsem.at[b])

    def drain(tile):
        b = tile % DISPATCH_BUFS
        for s in range(2):
            pltpu.make_async_copy(buf.at[b], dst_ref.at[pl.ds(0, tm), :], ssem.at[2 * b + s]).wait()

    @pl.when(i == 0)
    def _():
        load(0).start()

        @pl.when(nsteps > 1)
        def _():
            load(1).start()

    load(i).wait()
    b = i % DISPATCH_BUFS

    def issue(ii, carry):
        for k in range(DMA_UNROLL):
            j = ii * DMA_UNROLL + k
            row = buf.at[b, pl.ds(j, 1), :]
            pltpu.make_async_copy(row, dst_ref.at[pl.ds(pos_ref[base + j], 1), :], ssem.at[2 * b]).start()
            pltpu.make_async_copy(row, dst_ref.at[pl.ds(pos_ref[n_tok + base + j], 1), :], ssem.at[2 * b + 1]).start()
        return carry

    lax.fori_loop(0, tm // DMA_UNROLL, issue, 0)

    @pl.when(i >= 1)
    def _():
        drain(i - 1)

    @pl.when(i + 2 < nsteps)
    def _():
        load(i + 2).start()

    @pl.when(i == nsteps - 1)
    def _():
        drain(i)


def _dispatch(pos_flat, hp, tm=MOE_TM):
    n = pos_flat.shape[0]
    t, d = hp.shape
    assert t % tm == 0 and tm % DMA_UNROLL == 0
    return pl.pallas_call(
        functools.partial(_dispatch_body, n_tok=t, tm=tm),
        grid_spec=pltpu.PrefetchScalarGridSpec(
            num_scalar_prefetch=1, grid=(t // tm,),
            in_specs=[pl.BlockSpec(memory_space=pl.ANY)],
            out_specs=pl.BlockSpec(memory_space=pl.ANY),
            scratch_shapes=[pltpu.VMEM((DISPATCH_BUFS, tm, d), hp.dtype),
                            pltpu.SemaphoreType.DMA((DISPATCH_BUFS,)),
                            pltpu.SemaphoreType.DMA((2 * DISPATCH_BUFS,))]),
        out_shape=jax.ShapeDtypeStruct((n, d), hp.dtype),
        compiler_params=_cparams(("arbitrary",)),
        name="dispatch",
    )(pos_flat, hp)


def _gmm_body(tile_ref, exp_ref, nxt_ref, lo_ref, hi_ref, x_ref, wg_ref, wu_ref, wd_ref, o_ref,
              sg, su, sd, wgs, wus, wds, sem):
    w = pl.program_id(0)
    prev = jnp.maximum(w - 1, 0)
    new_expert = (w == 0) | (exp_ref[w] != exp_ref[prev])
    first_of_tile = (w == 0) | (tile_ref[w] != tile_ref[prev])
    lo = lo_ref[w]
    hi = hi_ref[w]

    def stage(e):
        return (pltpu.make_async_copy(wg_ref.at[e], sg, sem.at[0]),
                pltpu.make_async_copy(wu_ref.at[e], su, sem.at[1]),
                pltpu.make_async_copy(wd_ref.at[e], sd, sem.at[2]))

    @pl.when(w == 0)
    def _():
        for cp in stage(exp_ref[0]):
            cp.start()

    @pl.when(new_expert)
    def _():
        for cp in stage(exp_ref[w]):
            cp.wait()
        wgs[...] = sg[...].astype(BF16)
        wus[...] = su[...].astype(BF16)
        wds[...] = sd[...].astype(BF16)

        @pl.when(nxt_ref[w] >= 0)
        def _():
            for cp in stage(nxt_ref[w]):
                cp.start()

    @pl.when(hi > lo)
    def _():
        xx = x_ref[...].astype(BF16)
        g = jnp.dot(xx, wgs[...], preferred_element_type=F32)
        u = jnp.dot(xx, wus[...], preferred_element_type=F32)
        hid = (g * jax.nn.sigmoid(g)) * u
        y = jnp.dot(hid.astype(BF16), wds[...], preferred_element_type=F32)
        rows = lax.broadcasted_iota(I32, (y.shape[0], 1), 0)
        y = jnp.where((rows >= lo) & (rows < hi), y, 0.0)

        @pl.when(first_of_tile)
        def _():
            o_ref[...] = y

        @pl.when(jnp.logical_not(first_of_tile))
        def _():
            o_ref[...] += y


def _gmm(meta, xs, w_gate, w_up, w_down, tm):
    tile_id, expert_id, next_expert, row_lo, row_hi = meta
    n, _ = xs.shape
    e, d, f = w_gate.shape
    nw = tile_id.shape[0]
    hbm = pl.BlockSpec(memory_space=pl.ANY)
    row_tile = lambda w, ti, *_: (ti[w], 0)
    return pl.pallas_call(
        _gmm_body,
        grid_spec=pltpu.PrefetchScalarGridSpec(
            num_scalar_prefetch=5, grid=(nw,),
            in_specs=[pl.BlockSpec((tm, d), row_tile), hbm, hbm, hbm],
            out_specs=pl.BlockSpec((tm, d), row_tile),
            scratch_shapes=[pltpu.VMEM((d, f), F32), pltpu.VMEM((d, f), F32), pltpu.VMEM((f, d), F32),
                            pltpu.VMEM((d, f), BF16), pltpu.VMEM((d, f), BF16), pltpu.VMEM((f, d), BF16),
                            pltpu.SemaphoreType.DMA((3,))]),
        out_shape=jax.ShapeDtypeStruct((n, d), F32),
        compiler_params=_cparams(("arbitrary",)),
        name="expert_gmm",
    )(tile_id, expert_id, next_expert, row_lo, row_hi, xs, w_gate, w_up, w_down)


def _gmm_metadata(counts, n_rows, tm):
    nt = n_rows // tm
    nw = nt + N_EXPERTS - 1
    ends = jnp.cumsum(counts)
    starts = ends - counts
    first_tile = starts // tm
    n_items = jnp.where(counts > 0, (ends - 1) // tm - first_tile + 1, 0)
    item_end = jnp.cumsum(n_items)
    item_start = item_end - n_items
    total = item_end[-1]
    w = jnp.arange(nw, dtype=I32)
    wc = jnp.minimum(w, total - 1)
    ex = jnp.sum((item_end[None, :] <= wc[:, None]).astype(I32), axis=1)
    tile = first_tile[ex] + (wc - item_start[ex])
    lo = jnp.maximum(starts[ex], tile * tm) - tile * tm
    hi = jnp.minimum(ends[ex], (tile + 1) * tm) - tile * tm
    valid = w < total
    lo = jnp.where(valid, lo, 0)
    hi = jnp.where(valid, hi, 0)
    run_end = item_end[ex]
    nxt = jnp.where(run_end < total, ex[jnp.minimum(run_end, nw - 1)], -1)
    return tile.astype(I32), ex, nxt.astype(I32), lo.astype(I32), hi.astype(I32)


def _combine_body(pos_ref, x1_ref, rr_ref, ys_ref, o_ref, a0, a1, b0, b1, sem, *, n_tok):
    i = pl.program_id(0)
    nsteps = pl.num_programs(0)
    tm = x1_ref.shape[0]

    def gather(tile, bufs, sems):
        base = tile * tm

        def issue(jj, carry):
            for k in range(DMA_UNROLL):
                j = jj * DMA_UNROLL + k
                pltpu.make_async_copy(ys_ref.at[pl.ds(pos_ref[base + j], 1), :],
                                      bufs[0].at[pl.ds(j, 1), :], sem.at[sems[0]]).start()
                pltpu.make_async_copy(ys_ref.at[pl.ds(pos_ref[n_tok + base + j], 1), :],
                                      bufs[1].at[pl.ds(j, 1), :], sem.at[sems[1]]).start()
            return carry

        lax.fori_loop(0, tm // DMA_UNROLL, issue, 0)

    def step(cur, cur_sems, nxt, nxt_sems):
        @pl.when(i == 0)
        def _():
            gather(i, cur, cur_sems)

        @pl.when(i + 1 < nsteps)
        def _():
            gather(i + 1, nxt, nxt_sems)

        for buf, s in zip(cur, cur_sems):
            pltpu.make_async_copy(ys_ref.at[pl.ds(0, tm), :], buf, sem.at[s]).wait()
        rr = rr_ref[...]
        o_ref[...] = x1_ref[...] + rr[:, 2:3] * cur[0][...] + rr[:, 3:4] * cur[1][...]

    @pl.when(i % 2 == 0)
    def _():
        step((a0, a1), (0, 1), (b0, b1), (2, 3))

    @pl.when(i % 2 == 1)
    def _():
        step((b0, b1), (2, 3), (a0, a1), (0, 1))


def _combine(pos_flat, x1, route_rows, ys, tm=MOE_TM):
    t, d = x1.shape
    assert tm % DMA_UNROLL == 0
    buf = pltpu.VMEM((tm, d), F32)
    return pl.pallas_call(
        functools.partial(_combine_body, n_tok=t),
        grid_spec=pltpu.PrefetchScalarGridSpec(
            num_scalar_prefetch=1, grid=(t // tm,),
            in_specs=[pl.BlockSpec((tm, d), lambda i, p: (i, 0)),
                      pl.BlockSpec((tm, ROUTE_LANES), lambda i, p: (i, 0)),
                      pl.BlockSpec(memory_space=pl.ANY)],
            out_specs=pl.BlockSpec((tm, d), lambda i, p: (i, 0)),
            scratch_shapes=[buf, buf, buf, buf, pltpu.SemaphoreType.DMA((4,))]),
        out_shape=jax.ShapeDtypeStruct((t, d), F32),
        compiler_params=_cparams(("arbitrary",)),
        name="combine",
    )(pos_flat, x1, route_rows, ys)


def _moe(x1, hp, route_rows, route_t, w_gate, w_up, w_down, gmm_tm=MOE_TM, sort_chunk=SORT_CHUNK):
    t = x1.shape[0]
    n = 2 * t
    ids3 = route_t[0:2].astype(I32).reshape(n // sort_chunk, 1, sort_chunk)
    counts = _expert_counts(ids3)[:, 0]
    starts = jnp.cumsum(counts) - counts
    pos_flat = _sorted_positions(ids3, starts).reshape(n)
    xs = _dispatch(pos_flat, hp)
    ys = _gmm(_gmm_metadata(counts, n, gmm_tm), xs, w_gate, w_up, w_down, gmm_tm)
    return _combine(pos_flat, x1, route_rows, ys)


def _layer(x, attn_norm_g, w_in, lambda_re, lambda_im, log_dt, ssm_b_re, ssm_b_im, ssm_c_re, ssm_c_im,
           ssm_d, w_glu, q_norm_g, k_norm_g, w_branch_ssm, w_branch_att, w_out, ffn_norm_g,
           router_group_w, router_group_b, router_expert_w, router_expert_b,
           expert_w_gate, expert_w_up, expert_w_down):
    nb, seq, d = x.shape
    t = nb * seq
    ds = w_glu.shape[0]
    da = N_HEADS * HEAD_DIM
    x2 = x.reshape(t, d)

    o = 0
    u = _proj(x2, attn_norm_g, w_in, o, ds, "plain", F32, name="proj_u"); o += ds
    qg = q_norm_g.astype(F32) * (HEAD_DIM ** -0.5 * LOG2_E)
    q = _proj(x2, attn_norm_g, w_in, o, da, "headnorm", BF16, gain=qg, name="proj_q"); o += da
    k = _proj(x2, attn_norm_g, w_in, o, da, "headnorm", BF16, gain=k_norm_g, name="proj_k"); o += da
    v = _proj(x2, attn_norm_g, w_in, o, da, "plain", BF16, name="proj_v"); o += da
    gates = _proj(x2, attn_norm_g, w_in, o, w_in.shape[1] - o, "sigmoid", BF16, name="proj_gates")

    a_re, a_im, wb_re, wb_im, wc = _ssm_params(lambda_re, lambda_im, log_dt, ssm_b_re, ssm_b_im, ssm_c_re, ssm_c_im)
    y_ssm = _ssm(u.reshape(nb, seq, ds), a_re, a_im, wb_re, wb_im, wc, ssm_d.astype(F32), w_glu.astype(BF16))
    y_att = _attention(q.reshape(nb, seq, da), k.reshape(nb, seq, da), v.reshape(nb, seq, da))

    merged = _merge(y_ssm.reshape(t, ds), y_att.reshape(t, da), gates, w_branch_ssm, w_branch_att)

    n_pad = ROUTE_LANES - N_EXPERTS - N_EXPERT_GROUPS
    wr = jnp.concatenate([router_expert_w.astype(F32), router_group_w.astype(F32), jnp.zeros((d, n_pad), F32)], axis=1)
    br = jnp.concatenate([router_expert_b.astype(F32), router_group_b.astype(F32), jnp.zeros((n_pad,), F32)])[None, :]
    x1, hp, route_rows, route_t = _outproj(x2, merged, w_out.astype(BF16), ffn_norm_g, wr, br)

    out = _moe(x1, hp, route_rows, route_t, expert_w_gate, expert_w_up, expert_w_down)
    return out.reshape(nb, seq, d)


def kernel(x, attn_norm_g, w_in, lambda_re, lambda_im, log_dt, ssm_b_re, ssm_b_im, ssm_c_re, ssm_c_im, ssm_d, w_glu, q_norm_g, k_norm_g, w_branch_ssm, w_branch_att, w_out, ffn_norm_g, router_group_w, router_group_b, router_expert_w, router_expert_b, expert_w_gate, expert_w_up, expert_w_down):
    depth = attn_norm_g.shape[0]
    for l in range(depth):
        x = _layer(x, attn_norm_g[l], w_in[l], lambda_re[l], lambda_im[l], log_dt[l], ssm_b_re[l], ssm_b_im[l],
                   ssm_c_re[l], ssm_c_im[l], ssm_d[l], w_glu[l], q_norm_g[l], k_norm_g[l], w_branch_ssm[l],
                   w_branch_att[l], w_out[l], ffn_norm_g[l], router_group_w[l], router_group_b[l],
                   router_expert_w[l], router_expert_b[l], expert_w_gate[l], expert_w_up[l], expert_w_down[l])
    return x
```

```python
import functools

import jax
import jax.numpy as jnp
from jax import lax
from jax.experimental import pallas as pl
from jax.experimental.pallas import tpu as pltpu

F32 = jnp.float32
BF16 = jnp.bfloat16
I32 = jnp.int32

EPS = 1e-6
N_HEADS = 8
HEAD_DIM = 128
N_EXPERT_GROUPS = 4
EXPERTS_PER_GROUP = 8
N_EXPERTS = N_EXPERT_GROUPS * EXPERTS_PER_GROUP

LANES = 128
V7X_VMEM_BYTES = 64 * 1024 * 1024
V7X_VMEM_RESERVED_BYTES = 8 * 1024 * 1024
VMEM_LIMIT = V7X_VMEM_BYTES - V7X_VMEM_RESERVED_BYTES

PROJ_TM, PROJ_TN = 1024, 1024
MERGE_TM, MERGE_TN = 1024, 1024
OUTPROJ_TM = 512
SSM_TC = 64
SSM_LANE_CHUNK = 1024
SSM_GROUPS_PER_BLOCK = 16
ATTN_TQ = 256
ATTN_HEADS_PER_STEP = 8
SORT_CHUNK = 512
MOE_TM = 256
DMA_UNROLL = 16
SCAN_UNROLL = 4
DISPATCH_BUFS = 3

ROUTE_LANES = LANES
NEG_BIG = -1e30
EXP_ZERO_BELOW = -104.0
LOG2_E = 1.4426950408889634
SOFTPLUS2_LINEAR_ABOVE = 30.0


def _cparams(sem):
    return pltpu.CompilerParams(dimension_semantics=sem, vmem_limit_bytes=VMEM_LIMIT)


def _proj_body(*refs, mode, first):
    if first:
        x_ref, gn_ref, w_ref, g_ref, o_ref, xb_ref, rs_ref, wb_ref = refs
    else:
        xb_ref, rs_ref, gn_ref, w_ref, g_ref, o_ref, wb_ref = refs

    @pl.when(pl.program_id(1) == 0)
    def _():
        wb_ref[...] = (w_ref[...] * gn_ref[...]).astype(BF16)

    if first:
        x = x_ref[...]
        xb = x.astype(BF16)
        scale = lax.rsqrt(jnp.mean(x * x, axis=-1, keepdims=True) + EPS)
        xb_ref[...] = xb
        rs_ref[...] = jnp.broadcast_to(scale, rs_ref.shape)
    else:
        xb = xb_ref[...]
        scale = rs_ref[:, 0:1]
    acc = jnp.dot(xb, wb_ref[...], preferred_element_type=F32) * scale
    if mode == "plain":
        o_ref[...] = acc.astype(o_ref.dtype)
    elif mode == "sigmoid":
        o_ref[...] = jax.nn.sigmoid(acc).astype(o_ref.dtype)
    else:
        n = acc.shape[1] // HEAD_DIM
        for hh in range(n):
            blk = acc[:, hh * HEAD_DIM:(hh + 1) * HEAD_DIM]
            ms = jnp.mean(blk * blk, axis=-1, keepdims=True)
            o_ref[:, hh * HEAD_DIM:(hh + 1) * HEAD_DIM] = (blk * lax.rsqrt(ms + EPS) * g_ref[...]).astype(o_ref.dtype)


def _proj(rows, norm_g, w, col0, n, mode, out_dtype, gain=None, tm=PROJ_TM, tn=PROJ_TN, name="proj"):
    first = not isinstance(rows, tuple)
    t, k = rows.shape if first else rows[0].shape
    assert col0 % tn == 0 and n % tn == 0 and (not first or n == tn)
    j0 = col0 // tn
    if gain is None:
        gain = jnp.ones((HEAD_DIM,), F32)
    row_tile = lambda width: pl.BlockSpec((tm, width), lambda j, i: (i, 0))
    param_specs = [pl.BlockSpec((k, 1), lambda j, i: (0, 0)),
                   pl.BlockSpec((k, tn), lambda j, i: (0, j0 + j)),
                   pl.BlockSpec((1, HEAD_DIM), lambda j, i: (0, 0))]
    params = (norm_g.reshape(k, 1).astype(F32), w, gain.reshape(1, HEAD_DIM).astype(F32))
    out_spec = pl.BlockSpec((tm, tn), lambda j, i: (i, j))
    out_shape = jax.ShapeDtypeStruct((t, n), out_dtype)
    if first:
        in_specs, args = [row_tile(k)] + param_specs, (rows,) + params
        out_specs = [out_spec, row_tile(k), row_tile(LANES)]
        out_shape = [out_shape, jax.ShapeDtypeStruct((t, k), BF16), jax.ShapeDtypeStruct((t, LANES), F32)]
    else:
        in_specs, args = [row_tile(k), row_tile(LANES)] + param_specs, tuple(rows) + params
        out_specs = out_spec
    return pl.pallas_call(
        functools.partial(_proj_body, mode=mode, first=first),
        grid=(n // tn, t // tm),
        in_specs=in_specs,
        out_specs=out_specs,
        out_shape=out_shape,
        scratch_shapes=[pltpu.VMEM((k, tn), BF16)],
        compiler_params=_cparams(("arbitrary", "arbitrary")),
        name=name,
    )(*args)


def _ssm_body(u_ref, are_ref, aim_ref, wbre_ref, wbim_ref, wc_ref, d_ref, wglu_ref,
              o_ref, tm_ref, bure_ref, buim_ref, sre_ref, sim_ref, *, tc, lane_chunk):
    nb = u_ref.shape[0]
    ds = u_ref.shape[2]
    rows = nb * tc
    nslab = ds // LANES
    nblk = wbre_ref.shape[0]
    cb = wbre_ref.shape[1]
    sb = wbre_ref.shape[2]
    n_state = nblk * sb

    @pl.when(pl.program_id(0) == 0)
    def _():
        sre_ref[...] = jnp.zeros_like(sre_ref)
        sim_ref[...] = jnp.zeros_like(sim_ref)

    for b in range(nb):
        for j in range(nslab):
            tm_ref[j, pl.ds(b, tc, stride=nb), :] = u_ref[b, :, j * LANES:(j + 1) * LANES]
    u_tm = jnp.concatenate([tm_ref[j] for j in range(nslab)], axis=1).astype(BF16)
    for gb in range(nblk):
        lhs = u_tm[:, gb * cb:(gb + 1) * cb]
        bure_ref[:, gb * sb:(gb + 1) * sb] = jnp.dot(lhs, wbre_ref[gb], preferred_element_type=F32)
        buim_ref[:, gb * sb:(gb + 1) * sb] = jnp.dot(lhs, wbim_ref[gb], preferred_element_type=F32)

    for lc in range(n_state // lane_chunk):
        sl = slice(lc * lane_chunk, (lc + 1) * lane_chunk)
        ar = jnp.broadcast_to(are_ref[:, sl], (nb, lane_chunk))
        ai = jnp.broadcast_to(aim_ref[:, sl], (nb, lane_chunk))

        def steps(tt, carry, sl=sl, ar=ar, ai=ai):
            xr, xi = carry
            for k in range(SCAN_UNROLL):
                r0 = pl.multiple_of((tt * SCAN_UNROLL + k) * nb, nb)
                br = bure_ref[pl.ds(r0, nb), sl]
                bi = buim_ref[pl.ds(r0, nb), sl]
                xr, xi = ar * xr - ai * xi + br, ar * xi + ai * xr + bi
                bure_ref[pl.ds(r0, nb), sl] = xr
                buim_ref[pl.ds(r0, nb), sl] = xi
            return xr, xi

        xr, xi = lax.fori_loop(0, tc // SCAN_UNROLL, steps, (sre_ref[:, sl], sim_ref[:, sl]))
        sre_ref[:, sl] = xr
        sim_ref[:, sl] = xi

    for gb in range(nblk):
        xs = jnp.concatenate([bure_ref[:, gb * sb:(gb + 1) * sb], buim_ref[:, gb * sb:(gb + 1) * sb]], axis=1)
        y_blk = jnp.dot(xs.astype(BF16), wc_ref[gb], preferred_element_type=F32)
        for jj in range(cb // LANES):
            tm_ref[gb * (cb // LANES) + jj] = y_blk[:, jj * LANES:(jj + 1) * LANES]
    y_bm = jnp.concatenate(
        [jnp.concatenate([tm_ref[j, pl.ds(b, tc, stride=nb), :] for j in range(nslab)], axis=1) for b in range(nb)],
        axis=0)
    y = y_bm + d_ref[...] * u_ref[...].reshape(rows, ds)
    z = jax.nn.gelu(y)
    gate = jax.nn.sigmoid(jnp.dot(z.astype(BF16), wglu_ref[...], preferred_element_type=F32))
    o_ref[...] = (z * gate).reshape(nb, tc, ds).astype(o_ref.dtype)


def _ssm(u, a_re, a_im, wb_re, wb_im, wc, d_skip, w_glu, tc=SSM_TC, lane_chunk=SSM_LANE_CHUNK):
    nb, seq, ds = u.shape
    rows = nb * tc
    n_state = a_re.shape[1]
    assert tc % SCAN_UNROLL == 0
    const = lambda *shape: pl.BlockSpec(shape, lambda c: (0,) * len(shape))
    return pl.pallas_call(
        functools.partial(_ssm_body, tc=tc, lane_chunk=lane_chunk),
        grid=(seq // tc,),
        in_specs=[pl.BlockSpec((nb, tc, ds), lambda c: (0, c, 0)),
                  const(1, n_state), const(1, n_state),
                  const(*wb_re.shape), const(*wb_im.shape), const(*wc.shape),
                  const(1, ds), const(ds, ds)],
        out_specs=pl.BlockSpec((nb, tc, ds), lambda c: (0, c, 0)),
        out_shape=jax.ShapeDtypeStruct((nb, seq, ds), BF16),
        scratch_shapes=[pltpu.VMEM((ds // LANES, rows, LANES), F32),
                        pltpu.VMEM((rows, n_state), F32), pltpu.VMEM((rows, n_state), F32),
                        pltpu.VMEM((nb, n_state), F32), pltpu.VMEM((nb, n_state), F32)],
        compiler_params=_cparams(("arbitrary",)),
        name="s5_mixer",
    )(u, a_re, a_im, wb_re, wb_im, wc, d_skip.reshape(1, ds), w_glu)


def _ssm_params(lambda_re, lambda_im, log_dt, b_re, b_im, c_re, c_im):
    g, p = lambda_re.shape
    h = b_re.shape[2]
    nblk = g // SSM_GROUPS_PER_BLOCK
    dt = jnp.exp(log_dt.astype(F32))[:, None]
    lr = lambda_re.astype(F32)
    li = lambda_im.astype(F32)
    mag = jnp.exp(lr * dt)
    abar_re, abar_im = mag * jnp.cos(li * dt), mag * jnp.sin(li * dt)
    nr, ni = abar_re - 1.0, abar_im
    den = lr * lr + li * li
    coef_re = (nr * lr + ni * li) / den
    coef_im = (ni * lr - nr * li) / den
    bbar_re = coef_re[..., None] * b_re - coef_im[..., None] * b_im
    bbar_im = coef_re[..., None] * b_im + coef_im[..., None] * b_re
    gpb = SSM_GROUPS_PER_BLOCK

    def block_diag(m, rows_per_group, cols_per_group):
        r = jnp.arange(gpb * rows_per_group)[:, None] // rows_per_group
        c = jnp.arange(gpb * cols_per_group)[None, :] // cols_per_group
        return jnp.where(r == c, jnp.tile(m, (1, gpb, 1)), 0.0)

    def pack_b(bb):
        m = bb.reshape(nblk, gpb, p, h).transpose(0, 3, 1, 2).reshape(nblk, h, gpb * p)
        return block_diag(m, h, p)

    def pack_c(cc):
        m = cc.reshape(nblk, gpb, h, p).transpose(0, 3, 1, 2).reshape(nblk, p, gpb * h)
        return block_diag(m, p, h)

    wc = jnp.concatenate([pack_c(c_re.astype(F32)), -pack_c(c_im.astype(F32))], axis=1)
    return (abar_re.reshape(1, g * p), abar_im.reshape(1, g * p),
            pack_b(bbar_re).astype(BF16), pack_b(bbar_im).astype(BF16), wc.astype(BF16))


def _attn_body(q_ref, k_ref, v_ref, tri_ref, o_ref, acc_ref, rs_ref, *, tq, nhs):
    qi = pl.program_id(2)
    tri = tri_ref[...]
    row = lax.broadcasted_iota(I32, (nhs * tq, tq), 0) % tq
    col = lax.broadcasted_iota(I32, (nhs * tq, tq), 1)
    causal = col < row

    def sweep(j, masked):
        k0 = pl.multiple_of(j * tq, tq)
        heads = [slice(h * HEAD_DIM, (h + 1) * HEAD_DIM) for h in range(nhs)]
        z = jnp.concatenate(
            [lax.dot_general(q_ref[0, :, hs], k_ref[0, pl.ds(k0, tq), hs], (((1,), (1,)), ((), ())),
                             preferred_element_type=F32) for hs in heads], axis=0)
        sp = jnp.where(z > SOFTPLUS2_LINEAR_ABOVE, z, jnp.log(1.0 + jnp.exp2(z)) * LOG2_E)
        spm = jnp.where(causal, sp, 0.0) if masked else sp
        spb = spm.astype(BF16)
        tail = jnp.dot(spb, tri, preferred_element_type=F32)
        rsum = rs_ref[...]
        w = jnp.exp2(z - (sp + tail + jnp.concatenate([rsum] * (tq // LANES), axis=1)))
        if masked:
            w = jnp.where(causal, w, 0.0)
        wb = w.astype(BF16)
        for h, hs in enumerate(heads):
            acc_ref[h] += jnp.dot(wb[h * tq:(h + 1) * tq], v_ref[0, pl.ds(k0, tq), hs], preferred_element_type=F32)
        total = tail[:, 0:1] + spb[:, 0:1].astype(F32)
        rsum = rsum + jnp.broadcast_to(total, rsum.shape)
        rs_ref[...] = rsum
        return (jnp.min(rsum) < -EXP_ZERO_BELOW * LOG2_E).astype(I32)

    acc_ref[...] = jnp.zeros_like(acc_ref)
    rs_ref[...] = jnp.zeros_like(rs_ref)
    live = sweep(qi, True)

    def cond(c):
        return (c[0] >= 0) & (c[1] > 0)

    def body(c):
        return c[0] - 1, sweep(c[0], False)

    lax.while_loop(cond, body, (qi - 1, live))
    for h in range(nhs):
        o_ref[0, :, h * HEAD_DIM:(h + 1) * HEAD_DIM] = acc_ref[h].astype(o_ref.dtype)


def _attention(q, k, v, tq=ATTN_TQ, nhs=ATTN_HEADS_PER_STEP):
    nb, seq, da = q.shape
    nh = da // HEAD_DIM
    wd = nhs * HEAD_DIM
    r = jnp.arange(tq)
    tri = (r[:, None] > r[None, :]).astype(BF16)
    return pl.pallas_call(
        functools.partial(_attn_body, tq=tq, nhs=nhs),
        grid=(nb, nh // nhs, seq // tq),
        in_specs=[pl.BlockSpec((1, tq, wd), lambda b, h, i: (b, i, h)),
                  pl.BlockSpec((1, seq, wd), lambda b, h, i: (b, 0, h)),
                  pl.BlockSpec((1, seq, wd), lambda b, h, i: (b, 0, h)),
                  pl.BlockSpec((tq, tq), lambda b, h, i: (0, 0))],
        out_specs=pl.BlockSpec((1, tq, wd), lambda b, h, i: (b, i, h)),
        out_shape=jax.ShapeDtypeStruct((nb, seq, da), BF16),
        scratch_shapes=[pltpu.VMEM((nhs, tq, HEAD_DIM), F32), pltpu.VMEM((nhs * tq, LANES), F32)],
        compiler_params=_cparams(("arbitrary", "arbitrary", "arbitrary")),
        name="stick_attention",
    )(q, k, v, tri)


def _merge_body(ys_ref, ya_ref, gs_ref, ga_ref, ws_ref, wa_ref, o_ref, wsb_ref, wab_ref):
    @pl.when(pl.program_id(1) == 0)
    def _():
        wsb_ref[...] = ws_ref[...].astype(BF16)
        wab_ref[...] = wa_ref[...].astype(BF16)

    ps = jnp.dot(ys_ref[...], wsb_ref[...], preferred_element_type=F32)
    pa = jnp.dot(ya_ref[...], wab_ref[...], preferred_element_type=F32)
    o_ref[...] = (gs_ref[...].astype(F32) * ps + ga_ref[...].astype(F32) * pa).astype(o_ref.dtype)


def _merge(y_ssm, y_att, gates, w_s, w_a, tm=MERGE_TM, tn=MERGE_TN):
    t, ks = y_ssm.shape
    d = w_s.shape[1]
    nj = d // tn
    return pl.pallas_call(
        _merge_body,
        grid=(nj, t // tm),
        in_specs=[pl.BlockSpec((tm, ks), lambda j, i: (i, 0)),
                  pl.BlockSpec((tm, y_att.shape[1]), lambda j, i: (i, 0)),
                  pl.BlockSpec((tm, tn), lambda j, i: (i, j)),
                  pl.BlockSpec((tm, tn), lambda j, i: (i, nj + j)),
                  pl.BlockSpec((ks, tn), lambda j, i: (0, j)),
                  pl.BlockSpec((w_a.shape[0], tn), lambda j, i: (0, j))],
        out_specs=pl.BlockSpec((tm, tn), lambda j, i: (i, j)),
        out_shape=jax.ShapeDtypeStruct((t, d), BF16),
        scratch_shapes=[pltpu.VMEM((ks, tn), BF16), pltpu.VMEM((w_a.shape[0], tn), BF16)],
        compiler_params=_cparams(("arbitrary", "arbitrary")),
        name="branch_merge",
    )(y_ssm, y_att, gates, gates, w_s, w_a)


def _lane_min_index(mask, lane):
    return jnp.min(jnp.where(mask, lane, float(ROUTE_LANES)), axis=1, keepdims=True)


def _outproj_body(x_ref, m_ref, w_ref, g_ref, wrc_ref, br_ref,
                  x1_ref, hp_ref, rr_ref, rt_ref):
    x1 = x_ref[...] + jnp.dot(m_ref[...], w_ref[...], preferred_element_type=F32)
    x1_ref[...] = x1
    ms = jnp.mean(x1 * x1, axis=-1, keepdims=True)
    h2 = x1 * lax.rsqrt(ms + EPS) * g_ref[...]
    hp_ref[...] = h2
    hb = h2.astype(BF16)

    hl = (h2 - hb.astype(F32)).astype(BF16)
    p = jnp.dot(hb, wrc_ref[...], preferred_element_type=F32)
    logits = (p[:, :ROUTE_LANES] + p[:, ROUTE_LANES:]
              + jnp.dot(hl, wrc_ref[:, :ROUTE_LANES], preferred_element_type=F32)) + br_ref[...]
    tm = logits.shape[0]
    lane = lax.broadcasted_iota(I32, (tm, ROUTE_LANES), 1).astype(F32)
    is_group = (lane >= N_EXPERTS) & (lane < N_EXPERTS + N_EXPERT_GROUPS)
    gl = jnp.where(is_group, logits, NEG_BIG)
    gmax = jnp.max(gl, axis=1, keepdims=True)
    gidx = _lane_min_index(gl == gmax, lane) - N_EXPERTS
    g_top = 1.0 / jnp.sum(jnp.exp(gl - gmax), axis=1, keepdims=True)
    lo = gidx * EXPERTS_PER_GROUP
    in_grp = (lane >= lo) & (lane < lo + EXPERTS_PER_GROUP)
    el = jnp.where(in_grp, logits, NEG_BIG)
    m1 = jnp.max(el, axis=1, keepdims=True)
    i1 = _lane_min_index(el == m1, lane)
    el2 = jnp.where(lane == i1, NEG_BIG, el)
    m2 = jnp.max(el2, axis=1, keepdims=True)
    i2 = _lane_min_index(el2 == m2, lane)
    dlt = jnp.exp(m2 - m1)
    w1 = 1.0 / (1.0 + dlt)
    w2 = dlt * w1
    route = jnp.where(lane == 0, i1,
                      jnp.where(lane == 1, i2,
                                jnp.where(lane == 2, g_top * w1,
                                          jnp.where(lane == 3, g_top * w2, 0.0))))
    rr_ref[...] = route
    rt_ref[...] = jnp.transpose(route)[0:8, :]


def _outproj(x2, merged, w_out, g, wr, br, tm=OUTPROJ_TM):
    t, d = x2.shape
    wr_hi = wr.astype(BF16)
    wr_cat = jnp.concatenate([wr_hi, (wr - wr_hi.astype(F32)).astype(BF16)], axis=1)
    const = lambda *shape: pl.BlockSpec(shape, lambda i: (0,) * len(shape))
    return pl.pallas_call(
        _outproj_body,
        grid=(t // tm,),
        in_specs=[pl.BlockSpec((tm, d), lambda i: (i, 0)), pl.BlockSpec((tm, d), lambda i: (i, 0)),
                  const(d, d), const(1, d), const(d, 2 * ROUTE_LANES), const(1, ROUTE_LANES)],
        out_specs=[pl.BlockSpec((tm, d), lambda i: (i, 0)),
                   pl.BlockSpec((tm, d), lambda i: (i, 0)),
                   pl.BlockSpec((tm, ROUTE_LANES), lambda i: (i, 0)),
                   pl.BlockSpec((8, tm), lambda i: (0, i))],
        out_shape=[jax.ShapeDtypeStruct((t, d), F32),
                   jax.ShapeDtypeStruct((t, d), F32),
                   jax.ShapeDtypeStruct((t, ROUTE_LANES), F32),
                   jax.ShapeDtypeStruct((8, t), F32)],
        compiler_params=_cparams(("arbitrary",)),
        name="outproj_router",
    )(x2, merged, w_out, g.reshape(1, d), wr_cat, br)


def _count_body(ids_ref, cnt_ref):
    nc, _, c = ids_ref.shape
    eidx = lax.broadcasted_iota(I32, (N_EXPERTS, c), 0)

    def step(i, acc):
        return acc + jnp.where(ids_ref[i] == eidx, 1.0, 0.0)

    acc = lax.fori_loop(0, nc, step, jnp.zeros((N_EXPERTS, c), F32))
    tot = jnp.sum(acc, axis=1, keepdims=True)
    cnt_ref[...] = jnp.broadcast_to(tot, cnt_ref.shape).astype(I32)


def _expert_counts(ids3):
    return pl.pallas_call(
        _count_body,
        out_shape=jax.ShapeDtypeStruct((N_EXPERTS, 128), I32),
        name="expert_counts",
    )(ids3)


def _pos_body(ids_ref, start_ref, incl_ref, pos_ref):
    nc, _, c = ids_ref.shape
    eidx = lax.broadcasted_iota(I32, (N_EXPERTS, c), 0)
    incl = incl_ref[...]

    def step(i, carry):
        onehot = ids_ref[i] == eidx
        cum = jnp.dot(jnp.where(onehot, 1.0, 0.0).astype(BF16), incl, preferred_element_type=F32)
        val = jnp.where(onehot, cum - 1.0 + carry, 0.0)
        pos_ref[i] = jnp.sum(val, axis=0, keepdims=True).astype(I32)
        return carry + cum[:, c - 1:c]

    lax.fori_loop(0, nc, step, start_ref[:, 0:1].astype(F32))


def _sorted_positions(ids3, starts):
    nc, _, c = ids3.shape
    r = jnp.arange(c)
    incl = (r[:, None] <= r[None, :]).astype(BF16)
    start_b = jnp.broadcast_to(starts.astype(I32)[:, None], (N_EXPERTS, 128))
    return pl.pallas_call(
        _pos_body,
        out_shape=jax.ShapeDtypeStruct((nc, 1, c), I32),
        name="sorted_positions",
    )(ids3, start_b, incl)


def _dispatch_body(pos_ref, src_ref, dst_ref, buf, lsem, ssem, *, n_tok, tm):
    i = pl.program_id(0)
    nsteps = pl.num_programs(0)
    base = i * tm

    def load(tile):
        b = tile % DISPATCH_BUFS
        return pltpu.make_async_copy(src_ref.at[pl.ds(tile * tm, tm), :], buf.at[b], lsem.at[b])

    def drain(tile):
        b = tile % DISPATCH_BUFS
        for s in range(2):
            pltpu.make_async_copy(buf.at[b], dst_ref.at[pl.ds(0, tm), :], ssem.at[2 * b + s]).wait()

    @pl.when(i == 0)
    def _():
        load(0).start()

        @pl.when(nsteps > 1)
        def _():
            load(1).start()

    load(i).wait()
    b = i % DISPATCH_BUFS

    def issue(ii, carry):
        for k in range(DMA_UNROLL):
            j = ii * DMA_UNROLL + k
            row = buf.at[b, pl.ds(j, 1), :]
            pltpu.make_async_copy(row, dst_ref.at[pl.ds(pos_ref[base + j], 1), :], ssem.at[2 * b]).start()
            pltpu.make_async_copy(row, dst_ref.at[pl.ds(pos_ref[n_tok + base + j], 1), :], ssem.at[2 * b + 1]).start()
        return carry

    lax.fori_loop(0, tm // DMA_UNROLL, issue, 0)

    @pl.when(i >= 1)
    def _():
        drain(i - 1)

    @pl.when(i + 2 < nsteps)
    def _():
        load(i + 2).start()

    @pl.when(i == nsteps - 1)
    def _():
        drain(i)


def _dispatch(pos_flat, hp, tm=MOE_TM):
    n = pos_flat.shape[0]
    t, d = hp.shape
    assert t % tm == 0 and tm % DMA_UNROLL == 0
    return pl.pallas_call(
        functools.partial(_dispatch_body, n_tok=t, tm=tm),
        grid_spec=pltpu.PrefetchScalarGridSpec(
            num_scalar_prefetch=1, grid=(t // tm,),
            in_specs=[pl.BlockSpec(memory_space=pl.ANY)],
            out_specs=pl.BlockSpec(memory_space=pl.ANY),
            scratch_shapes=[pltpu.VMEM((DISPATCH_BUFS, tm, d), hp.dtype),
                            pltpu.SemaphoreType.DMA((DISPATCH_BUFS,)),
                            pltpu.SemaphoreType.DMA((2 * DISPATCH_BUFS,))]),
        out_shape=jax.ShapeDtypeStruct((n, d), hp.dtype),
        compiler_params=_cparams(("arbitrary",)),
        name="dispatch",
    )(pos_flat, hp)


def _gmm_body(tile_ref, exp_ref, nxt_ref, lo_ref, hi_ref, x_ref, wg_ref, wu_ref, wd_ref, o_ref,
              sg, su, sd, wgs, wus, wds, sem):
    w = pl.program_id(0)
    prev = jnp.maximum(w - 1, 0)
    new_expert = (w == 0) | (exp_ref[w] != exp_ref[prev])
    first_of_tile = (w == 0) | (tile_ref[w] != tile_ref[prev])
    lo = lo_ref[w]
    hi = hi_ref[w]

    def stage(e):
        return (pltpu.make_async_copy(wg_ref.at[e], sg, sem.at[0]),
                pltpu.make_async_copy(wu_ref.at[e], su, sem.at[1]),
                pltpu.make_async_copy(wd_ref.at[e], sd, sem.at[2]))

    @pl.when(w == 0)
    def _():
        for cp in stage(exp_ref[0]):
            cp.start()

    @pl.when(new_expert)
    def _():
        for cp in stage(exp_ref[w]):
            cp.wait()
        wgs[...] = sg[...].astype(BF16)
        wus[...] = su[...].astype(BF16)
        wds[...] = sd[...].astype(BF16)

        @pl.when(nxt_ref[w] >= 0)
        def _():
            for cp in stage(nxt_ref[w]):
                cp.start()

    @pl.when(hi > lo)
    def _():
        xx = x_ref[...].astype(BF16)
        g = jnp.dot(xx, wgs[...], preferred_element_type=F32)
        u = jnp.dot(xx, wus[...], preferred_element_type=F32)
        hid = (g * jax.nn.sigmoid(g)) * u
        y = jnp.dot(hid.astype(BF16), wds[...], preferred_element_type=F32)
        rows = lax.broadcasted_iota(I32, (y.shape[0], 1), 0)
        y = jnp.where((rows >= lo) & (rows < hi), y, 0.0)

        @pl.when(first_of_tile)
        def _():
            o_ref[...] = y

        @pl.when(jnp.logical_not(first_of_tile))
        def _():
            o_ref[...] += y


def _gmm(meta, xs, w_gate, w_up, w_down, tm):
    tile_id, expert_id, next_expert, row_lo, row_hi = meta
    n, _ = xs.shape
    e, d, f = w_gate.shape
    nw = tile_id.shape[0]
    hbm = pl.BlockSpec(memory_space=pl.ANY)
    row_tile = lambda w, ti, *_: (ti[w], 0)
    return pl.pallas_call(
        _gmm_body,
        grid_spec=pltpu.PrefetchScalarGridSpec(
            num_scalar_prefetch=5, grid=(nw,),
            in_specs=[pl.BlockSpec((tm, d), row_tile), hbm, hbm, hbm],
            out_specs=pl.BlockSpec((tm, d), row_tile),
            scratch_shapes=[pltpu.VMEM((d, f), F32), pltpu.VMEM((d, f), F32), pltpu.VMEM((f, d), F32),
                            pltpu.VMEM((d, f), BF16), pltpu.VMEM((d, f), BF16), pltpu.VMEM((f, d), BF16),
                            pltpu.SemaphoreType.DMA((3,))]),
        out_shape=jax.ShapeDtypeStruct((n, d), F32),
        compiler_params=_cparams(("arbitrary",)),
        name="expert_gmm",
    )(tile_id, expert_id, next_expert, row_lo, row_hi, xs, w_gate, w_up, w_down)


def _gmm_metadata(counts, n_rows, tm):
    nt = n_rows // tm
    nw = nt + N_EXPERTS - 1
    ends = jnp.cumsum(counts)
    starts = ends - counts
    first_tile = starts // tm
    n_items = jnp.where(counts > 0, (ends - 1) // tm - first_tile + 1, 0)
    item_end = jnp.cumsum(n_items)
    item_start = item_end - n_items
    total = item_end[-1]
    w = jnp.arange(nw, dtype=I32)
    wc = jnp.minimum(w, total - 1)
    ex = jnp.sum((item_end[None, :] <= wc[:, None]).astype(I32), axis=1)
    tile = first_tile[ex] + (wc - item_start[ex])
    lo = jnp.maximum(starts[ex], tile * tm) - tile * tm
    hi = jnp.minimum(ends[ex], (tile + 1) * tm) - tile * tm
    valid = w < total
    lo = jnp.where(valid, lo, 0)
    hi = jnp.where(valid, hi, 0)
    run_end = item_end[ex]
    nxt = jnp.where(run_end < total, ex[jnp.minimum(run_end, nw - 1)], -1)
    return tile.astype(I32), ex, nxt.astype(I32), lo.astype(I32), hi.astype(I32)


def _combine_body(pos_ref, x1_ref, rr_ref, ys_ref, o_ref, a0, a1, b0, b1, sem, *, n_tok):
    i = pl.program_id(0)
    nsteps = pl.num_programs(0)
    tm = x1_ref.shape[0]

    def gather(tile, bufs, sems):
        base = tile * tm

        def issue(jj, carry):
            for k in range(DMA_UNROLL):
                j = jj * DMA_UNROLL + k
                pltpu.make_async_copy(ys_ref.at[pl.ds(pos_ref[base + j], 1), :],
                                      bufs[0].at[pl.ds(j, 1), :], sem.at[sems[0]]).start()
                pltpu.make_async_copy(ys_ref.at[pl.ds(pos_ref[n_tok + base + j], 1), :],
                                      bufs[1].at[pl.ds(j, 1), :], sem.at[sems[1]]).start()
            return carry

        lax.fori_loop(0, tm // DMA_UNROLL, issue, 0)

    def step(cur, cur_sems, nxt, nxt_sems):
        @pl.when(i == 0)
        def _():
            gather(i, cur, cur_sems)

        @pl.when(i + 1 < nsteps)
        def _():
            gather(i + 1, nxt, nxt_sems)

        for buf, s in zip(cur, cur_sems):
            pltpu.make_async_copy(ys_ref.at[pl.ds(0, tm), :], buf, sem.at[s]).wait()
        rr = rr_ref[...]
        o_ref[...] = x1_ref[...] + rr[:, 2:3] * cur[0][...] + rr[:, 3:4] * cur[1][...]

    @pl.when(i % 2 == 0)
    def _():
        step((a0, a1), (0, 1), (b0, b1), (2, 3))

    @pl.when(i % 2 == 1)
    def _():
        step((b0, b1), (2, 3), (a0, a1), (0, 1))


def _combine(pos_flat, x1, route_rows, ys, tm=MOE_TM):
    t, d = x1.shape
    assert tm % DMA_UNROLL == 0
    buf = pltpu.VMEM((tm, d), F32)
    return pl.pallas_call(
        functools.partial(_combine_body, n_tok=t),
        grid_spec=pltpu.PrefetchScalarGridSpec(
            num_scalar_prefetch=1, grid=(t // tm,),
            in_specs=[pl.BlockSpec((tm, d), lambda i, p: (i, 0)),
                      pl.BlockSpec((tm, ROUTE_LANES), lambda i, p: (i, 0)),
                      pl.BlockSpec(memory_space=pl.ANY)],
            out_specs=pl.BlockSpec((tm, d), lambda i, p: (i, 0)),
            scratch_shapes=[buf, buf, buf, buf, pltpu.SemaphoreType.DMA((4,))]),
        out_shape=jax.ShapeDtypeStruct((t, d), F32),
        compiler_params=_cparams(("arbitrary",)),
        name="combine",
    )(pos_flat, x1, route_rows, ys)


def _moe(x1, hp, route_rows, route_t, w_gate, w_up, w_down, gmm_tm=MOE_TM, sort_chunk=SORT_CHUNK):
    t = x1.shape[0]
    n = 2 * t
    ids3 = route_t[0:2].astype(I32).reshape(n // sort_chunk, 1, sort_chunk)
    counts = _expert_counts(ids3)[:, 0]
    starts = jnp.cumsum(counts) - counts
    pos_flat = _sorted_positions(ids3, starts).reshape(n)
    xs = _dispatch(pos_flat, hp)
    ys = _gmm(_gmm_metadata(counts, n, gmm_tm), xs, w_gate, w_up, w_down, gmm_tm)
    return _combine(pos_flat, x1, route_rows, ys)


def _layer(x, attn_norm_g, w_in, lambda_re, lambda_im, log_dt, ssm_b_re, ssm_b_im, ssm_c_re, ssm_c_im,
           ssm_d, w_glu, q_norm_g, k_norm_g, w_branch_ssm, w_branch_att, w_out, ffn_norm_g,
           router_group_w, router_group_b, router_expert_w, router_expert_b,
           expert_w_gate, expert_w_up, expert_w_down):
    nb, seq, d = x.shape
    t = nb * seq
    ds = w_glu.shape[0]
    da = N_HEADS * HEAD_DIM
    x2 = x.reshape(t, d)

    o = 0
    u, xb, x_scale = _proj(x2, attn_norm_g, w_in, o, ds, "plain", F32, name="proj_u"); o += ds
    rows = (xb, x_scale)
    qg = q_norm_g.astype(F32) * (HEAD_DIM ** -0.5 * LOG2_E)
    q = _proj(rows, attn_norm_g, w_in, o, da, "headnorm", BF16, gain=qg, name="proj_q"); o += da
    k = _proj(rows, attn_norm_g, w_in, o, da, "headnorm", BF16, gain=k_norm_g, name="proj_k"); o += da
    v = _proj(rows, attn_norm_g, w_in, o, da, "plain", BF16, name="proj_v"); o += da
    gates = _proj(rows, attn_norm_g, w_in, o, w_in.shape[1] - o, "sigmoid", BF16, name="proj_gates")

    a_re, a_im, wb_re, wb_im, wc = _ssm_params(lambda_re, lambda_im, log_dt, ssm_b_re, ssm_b_im, ssm_c_re, ssm_c_im)
    y_ssm = _ssm(u.reshape(nb, seq, ds), a_re, a_im, wb_re, wb_im, wc, ssm_d.astype(F32), w_glu.astype(BF16))
    y_att = _attention(q.reshape(nb, seq, da), k.reshape(nb, seq, da), v.reshape(nb, seq, da))

    merged = _merge(y_ssm.reshape(t, ds), y_att.reshape(t, da), gates, w_branch_ssm, w_branch_att)

    n_pad = ROUTE_LANES - N_EXPERTS - N_EXPERT_GROUPS
    wr = jnp.concatenate([router_expert_w.astype(F32), router_group_w.astype(F32), jnp.zeros((d, n_pad), F32)], axis=1)
    br = jnp.concatenate([router_expert_b.astype(F32), router_group_b.astype(F32), jnp.zeros((n_pad,), F32)])[None, :]
    x1, hp, route_rows, route_t = _outproj(x2, merged, w_out.astype(BF16), ffn_norm_g, wr, br)

    out = _moe(x1, hp, route_rows, route_t, expert_w_gate, expert_w_up, expert_w_down)
    return out.reshape(nb, seq, d)


def kernel(x, attn_norm_g, w_in, lambda_re, lambda_im, log_dt, ssm_b_re, ssm_b_im, ssm_c_re, ssm_c_im, ssm_d, w_glu, q_norm_g, k_norm_g, w_branch_ssm, w_branch_att, w_out, ffn_norm_g, router_group_w, router_group_b, router_expert_w, router_expert_b, expert_w_gate, expert_w_up, expert_w_down):
    depth = attn_norm_g.shape[0]
    for l in range(depth):
        x = _layer(x, attn_norm_g[l], w_in[l], lambda_re[l], lambda_im[l], log_dt[l], ssm_b_re[l], ssm_b_im[l],
                   ssm_c_re[l], ssm_c_im[l], ssm_d[l], w_glu[l], q_norm_g[l], k_norm_g[l], w_branch_ssm[l],
                   w_branch_att[l], w_out[l], ffn_norm_g[l], router_group_w[l], router_group_b[l],
                   router_expert_w[l], router_expert_b[l], expert_w_gate[l], expert_w_up[l], expert_w_down[l])
    return x
```

```python
import functools

import jax
import jax.numpy as jnp
from jax import lax
from jax.experimental import pallas as pl
from jax.experimental.pallas import tpu as pltpu

F32 = jnp.float32
BF16 = jnp.bfloat16
I32 = jnp.int32

EPS = 1e-6
N_HEADS = 8
HEAD_DIM = 128
N_EXPERT_GROUPS = 4
EXPERTS_PER_GROUP = 8
N_EXPERTS = N_EXPERT_GROUPS * EXPERTS_PER_GROUP

LANES = 128
V7X_VMEM_BYTES = 64 * 1024 * 1024
V7X_VMEM_RESERVED_BYTES = 8 * 1024 * 1024
VMEM_LIMIT = V7X_VMEM_BYTES - V7X_VMEM_RESERVED_BYTES

PROJ_TM, PROJ_TN = 1024, 1024
MERGE_TM, MERGE_TN = 1024, 1024
OUTPROJ_TM = 512
SSM_TC = 64
SSM_LANE_CHUNK = 1024
SSM_GROUPS_PER_BLOCK = 16
ATTN_TQ = 256
ATTN_HEADS_PER_STEP = 8
SORT_CHUNK = 512
MOE_TM = 256
DMA_UNROLL = 16
SCAN_UNROLL = 4
DISPATCH_BUFS = 3

ROUTE_LANES = LANES
NEG_BIG = -1e30
EXP_ZERO_BELOW = -104.0
LOG2_E = 1.4426950408889634
SOFTPLUS2_LINEAR_ABOVE = 30.0


def _cparams(sem):
    return pltpu.CompilerParams(dimension_semantics=sem, vmem_limit_bytes=VMEM_LIMIT)


def _proj_body(x_ref, gn_ref, w_ref, g_ref, o_ref, wb_ref, *, mode):
    @pl.when(pl.program_id(1) == 0)
    def _():
        wb_ref[...] = (w_ref[...] * gn_ref[...]).astype(BF16)

    x = x_ref[...]
    acc = jnp.dot(x.astype(BF16), wb_ref[...], preferred_element_type=F32)
    acc = acc * lax.rsqrt(jnp.mean(x * x, axis=-1, keepdims=True) + EPS)
    if mode == "plain":
        o_ref[...] = acc.astype(o_ref.dtype)
    elif mode == "sigmoid":
        o_ref[...] = jax.nn.sigmoid(acc).astype(o_ref.dtype)
    else:
        n = acc.shape[1] // HEAD_DIM
        for hh in range(n):
            blk = acc[:, hh * HEAD_DIM:(hh + 1) * HEAD_DIM]
            ms = jnp.mean(blk * blk, axis=-1, keepdims=True)
            o_ref[:, hh * HEAD_DIM:(hh + 1) * HEAD_DIM] = (blk * lax.rsqrt(ms + EPS) * g_ref[...]).astype(o_ref.dtype)


def _proj(x2, norm_g, w, col0, n, mode, out_dtype, gain=None, tm=PROJ_TM, tn=PROJ_TN, name="proj"):
    t, k = x2.shape
    assert col0 % tn == 0 and n % tn == 0
    j0 = col0 // tn
    if gain is None:
        gain = jnp.ones((HEAD_DIM,), F32)
    return pl.pallas_call(
        functools.partial(_proj_body, mode=mode),
        grid=(n // tn, t // tm),
        in_specs=[pl.BlockSpec((tm, k), lambda j, i: (i, 0)),
                  pl.BlockSpec((k, 1), lambda j, i: (0, 0)),
                  pl.BlockSpec((k, tn), lambda j, i: (0, j0 + j)),
                  pl.BlockSpec((1, HEAD_DIM), lambda j, i: (0, 0))],
        out_specs=pl.BlockSpec((tm, tn), lambda j, i: (i, j)),
        out_shape=jax.ShapeDtypeStruct((t, n), out_dtype),
        scratch_shapes=[pltpu.VMEM((k, tn), BF16)],
        compiler_params=_cparams(("arbitrary", "arbitrary")),
        name=name,
    )(x2, norm_g.reshape(k, 1).astype(F32), w, gain.reshape(1, HEAD_DIM).astype(F32))


def _ssm_body(u_ref, are_ref, aim_ref, wbre_ref, wbim_ref, wc_ref, d_ref, wglu_ref,
              o_ref, tm_ref, bure_ref, buim_ref, sre_ref, sim_ref, *, tc, lane_chunk):
    nb = u_ref.shape[0]
    ds = u_ref.shape[2]
    rows = nb * tc
    nslab = ds // LANES
    nblk = wbre_ref.shape[0]
    cb = wbre_ref.shape[1]
    sb = wbre_ref.shape[2]
    n_state = nblk * sb

    @pl.when(pl.program_id(0) == 0)
    def _():
        sre_ref[...] = jnp.zeros_like(sre_ref)
        sim_ref[...] = jnp.zeros_like(sim_ref)

    for b in range(nb):
        for j in range(nslab):
            tm_ref[j, pl.ds(b, tc, stride=nb), :] = u_ref[b, :, j * LANES:(j + 1) * LANES]
    u_tm = jnp.concatenate([tm_ref[j] for j in range(nslab)], axis=1).astype(BF16)
    for gb in range(nblk):
        lhs = u_tm[:, gb * cb:(gb + 1) * cb]
        bure_ref[:, gb * sb:(gb + 1) * sb] = jnp.dot(lhs, wbre_ref[gb], preferred_element_type=F32)
        buim_ref[:, gb * sb:(gb + 1) * sb] = jnp.dot(lhs, wbim_ref[gb], preferred_element_type=F32)

    for lc in range(n_state // lane_chunk):
        sl = slice(lc * lane_chunk, (lc + 1) * lane_chunk)
        ar = jnp.broadcast_to(are_ref[:, sl], (nb, lane_chunk))
        ai = jnp.broadcast_to(aim_ref[:, sl], (nb, lane_chunk))

        def steps(tt, carry, sl=sl, ar=ar, ai=ai):
            xr, xi = carry
            for k in range(SCAN_UNROLL):
                r0 = pl.multiple_of((tt * SCAN_UNROLL + k) * nb, nb)
                br = bure_ref[pl.ds(r0, nb), sl]
                bi = buim_ref[pl.ds(r0, nb), sl]
                xr, xi = ar * xr - ai * xi + br, ar * xi + ai * xr + bi
                bure_ref[pl.ds(r0, nb), sl] = xr
                buim_ref[pl.ds(r0, nb), sl] = xi
            return xr, xi

        xr, xi = lax.fori_loop(0, tc // SCAN_UNROLL, steps, (sre_ref[:, sl], sim_ref[:, sl]))
        sre_ref[:, sl] = xr
        sim_ref[:, sl] = xi

    for gb in range(nblk):
        xs = jnp.concatenate([bure_ref[:, gb * sb:(gb + 1) * sb], buim_ref[:, gb * sb:(gb + 1) * sb]], axis=1)
        y_blk = jnp.dot(xs.astype(BF16), wc_ref[gb], preferred_element_type=F32)
        for jj in range(cb // LANES):
            tm_ref[gb * (cb // LANES) + jj] = y_blk[:, jj * LANES:(jj + 1) * LANES]
    y_bm = jnp.concatenate(
        [jnp.concatenate([tm_ref[j, pl.ds(b, tc, stride=nb), :] for j in range(nslab)], axis=1) for b in range(nb)],
        axis=0)
    y = y_bm + d_ref[...] * u_ref[...].reshape(rows, ds)
    z = jax.nn.gelu(y)
    gate = jax.nn.sigmoid(jnp.dot(z.astype(BF16), wglu_ref[...], preferred_element_type=F32))
    o_ref[...] = (z * gate).reshape(nb, tc, ds).astype(o_ref.dtype)


def _ssm(u, a_re, a_im, wb_re, wb_im, wc, d_skip, w_glu, tc=SSM_TC, lane_chunk=SSM_LANE_CHUNK):
    nb, seq, ds = u.shape
    rows = nb * tc
    n_state = a_re.shape[1]
    assert tc % SCAN_UNROLL == 0
    const = lambda *shape: pl.BlockSpec(shape, lambda c: (0,) * len(shape))
    return pl.pallas_call(
        functools.partial(_ssm_body, tc=tc, lane_chunk=lane_chunk),
        grid=(seq // tc,),
        in_specs=[pl.BlockSpec((nb, tc, ds), lambda c: (0, c, 0)),
                  const(1, n_state), const(1, n_state),
                  const(*wb_re.shape), const(*wb_im.shape), const(*wc.shape),
                  const(1, ds), const(ds, ds)],
        out_specs=pl.BlockSpec((nb, tc, ds), lambda c: (0, c, 0)),
        out_shape=jax.ShapeDtypeStruct((nb, seq, ds), BF16),
        scratch_shapes=[pltpu.VMEM((ds // LANES, rows, LANES), F32),
                        pltpu.VMEM((rows, n_state), F32), pltpu.VMEM((rows, n_state), F32),
                        pltpu.VMEM((nb, n_state), F32), pltpu.VMEM((nb, n_state), F32)],
        compiler_params=_cparams(("arbitrary",)),
        name="s5_mixer",
    )(u, a_re, a_im, wb_re, wb_im, wc, d_skip.reshape(1, ds), w_glu)


def _ssm_params(lambda_re, lambda_im, log_dt, b_re, b_im, c_re, c_im):
    g, p = lambda_re.shape
    h = b_re.shape[2]
    nblk = g // SSM_GROUPS_PER_BLOCK
    dt = jnp.exp(log_dt.astype(F32))[:, None]
    lr = lambda_re.astype(F32)
    li = lambda_im.astype(F32)
    mag = jnp.exp(lr * dt)
    abar_re, abar_im = mag * jnp.cos(li * dt), mag * jnp.sin(li * dt)
    nr, ni = abar_re - 1.0, abar_im
    den = lr * lr + li * li
    coef_re = (nr * lr + ni * li) / den
    coef_im = (ni * lr - nr * li) / den
    bbar_re = coef_re[..., None] * b_re - coef_im[..., None] * b_im
    bbar_im = coef_re[..., None] * b_im + coef_im[..., None] * b_re
    gpb = SSM_GROUPS_PER_BLOCK

    def block_diag(m, rows_per_group, cols_per_group):
        r = jnp.arange(gpb * rows_per_group)[:, None] // rows_per_group
        c = jnp.arange(gpb * cols_per_group)[None, :] // cols_per_group
        return jnp.where(r == c, jnp.tile(m, (1, gpb, 1)), 0.0)

    def pack_b(bb):
        m = bb.reshape(nblk, gpb, p, h).transpose(0, 3, 1, 2).reshape(nblk, h, gpb * p)
        return block_diag(m, h, p)

    def pack_c(cc):
        m = cc.reshape(nblk, gpb, h, p).transpose(0, 3, 1, 2).reshape(nblk, p, gpb * h)
        return block_diag(m, p, h)

    wc = jnp.concatenate([pack_c(c_re.astype(F32)), -pack_c(c_im.astype(F32))], axis=1)
    return (abar_re.reshape(1, g * p), abar_im.reshape(1, g * p),
            pack_b(bbar_re).astype(BF16), pack_b(bbar_im).astype(BF16), wc.astype(BF16))


def _attn_body(q_ref, k_ref, v_ref, tri_ref, o_ref, acc_ref, rs_ref, *, tq, nhs):
    qi = pl.program_id(2)
    tri = tri_ref[...]
    row = lax.broadcasted_iota(I32, (nhs * tq, tq), 0) % tq
    col = lax.broadcasted_iota(I32, (nhs * tq, tq), 1)
    causal = col < row

    def sweep(j, masked):
        k0 = pl.multiple_of(j * tq, tq)
        heads = [slice(h * HEAD_DIM, (h + 1) * HEAD_DIM) for h in range(nhs)]
        z = jnp.concatenate(
            [lax.dot_general(q_ref[0, :, hs], k_ref[0, pl.ds(k0, tq), hs], (((1,), (1,)), ((), ())),
                             preferred_element_type=F32) for hs in heads], axis=0)
        sp = jnp.where(z > SOFTPLUS2_LINEAR_ABOVE, z, jnp.log(1.0 + jnp.exp2(z)) * LOG2_E)
        spm = jnp.where(causal, sp, 0.0) if masked else sp
        spb = spm.astype(BF16)
        tail = jnp.dot(spb, tri, preferred_element_type=F32)
        rsum = rs_ref[...]
        w = jnp.exp2(z - (sp + tail + jnp.concatenate([rsum] * (tq // LANES), axis=1)))
        if masked:
            w = jnp.where(causal, w, 0.0)
        wb = w.astype(BF16)
        for h, hs in enumerate(heads):
            acc_ref[h] += jnp.dot(wb[h * tq:(h + 1) * tq], v_ref[0, pl.ds(k0, tq), hs], preferred_element_type=F32)
        total = tail[:, 0:1] + spb[:, 0:1].astype(F32)
        rsum = rsum + jnp.broadcast_to(total, rsum.shape)
        rs_ref[...] = rsum
        return (jnp.min(rsum) < -EXP_ZERO_BELOW * LOG2_E).astype(I32)

    acc_ref[...] = jnp.zeros_like(acc_ref)
    rs_ref[...] = jnp.zeros_like(rs_ref)
    live = sweep(qi, True)

    def cond(c):
        return (c[0] >= 0) & (c[1] > 0)

    def body(c):
        return c[0] - 1, sweep(c[0], False)

    lax.while_loop(cond, body, (qi - 1, live))
    for h in range(nhs):
        o_ref[0, :, h * HEAD_DIM:(h + 1) * HEAD_DIM] = acc_ref[h].astype(o_ref.dtype)


def _attention(q, k, v, tq=ATTN_TQ, nhs=ATTN_HEADS_PER_STEP):
    nb, seq, da = q.shape
    nh = da // HEAD_DIM
    wd = nhs * HEAD_DIM
    r = jnp.arange(tq)
    tri = (r[:, None] > r[None, :]).astype(BF16)
    return pl.pallas_call(
        functools.partial(_attn_body, tq=tq, nhs=nhs),
        grid=(nb, nh // nhs, seq // tq),
        in_specs=[pl.BlockSpec((1, tq, wd), lambda b, h, i: (b, i, h)),
                  pl.BlockSpec((1, seq, wd), lambda b, h, i: (b, 0, h)),
                  pl.BlockSpec((1, seq, wd), lambda b, h, i: (b, 0, h)),
                  pl.BlockSpec((tq, tq), lambda b, h, i: (0, 0))],
        out_specs=pl.BlockSpec((1, tq, wd), lambda b, h, i: (b, i, h)),
        out_shape=jax.ShapeDtypeStruct((nb, seq, da), BF16),
        scratch_shapes=[pltpu.VMEM((nhs, tq, HEAD_DIM), F32), pltpu.VMEM((nhs * tq, LANES), F32)],
        compiler_params=_cparams(("arbitrary", "arbitrary", "arbitrary")),
        name="stick_attention",
    )(q, k, v, tri)


def _merge_body(ys_ref, ya_ref, gs_ref, ga_ref, ws_ref, wa_ref, o_ref, wsb_ref, wab_ref):
    @pl.when(pl.program_id(1) == 0)
    def _():
        wsb_ref[...] = ws_ref[...].astype(BF16)
        wab_ref[...] = wa_ref[...].astype(BF16)

    ps = jnp.dot(ys_ref[...], wsb_ref[...], preferred_element_type=F32)
    pa = jnp.dot(ya_ref[...], wab_ref[...], preferred_element_type=F32)
    o_ref[...] = (gs_ref[...].astype(F32) * ps + ga_ref[...].astype(F32) * pa).astype(o_ref.dtype)


def _merge(y_ssm, y_att, gates, w_s, w_a, tm=MERGE_TM, tn=MERGE_TN):
    t, ks = y_ssm.shape
    d = w_s.shape[1]
    nj = d // tn
    return pl.pallas_call(
        _merge_body,
        grid=(nj, t // tm),
        in_specs=[pl.BlockSpec((tm, ks), lambda j, i: (i, 0)),
                  pl.BlockSpec((tm, y_att.shape[1]), lambda j, i: (i, 0)),
                  pl.BlockSpec((tm, tn), lambda j, i: (i, j)),
                  pl.BlockSpec((tm, tn), lambda j, i: (i, nj + j)),
                  pl.BlockSpec((ks, tn), lambda j, i: (0, j)),
                  pl.BlockSpec((w_a.shape[0], tn), lambda j, i: (0, j))],
        out_specs=pl.BlockSpec((tm, tn), lambda j, i: (i, j)),
        out_shape=jax.ShapeDtypeStruct((t, d), BF16),
        scratch_shapes=[pltpu.VMEM((ks, tn), BF16), pltpu.VMEM((w_a.shape[0], tn), BF16)],
        compiler_params=_cparams(("arbitrary", "arbitrary")),
        name="branch_merge",
    )(y_ssm, y_att, gates, gates, w_s, w_a)


def _lane_min_index(mask, lane):
    return jnp.min(jnp.where(mask, lane, float(ROUTE_LANES)), axis=1, keepdims=True)


def _outproj_body(x_ref, m_ref, w_ref, g_ref, wrc_ref, br_ref,
                  x1_ref, h2_ref, rr_ref, rt_ref):
    x1 = x_ref[...] + jnp.dot(m_ref[...], w_ref[...], preferred_element_type=F32)
    x1_ref[...] = x1
    ms = jnp.mean(x1 * x1, axis=-1, keepdims=True)
    h2 = x1 * lax.rsqrt(ms + EPS) * g_ref[...]
    h2_ref[...] = h2
    hb = h2.astype(BF16)

    hl = (h2 - hb.astype(F32)).astype(BF16)
    p = jnp.dot(hb, wrc_ref[...], preferred_element_type=F32)
    logits = (p[:, :ROUTE_LANES] + p[:, ROUTE_LANES:]
              + jnp.dot(hl, wrc_ref[:, :ROUTE_LANES], preferred_element_type=F32)) + br_ref[...]
    tm = logits.shape[0]
    lane = lax.broadcasted_iota(I32, (tm, ROUTE_LANES), 1).astype(F32)
    is_group = (lane >= N_EXPERTS) & (lane < N_EXPERTS + N_EXPERT_GROUPS)
    gl = jnp.where(is_group, logits, NEG_BIG)
    gmax = jnp.max(gl, axis=1, keepdims=True)
    gidx = _lane_min_index(gl == gmax, lane) - N_EXPERTS
    g_top = 1.0 / jnp.sum(jnp.exp(gl - gmax), axis=1, keepdims=True)
    lo = gidx * EXPERTS_PER_GROUP
    in_grp = (lane >= lo) & (lane < lo + EXPERTS_PER_GROUP)
    el = jnp.where(in_grp, logits, NEG_BIG)
    m1 = jnp.max(el, axis=1, keepdims=True)
    i1 = _lane_min_index(el == m1, lane)
    el2 = jnp.where(lane == i1, NEG_BIG, el)
    m2 = jnp.max(el2, axis=1, keepdims=True)
    i2 = _lane_min_index(el2 == m2, lane)
    dlt = jnp.exp(m2 - m1)
    w1 = 1.0 / (1.0 + dlt)
    w2 = dlt * w1
    route = jnp.where(lane == 0, i1,
                      jnp.where(lane == 1, i2,
                                jnp.where(lane == 2, g_top * w1,
                                          jnp.where(lane == 3, g_top * w2, 0.0))))
    rr_ref[...] = route
    rt_ref[...] = jnp.transpose(route)[0:8, :]


def _outproj(x2, merged, w_out, g, wr, br, tm=OUTPROJ_TM):
    t, d = x2.shape
    wr_hi = wr.astype(BF16)
    wr_cat = jnp.concatenate([wr_hi, (wr - wr_hi.astype(F32)).astype(BF16)], axis=1)
    const = lambda *shape: pl.BlockSpec(shape, lambda i: (0,) * len(shape))
    return pl.pallas_call(
        _outproj_body,
        grid=(t // tm,),
        in_specs=[pl.BlockSpec((tm, d), lambda i: (i, 0)), pl.BlockSpec((tm, d), lambda i: (i, 0)),
                  const(d, d), const(1, d), const(d, 2 * ROUTE_LANES), const(1, ROUTE_LANES)],
        out_specs=[pl.BlockSpec((tm, d), lambda i: (i, 0)),
                   pl.BlockSpec((tm, d), lambda i: (i, 0)),
                   pl.BlockSpec((tm, ROUTE_LANES), lambda i: (i, 0)),
                   pl.BlockSpec((8, tm), lambda i: (0, i))],
        out_shape=[jax.ShapeDtypeStruct((t, d), F32),
                   jax.ShapeDtypeStruct((t, d), F32),
                   jax.ShapeDtypeStruct((t, ROUTE_LANES), F32),
                   jax.ShapeDtypeStruct((8, t), F32)],
        compiler_params=_cparams(("arbitrary",)),
        name="outproj_router",
    )(x2, merged, w_out, g.reshape(1, d), wr_cat, br)


def _count_body(ids_ref, cnt_ref):
    nc, _, c = ids_ref.shape
    eidx = lax.broadcasted_iota(I32, (N_EXPERTS, c), 0)

    def step(i, acc):
        return acc + jnp.where(ids_ref[i] == eidx, 1.0, 0.0)

    acc = lax.fori_loop(0, nc, step, jnp.zeros((N_EXPERTS, c), F32))
    tot = jnp.sum(acc, axis=1, keepdims=True)
    cnt_ref[...] = jnp.broadcast_to(tot, cnt_ref.shape).astype(I32)


def _expert_counts(ids3):
    return pl.pallas_call(
        _count_body,
        out_shape=jax.ShapeDtypeStruct((N_EXPERTS, LANES), I32),
        name="expert_counts",
    )(ids3)


def _pos_body(ids_ref, start_ref, incl_ref, pos_ref):
    nc, _, c = ids_ref.shape
    eidx = lax.broadcasted_iota(I32, (N_EXPERTS, c), 0)
    incl = incl_ref[...]

    def step(i, carry):
        onehot = ids_ref[i] == eidx
        cum = jnp.dot(jnp.where(onehot, 1.0, 0.0).astype(BF16), incl, preferred_element_type=F32)
        val = jnp.where(onehot, cum - 1.0 + carry, 0.0)
        pos_ref[i] = jnp.sum(val, axis=0, keepdims=True).astype(I32)
        return carry + cum[:, c - 1:c]

    lax.fori_loop(0, nc, step, start_ref[:, 0:1].astype(F32))


def _sorted_positions(ids3, starts):
    nc, _, c = ids3.shape
    r = jnp.arange(c)
    incl = (r[:, None] <= r[None, :]).astype(BF16)
    start_b = jnp.broadcast_to(starts.astype(I32)[:, None], (N_EXPERTS, LANES))
    return pl.pallas_call(
        _pos_body,
        out_shape=jax.ShapeDtypeStruct((nc, 1, c), I32),
        name="sorted_positions",
    )(ids3, start_b, incl)


def _dispatch_body(pos_ref, src_ref, dst_ref, buf, lsem, ssem, *, n_tok, tm):
    i = pl.program_id(0)
    nsteps = pl.num_programs(0)
    base = i * tm

    def load(tile):
        b = tile % DISPATCH_BUFS
        return pltpu.make_async_copy(src_ref.at[pl.ds(tile * tm, tm), :], buf.at[b], lsem.at[b])

    def drain(tile):
        b = tile % DISPATCH_BUFS
        for s in range(2):
            pltpu.make_async_copy(buf.at[b], dst_ref.at[pl.ds(0, tm), :], ssem.at[2 * b + s]).wait()

    @pl.when(i == 0)
    def _():
        load(0).start()

        @pl.when(nsteps > 1)
        def _():
            load(1).start()

    load(i).wait()
    b = i % DISPATCH_BUFS

    def issue(ii, carry):
        for k in range(DMA_UNROLL):
            j = ii * DMA_UNROLL + k
            row = buf.at[b, pl.ds(j, 1), :]
            pltpu.make_async_copy(row, dst_ref.at[pl.ds(pos_ref[base + j], 1), :], ssem.at[2 * b]).start()
            pltpu.make_async_copy(row, dst_ref.at[pl.ds(pos_ref[n_tok + base + j], 1), :], ssem.at[2 * b + 1]).start()
        return carry

    lax.fori_loop(0, tm // DMA_UNROLL, issue, 0)

    @pl.when(i >= 1)
    def _():
        drain(i - 1)

    @pl.when(i + 2 < nsteps)
    def _():
        load(i + 2).start()

    @pl.when(i == nsteps - 1)
    def _():
        drain(i)


def _dispatch(pos_flat, h2, tm=MOE_TM):
    n = pos_flat.shape[0]
    t, d = h2.shape
    assert t % tm == 0 and tm % DMA_UNROLL == 0
    return pl.pallas_call(
        functools.partial(_dispatch_body, n_tok=t, tm=tm),
        grid_spec=pltpu.PrefetchScalarGridSpec(
            num_scalar_prefetch=1, grid=(t // tm,),
            in_specs=[pl.BlockSpec(memory_space=pl.ANY)],
            out_specs=pl.BlockSpec(memory_space=pl.ANY),
            scratch_shapes=[pltpu.VMEM((DISPATCH_BUFS, tm, d), h2.dtype),
                            pltpu.SemaphoreType.DMA((DISPATCH_BUFS,)),
                            pltpu.SemaphoreType.DMA((2 * DISPATCH_BUFS,))]),
        out_shape=jax.ShapeDtypeStruct((n, d), h2.dtype),
        compiler_params=_cparams(("arbitrary",)),
        name="dispatch",
    )(pos_flat, h2)


def _gmm_body(tile_ref, exp_ref, nxt_ref, lo_ref, hi_ref, x_ref, wg_ref, wu_ref, wd_ref, o_ref,
              sg, su, sd, wgs, wus, wds, sem):
    w = pl.program_id(0)
    prev = jnp.maximum(w - 1, 0)
    new_expert = (w == 0) | (exp_ref[w] != exp_ref[prev])
    first_of_tile = (w == 0) | (tile_ref[w] != tile_ref[prev])
    lo = lo_ref[w]
    hi = hi_ref[w]

    def stage(e):
        return (pltpu.make_async_copy(wg_ref.at[e], sg, sem.at[0]),
                pltpu.make_async_copy(wu_ref.at[e], su, sem.at[1]),
                pltpu.make_async_copy(wd_ref.at[e], sd, sem.at[2]))

    @pl.when(w == 0)
    def _():
        for cp in stage(exp_ref[0]):
            cp.start()

    @pl.when(new_expert)
    def _():
        for cp in stage(exp_ref[w]):
            cp.wait()
        wgs[...] = sg[...].astype(BF16)
        wus[...] = su[...].astype(BF16)
        wds[...] = sd[...].astype(BF16)

        @pl.when(nxt_ref[w] >= 0)
        def _():
            for cp in stage(nxt_ref[w]):
                cp.start()

    @pl.when(hi > lo)
    def _():
        xx = x_ref[...].astype(BF16)
        g = jnp.dot(xx, wgs[...], preferred_element_type=F32)
        u = jnp.dot(xx, wus[...], preferred_element_type=F32)
        hid = (g * jax.nn.sigmoid(g)) * u
        y = jnp.dot(hid.astype(BF16), wds[...], preferred_element_type=F32)
        rows = lax.broadcasted_iota(I32, (y.shape[0], 1), 0)
        y = jnp.where((rows >= lo) & (rows < hi), y, 0.0)

        @pl.when(first_of_tile)
        def _():
            o_ref[...] = y

        @pl.when(jnp.logical_not(first_of_tile))
        def _():
            o_ref[...] += y


def _gmm(meta, xs, w_gate, w_up, w_down, tm):
    tile_id, expert_id, next_expert, row_lo, row_hi = meta
    n, _ = xs.shape
    e, d, f = w_gate.shape
    nw = tile_id.shape[0]
    hbm = pl.BlockSpec(memory_space=pl.ANY)
    row_tile = lambda w, ti, *_: (ti[w], 0)
    return pl.pallas_call(
        _gmm_body,
        grid_spec=pltpu.PrefetchScalarGridSpec(
            num_scalar_prefetch=5, grid=(nw,),
            in_specs=[pl.BlockSpec((tm, d), row_tile), hbm, hbm, hbm],
            out_specs=pl.BlockSpec((tm, d), row_tile),
            scratch_shapes=[pltpu.VMEM((d, f), F32), pltpu.VMEM((d, f), F32), pltpu.VMEM((f, d), F32),
                            pltpu.VMEM((d, f), BF16), pltpu.VMEM((d, f), BF16), pltpu.VMEM((f, d), BF16),
                            pltpu.SemaphoreType.DMA((3,))]),
        out_shape=jax.ShapeDtypeStruct((n, d), F32),
        compiler_params=_cparams(("arbitrary",)),
        name="expert_gmm",
    )(tile_id, expert_id, next_expert, row_lo, row_hi, xs, w_gate, w_up, w_down)


def _gmm_metadata(counts, n_rows, tm):
    nt = n_rows // tm
    nw = nt + N_EXPERTS - 1
    ends = jnp.cumsum(counts)
    starts = ends - counts
    first_tile = starts // tm
    n_items = jnp.where(counts > 0, (ends - 1) // tm - first_tile + 1, 0)
    item_end = jnp.cumsum(n_items)
    item_start = item_end - n_items
    total = item_end[-1]
    w = jnp.arange(nw, dtype=I32)
    wc = jnp.minimum(w, total - 1)
    ex = jnp.sum((item_end[None, :] <= wc[:, None]).astype(I32), axis=1)
    tile = first_tile[ex] + (wc - item_start[ex])
    lo = jnp.maximum(starts[ex], tile * tm) - tile * tm
    hi = jnp.minimum(ends[ex], (tile + 1) * tm) - tile * tm
    valid = w < total
    lo = jnp.where(valid, lo, 0)
    hi = jnp.where(valid, hi, 0)
    run_end = item_end[ex]
    nxt = jnp.where(run_end < total, ex[jnp.minimum(run_end, nw - 1)], -1)
    return tile.astype(I32), ex, nxt.astype(I32), lo.astype(I32), hi.astype(I32)


def _combine_body(pos_ref, x1_ref, rr_ref, ys_ref, o_ref, a0, a1, b0, b1, sem, *, n_tok):
    i = pl.program_id(0)
    nsteps = pl.num_programs(0)
    tm = x1_ref.shape[0]

    def gather(tile, bufs, sems):
        base = tile * tm

        def issue(jj, carry):
            for k in range(DMA_UNROLL):
                j = jj * DMA_UNROLL + k
                pltpu.make_async_copy(ys_ref.at[pl.ds(pos_ref[base + j], 1), :],
                                      bufs[0].at[pl.ds(j, 1), :], sem.at[sems[0]]).start()
                pltpu.make_async_copy(ys_ref.at[pl.ds(pos_ref[n_tok + base + j], 1), :],
                                      bufs[1].at[pl.ds(j, 1), :], sem.at[sems[1]]).start()
            return carry

        lax.fori_loop(0, tm // DMA_UNROLL, issue, 0)

    def step(cur, cur_sems, nxt, nxt_sems):
        @pl.when(i == 0)
        def _():
            gather(i, cur, cur_sems)

        @pl.when(i + 1 < nsteps)
        def _():
            gather(i + 1, nxt, nxt_sems)

        for buf, s in zip(cur, cur_sems):
            pltpu.make_async_copy(ys_ref.at[pl.ds(0, tm), :], buf, sem.at[s]).wait()
        rr = rr_ref[...]
        o_ref[...] = x1_ref[...] + rr[:, 2:3] * cur[0][...] + rr[:, 3:4] * cur[1][...]

    @pl.when(i % 2 == 0)
    def _():
        step((a0, a1), (0, 1), (b0, b1), (2, 3))

    @pl.when(i % 2 == 1)
    def _():
        step((b0, b1), (2, 3), (a0, a1), (0, 1))


def _combine(pos_flat, x1, route_rows, ys, tm=MOE_TM):
    t, d = x1.shape
    assert tm % DMA_UNROLL == 0
    buf = pltpu.VMEM((tm, d), F32)
    return pl.pallas_call(
        functools.partial(_combine_body, n_tok=t),
        grid_spec=pltpu.PrefetchScalarGridSpec(
            num_scalar_prefetch=1, grid=(t // tm,),
            in_specs=[pl.BlockSpec((tm, d), lambda i, p: (i, 0)),
                      pl.BlockSpec((tm, ROUTE_LANES), lambda i, p: (i, 0)),
                      pl.BlockSpec(memory_space=pl.ANY)],
            out_specs=pl.BlockSpec((tm, d), lambda i, p: (i, 0)),
            scratch_shapes=[buf, buf, buf, buf, pltpu.SemaphoreType.DMA((4,))]),
        out_shape=jax.ShapeDtypeStruct((t, d), F32),
        compiler_params=_cparams(("arbitrary",)),
        name="combine",
    )(pos_flat, x1, route_rows, ys)


def _moe(x1, h2, route_rows, route_t, w_gate, w_up, w_down, gmm_tm=MOE_TM, sort_chunk=SORT_CHUNK):
    t = x1.shape[0]
    n = 2 * t
    ids3 = route_t[0:2].astype(I32).reshape(n // sort_chunk, 1, sort_chunk)
    counts = _expert_counts(ids3)[:, 0]
    starts = jnp.cumsum(counts) - counts
    pos_flat = _sorted_positions(ids3, starts).reshape(n)
    xs = _dispatch(pos_flat, h2)
    ys = _gmm(_gmm_metadata(counts, n, gmm_tm), xs, w_gate, w_up, w_down, gmm_tm)
    return _combine(pos_flat, x1, route_rows, ys)


def _layer(x, attn_norm_g, w_in, lambda_re, lambda_im, log_dt, ssm_b_re, ssm_b_im, ssm_c_re, ssm_c_im,
           ssm_d, w_glu, q_norm_g, k_norm_g, w_branch_ssm, w_branch_att, w_out, ffn_norm_g,
           router_group_w, router_group_b, router_expert_w, router_expert_b,
           expert_w_gate, expert_w_up, expert_w_down):
    nb, seq, d = x.shape
    t = nb * seq
    ds = w_glu.shape[0]
    da = N_HEADS * HEAD_DIM
    x2 = x.reshape(t, d)

    o = 0
    u = _proj(x2, attn_norm_g, w_in, o, ds, "plain", F32, name="proj_u"); o += ds
    qg = q_norm_g.astype(F32) * (HEAD_DIM ** -0.5 * LOG2_E)
    q = _proj(x2, attn_norm_g, w_in, o, da, "headnorm", BF16, gain=qg, name="proj_q"); o += da
    k = _proj(x2, attn_norm_g, w_in, o, da, "headnorm", BF16, gain=k_norm_g, name="proj_k"); o += da
    v = _proj(x2, attn_norm_g, w_in, o, da, "plain", BF16, name="proj_v"); o += da
    gates = _proj(x2, attn_norm_g, w_in, o, w_in.shape[1] - o, "sigmoid", BF16, name="proj_gates")

    a_re, a_im, wb_re, wb_im, wc = _ssm_params(lambda_re, lambda_im, log_dt, ssm_b_re, ssm_b_im, ssm_c_re, ssm_c_im)
    y_ssm = _ssm(u.reshape(nb, seq, ds), a_re, a_im, wb_re, wb_im, wc, ssm_d.astype(F32), w_glu.astype(BF16))
    y_att = _attention(q.reshape(nb, seq, da), k.reshape(nb, seq, da), v.reshape(nb, seq, da))

    merged = _merge(y_ssm.reshape(t, ds), y_att.reshape(t, da), gates, w_branch_ssm, w_branch_att)

    n_pad = ROUTE_LANES - N_EXPERTS - N_EXPERT_GROUPS
    wr = jnp.concatenate([router_expert_w.astype(F32), router_group_w.astype(F32), jnp.zeros((d, n_pad), F32)], axis=1)
    br = jnp.concatenate([router_expert_b.astype(F32), router_group_b.astype(F32), jnp.zeros((n_pad,), F32)])[None, :]
    x1, h2, route_rows, route_t = _outproj(x2, merged, w_out.astype(BF16), ffn_norm_g, wr, br)

    out = _moe(x1, h2, route_rows, route_t, expert_w_gate, expert_w_up, expert_w_down)
    return out.reshape(nb, seq, d)


def kernel(x, attn_norm_g, w_in, lambda_re, lambda_im, log_dt, ssm_b_re, ssm_b_im, ssm_c_re, ssm_c_im, ssm_d, w_glu, q_norm_g, k_norm_g, w_branch_ssm, w_branch_att, w_out, ffn_norm_g, router_group_w, router_group_b, router_expert_w, router_expert_b, expert_w_gate, expert_w_up, expert_w_down):
    depth = attn_norm_g.shape[0]
    for l in range(depth):
        x = _layer(x, attn_norm_g[l], w_in[l], lambda_re[l], lambda_im[l], log_dt[l], ssm_b_re[l], ssm_b_im[l],
                   ssm_c_re[l], ssm_c_im[l], ssm_d[l], w_glu[l], q_norm_g[l], k_norm_g[l], w_branch_ssm[l],
                   w_branch_att[l], w_out[l], ffn_norm_g[l], router_group_w[l], router_group_b[l],
                   router_expert_w[l], router_expert_b[l], expert_w_gate[l], expert_w_up[l], expert_w_down[l])
    return x
```

```python
import functools

import jax
import jax.numpy as jnp
from jax import lax
from jax.experimental import pallas as pl
from jax.experimental.pallas import tpu as pltpu

F32 = jnp.float32
BF16 = jnp.bfloat16
I32 = jnp.int32

EPS = 1e-6
N_HEADS = 8
HEAD_DIM = 128
N_EXPERT_GROUPS = 4
EXPERTS_PER_GROUP = 8
N_EXPERTS = N_EXPERT_GROUPS * EXPERTS_PER_GROUP

LANES = 128
V7X_VMEM_BYTES = 64 * 1024 * 1024
V7X_VMEM_RESERVED_BYTES = 8 * 1024 * 1024
VMEM_LIMIT = V7X_VMEM_BYTES - V7X_VMEM_RESERVED_BYTES

PROJ_TM, PROJ_TN = 1024, 1024
MERGE_TM, MERGE_TN = 1024, 1024
OUTPROJ_TM = 512
SSM_TC = 64
SSM_LANE_CHUNK = 1024
SSM_GROUPS_PER_BLOCK = 16
ATTN_TQ = 256
ATTN_HEADS_PER_STEP = 8
SORT_CHUNK = 512
MOE_TM = 256
DMA_UNROLL = 16
SCAN_UNROLL = 4
DISPATCH_BUFS = 3

ROUTE_LANES = LANES
NEG_BIG = -1e30
EXP_ZERO_BELOW = -104.0
LOG2_E = 1.4426950408889634
SOFTPLUS2_LINEAR_ABOVE = 30.0


def _cparams(sem):
    return pltpu.CompilerParams(dimension_semantics=sem, vmem_limit_bytes=VMEM_LIMIT)


def _proj_body(x_ref, gn_ref, w_ref, g_ref, o_ref, wb_ref, *, mode):
    @pl.when(pl.program_id(1) == 0)
    def _():
        wb_ref[...] = (w_ref[...] * gn_ref[...]).astype(BF16)

    x = x_ref[...]
    acc = jnp.dot(x.astype(BF16), wb_ref[...], preferred_element_type=F32)
    acc = acc * lax.rsqrt(jnp.mean(x * x, axis=-1, keepdims=True) + EPS)
    if mode == "plain":
        o_ref[...] = acc.astype(o_ref.dtype)
    elif mode == "sigmoid":
        o_ref[...] = jax.nn.sigmoid(acc).astype(o_ref.dtype)
    else:
        n = acc.shape[1] // HEAD_DIM
        for hh in range(n):
            blk = acc[:, hh * HEAD_DIM:(hh + 1) * HEAD_DIM]
            ms = jnp.mean(blk * blk, axis=-1, keepdims=True)
            o_ref[:, hh * HEAD_DIM:(hh + 1) * HEAD_DIM] = (blk * lax.rsqrt(ms + EPS) * g_ref[...]).astype(o_ref.dtype)


def _proj(x2, norm_g, w, col0, n, mode, out_dtype, gain=None, tm=PROJ_TM, tn=PROJ_TN, name="proj"):
    t, k = x2.shape
    assert col0 % tn == 0 and n % tn == 0
    j0 = col0 // tn
    if gain is None:
        gain = jnp.ones((HEAD_DIM,), F32)
    return pl.pallas_call(
        functools.partial(_proj_body, mode=mode),
        grid=(n // tn, t // tm),
        in_specs=[pl.BlockSpec((tm, k), lambda j, i: (i, 0)),
                  pl.BlockSpec((k, 1), lambda j, i: (0, 0)),
                  pl.BlockSpec((k, tn), lambda j, i: (0, j0 + j)),
                  pl.BlockSpec((1, HEAD_DIM), lambda j, i: (0, 0))],
        out_specs=pl.BlockSpec((tm, tn), lambda j, i: (i, j)),
        out_shape=jax.ShapeDtypeStruct((t, n), out_dtype),
        scratch_shapes=[pltpu.VMEM((k, tn), BF16)],
        compiler_params=_cparams(("arbitrary", "arbitrary")),
        name=name,
    )(x2, norm_g.reshape(k, 1).astype(F32), w, gain.reshape(1, HEAD_DIM).astype(F32))


def _ssm_body(u_ref, are_ref, aim_ref, wbre_ref, wbim_ref, wc_ref, d_ref, wglu_ref,
              o_ref, tm_ref, bure_ref, buim_ref, sre_ref, sim_ref, *, tc, lane_chunk):
    nb = u_ref.shape[0]
    ds = u_ref.shape[2]
    rows = nb * tc
    nslab = ds // LANES
    nblk = wbre_ref.shape[0]
    cb = wbre_ref.shape[1]
    sb = wbre_ref.shape[2]
    n_state = nblk * sb

    @pl.when(pl.program_id(0) == 0)
    def _():
        sre_ref[...] = jnp.zeros_like(sre_ref)
        sim_ref[...] = jnp.zeros_like(sim_ref)

    for b in range(nb):
        for j in range(nslab):
            tm_ref[j, pl.ds(b, tc, stride=nb), :] = u_ref[b, :, j * LANES:(j + 1) * LANES]
    u_tm = jnp.concatenate([tm_ref[j] for j in range(nslab)], axis=1).astype(BF16)
    for gb in range(nblk):
        lhs = u_tm[:, gb * cb:(gb + 1) * cb]
        bure_ref[:, gb * sb:(gb + 1) * sb] = jnp.dot(lhs, wbre_ref[gb], preferred_element_type=F32)
        buim_ref[:, gb * sb:(gb + 1) * sb] = jnp.dot(lhs, wbim_ref[gb], preferred_element_type=F32)

    for lc in range(n_state // lane_chunk):
        sl = slice(lc * lane_chunk, (lc + 1) * lane_chunk)
        ar = jnp.broadcast_to(are_ref[:, sl], (nb, lane_chunk))
        ai = jnp.broadcast_to(aim_ref[:, sl], (nb, lane_chunk))

        def steps(tt, carry, sl=sl, ar=ar, ai=ai):
            xr, xi = carry
            for k in range(SCAN_UNROLL):
                r0 = pl.multiple_of((tt * SCAN_UNROLL + k) * nb, nb)
                br = bure_ref[pl.ds(r0, nb), sl]
                bi = buim_ref[pl.ds(r0, nb), sl]
                xr, xi = ar * xr - ai * xi + br, ar * xi + ai * xr + bi
                bure_ref[pl.ds(r0, nb), sl] = xr
                buim_ref[pl.ds(r0, nb), sl] = xi
            return xr, xi

        xr, xi = lax.fori_loop(0, tc // SCAN_UNROLL, steps, (sre_ref[:, sl], sim_ref[:, sl]))
        sre_ref[:, sl] = xr
        sim_ref[:, sl] = xi

    for gb in range(nblk):
        xs = jnp.concatenate([bure_ref[:, gb * sb:(gb + 1) * sb], buim_ref[:, gb * sb:(gb + 1) * sb]], axis=1)
        y_blk = jnp.dot(xs.astype(BF16), wc_ref[gb], preferred_element_type=F32)
        for jj in range(cb // LANES):
            tm_ref[gb * (cb // LANES) + jj] = y_blk[:, jj * LANES:(jj + 1) * LANES]
    y_bm = jnp.concatenate(
        [jnp.concatenate([tm_ref[j, pl.ds(b, tc, stride=nb), :] for j in range(nslab)], axis=1) for b in range(nb)],
        axis=0)
    y = y_bm + d_ref[...] * u_ref[...].reshape(rows, ds)
    z = jax.nn.gelu(y)
    gate = jax.nn.sigmoid(jnp.dot(z.astype(BF16), wglu_ref[...], preferred_element_type=F32))
    o_ref[...] = (z * gate).reshape(nb, tc, ds).astype(o_ref.dtype)


def _ssm(u, a_re, a_im, wb_re, wb_im, wc, d_skip, w_glu, tc=SSM_TC, lane_chunk=SSM_LANE_CHUNK):
    nb, seq, ds = u.shape
    rows = nb * tc
    n_state = a_re.shape[1]
    assert tc % SCAN_UNROLL == 0
    const = lambda *shape: pl.BlockSpec(shape, lambda c: (0,) * len(shape))
    return pl.pallas_call(
        functools.partial(_ssm_body, tc=tc, lane_chunk=lane_chunk),
        grid=(seq // tc,),
        in_specs=[pl.BlockSpec((nb, tc, ds), lambda c: (0, c, 0)),
                  const(1, n_state), const(1, n_state),
                  const(*wb_re.shape), const(*wb_im.shape), const(*wc.shape),
                  const(1, ds), const(ds, ds)],
        out_specs=pl.BlockSpec((nb, tc, ds), lambda c: (0, c, 0)),
        out_shape=jax.ShapeDtypeStruct((nb, seq, ds), BF16),
        scratch_shapes=[pltpu.VMEM((ds // LANES, rows, LANES), F32),
                        pltpu.VMEM((rows, n_state), F32), pltpu.VMEM((rows, n_state), F32),
                        pltpu.VMEM((nb, n_state), F32), pltpu.VMEM((nb, n_state), F32)],
        compiler_params=_cparams(("arbitrary",)),
        name="s5_mixer",
    )(u, a_re, a_im, wb_re, wb_im, wc, d_skip.reshape(1, ds), w_glu)


def _ssm_params(lambda_re, lambda_im, log_dt, b_re, b_im, c_re, c_im):
    g, p = lambda_re.shape
    h = b_re.shape[2]
    nblk = g // SSM_GROUPS_PER_BLOCK
    dt = jnp.exp(log_dt.astype(F32))[:, None]
    lr = lambda_re.astype(F32)
    li = lambda_im.astype(F32)
    mag = jnp.exp(lr * dt)
    abar_re, abar_im = mag * jnp.cos(li * dt), mag * jnp.sin(li * dt)
    nr, ni = abar_re - 1.0, abar_im
    den = lr * lr + li * li
    coef_re = (nr * lr + ni * li) / den
    coef_im = (ni * lr - nr * li) / den
    bbar_re = coef_re[..., None] * b_re - coef_im[..., None] * b_im
    bbar_im = coef_re[..., None] * b_im + coef_im[..., None] * b_re
    gpb = SSM_GROUPS_PER_BLOCK

    def block_diag(m, rows_per_group, cols_per_group):
        r = jnp.arange(gpb * rows_per_group)[:, None] // rows_per_group
        c = jnp.arange(gpb * cols_per_group)[None, :] // cols_per_group
        return jnp.where(r == c, jnp.tile(m, (1, gpb, 1)), 0.0)

    def pack_b(bb):
        m = bb.reshape(nblk, gpb, p, h).transpose(0, 3, 1, 2).reshape(nblk, h, gpb * p)
        return block_diag(m, h, p)

    def pack_c(cc):
        m = cc.reshape(nblk, gpb, h, p).transpose(0, 3, 1, 2).reshape(nblk, p, gpb * h)
        return block_diag(m, p, h)

    wc = jnp.concatenate([pack_c(c_re.astype(F32)), -pack_c(c_im.astype(F32))], axis=1)
    return (abar_re.reshape(1, g * p), abar_im.reshape(1, g * p),
            pack_b(bbar_re).astype(BF16), pack_b(bbar_im).astype(BF16), wc.astype(BF16))


def _attn_body(q_ref, k_ref, v_ref, tri_ref, o_ref, acc_ref, rs_ref, *, tq, nhs):
    qi = pl.program_id(2)
    tri = tri_ref[...]
    row = lax.broadcasted_iota(I32, (nhs * tq, tq), 0) % tq
    col = lax.broadcasted_iota(I32, (nhs * tq, tq), 1)
    causal = col < row

    def sweep(j, masked):
        k0 = pl.multiple_of(j * tq, tq)
        heads = [slice(h * HEAD_DIM, (h + 1) * HEAD_DIM) for h in range(nhs)]
        z = jnp.concatenate(
            [lax.dot_general(q_ref[0, :, hs], k_ref[0, pl.ds(k0, tq), hs], (((1,), (1,)), ((), ())),
                             preferred_element_type=F32) for hs in heads], axis=0)
        sp = jnp.where(z > SOFTPLUS2_LINEAR_ABOVE, z, jnp.log(1.0 + jnp.exp2(z)) * LOG2_E)
        spm = jnp.where(causal, sp, 0.0) if masked else sp
        spb = spm.astype(BF16)
        tail = jnp.dot(spb, tri, preferred_element_type=F32)
        rsum = rs_ref[...]
        w = jnp.exp2(z - (sp + tail + jnp.concatenate([rsum] * (tq // LANES), axis=1)))
        if masked:
            w = jnp.where(causal, w, 0.0)
        wb = w.astype(BF16)
        for h, hs in enumerate(heads):
            acc_ref[h] += jnp.dot(wb[h * tq:(h + 1) * tq], v_ref[0, pl.ds(k0, tq), hs], preferred_element_type=F32)
        total = tail[:, 0:1] + spb[:, 0:1].astype(F32)
        rsum = rsum + jnp.broadcast_to(total, rsum.shape)
        rs_ref[...] = rsum
        return (jnp.min(rsum) < -EXP_ZERO_BELOW * LOG2_E).astype(I32)

    acc_ref[...] = jnp.zeros_like(acc_ref)
    rs_ref[...] = jnp.zeros_like(rs_ref)
    live = sweep(qi, True)

    def cond(c):
        return (c[0] >= 0) & (c[1] > 0)

    def body(c):
        return c[0] - 1, sweep(c[0], False)

    lax.while_loop(cond, body, (qi - 1, live))
    for h in range(nhs):
        o_ref[0, :, h * HEAD_DIM:(h + 1) * HEAD_DIM] = acc_ref[h].astype(o_ref.dtype)


def _attention(q, k, v, tq=ATTN_TQ, nhs=ATTN_HEADS_PER_STEP):
    nb, seq, da = q.shape
    nh = da // HEAD_DIM
    wd = nhs * HEAD_DIM
    r = jnp.arange(tq)
    tri = (r[:, None] > r[None, :]).astype(BF16)
    return pl.pallas_call(
        functools.partial(_attn_body, tq=tq, nhs=nhs),
        grid=(nb, nh // nhs, seq // tq),
        in_specs=[pl.BlockSpec((1, tq, wd), lambda b, h, i: (b, i, h)),
                  pl.BlockSpec((1, seq, wd), lambda b, h, i: (b, 0, h)),
                  pl.BlockSpec((1, seq, wd), lambda b, h, i: (b, 0, h)),
                  pl.BlockSpec((tq, tq), lambda b, h, i: (0, 0))],
        out_specs=pl.BlockSpec((1, tq, wd), lambda b, h, i: (b, i, h)),
        out_shape=jax.ShapeDtypeStruct((nb, seq, da), BF16),
        scratch_shapes=[pltpu.VMEM((nhs, tq, HEAD_DIM), F32), pltpu.VMEM((nhs * tq, LANES), F32)],
        compiler_params=_cparams(("arbitrary", "arbitrary", "arbitrary")),
        name="stick_attention",
    )(q, k, v, tri)


def _merge_body(ys_ref, ya_ref, gs_ref, ga_ref, ws_ref, wa_ref, o_ref, wsb_ref, wab_ref):
    @pl.when(pl.program_id(1) == 0)
    def _():
        wsb_ref[...] = ws_ref[...].astype(BF16)
        wab_ref[...] = wa_ref[...].astype(BF16)

    ps = jnp.dot(ys_ref[...], wsb_ref[...], preferred_element_type=F32)
    pa = jnp.dot(ya_ref[...], wab_ref[...], preferred_element_type=F32)
    o_ref[...] = (gs_ref[...].astype(F32) * ps + ga_ref[...].astype(F32) * pa).astype(o_ref.dtype)


def _merge(y_ssm, y_att, gates, w_s, w_a, tm=MERGE_TM, tn=MERGE_TN):
    t, ks = y_ssm.shape
    d = w_s.shape[1]
    nj = d // tn
    return pl.pallas_call(
        _merge_body,
        grid=(nj, t // tm),
        in_specs=[pl.BlockSpec((tm, ks), lambda j, i: (i, 0)),
                  pl.BlockSpec((tm, y_att.shape[1]), lambda j, i: (i, 0)),
                  pl.BlockSpec((tm, tn), lambda j, i: (i, j)),
                  pl.BlockSpec((tm, tn), lambda j, i: (i, nj + j)),
                  pl.BlockSpec((ks, tn), lambda j, i: (0, j)),
                  pl.BlockSpec((w_a.shape[0], tn), lambda j, i: (0, j))],
        out_specs=pl.BlockSpec((tm, tn), lambda j, i: (i, j)),
        out_shape=jax.ShapeDtypeStruct((t, d), BF16),
        scratch_shapes=[pltpu.VMEM((ks, tn), BF16), pltpu.VMEM((w_a.shape[0], tn), BF16)],
        compiler_params=_cparams(("arbitrary", "arbitrary")),
        name="branch_merge",
    )(y_ssm, y_att, gates, gates, w_s, w_a)


def _lane_min_index(mask, lane):
    return jnp.min(jnp.where(mask, lane, float(ROUTE_LANES)), axis=1, keepdims=True)


def _outproj_body(x_ref, m_ref, w_ref, g_ref, wrc_ref, br_ref,
                  x1_ref, h2_ref, rr_ref, rt_ref):
    x1 = x_ref[...] + jnp.dot(m_ref[...], w_ref[...], preferred_element_type=F32)
    x1_ref[...] = x1
    ms = jnp.mean(x1 * x1, axis=-1, keepdims=True)
    h2 = x1 * lax.rsqrt(ms + EPS) * g_ref[...]
    h2_ref[...] = h2
    hb = h2.astype(BF16)

    hl = (h2 - hb.astype(F32)).astype(BF16)
    p = jnp.dot(hb, wrc_ref[...], preferred_element_type=F32)
    logits = (p[:, :ROUTE_LANES] + p[:, ROUTE_LANES:]
              + jnp.dot(hl, wrc_ref[:, :ROUTE_LANES], preferred_element_type=F32)) + br_ref[...]
    tm = logits.shape[0]
    lane = lax.broadcasted_iota(I32, (tm, ROUTE_LANES), 1).astype(F32)
    is_group = (lane >= N_EXPERTS) & (lane < N_EXPERTS + N_EXPERT_GROUPS)
    gl = jnp.where(is_group, logits, NEG_BIG)
    gmax = jnp.max(gl, axis=1, keepdims=True)
    gidx = _lane_min_index(gl == gmax, lane) - N_EXPERTS
    g_top = 1.0 / jnp.sum(jnp.exp(gl - gmax), axis=1, keepdims=True)
    lo = gidx * EXPERTS_PER_GROUP
    in_grp = (lane >= lo) & (lane < lo + EXPERTS_PER_GROUP)
    el = jnp.where(in_grp, logits, NEG_BIG)
    m1 = jnp.max(el, axis=1, keepdims=True)
    i1 = _lane_min_index(el == m1, lane)
    el2 = jnp.where(lane == i1, NEG_BIG, el)
    m2 = jnp.max(el2, axis=1, keepdims=True)
    i2 = _lane_min_index(el2 == m2, lane)
    dlt = jnp.exp(m2 - m1)
    w1 = 1.0 / (1.0 + dlt)
    w2 = dlt * w1
    route = jnp.where(lane == 0, i1,
                      jnp.where(lane == 1, i2,
                                jnp.where(lane == 2, g_top * w1,
                                          jnp.where(lane == 3, g_top * w2, 0.0))))
    rr_ref[...] = route
    rt_ref[...] = jnp.transpose(route)[0:8, :]


def _outproj(x2, merged, w_out, g, wr, br, tm=OUTPROJ_TM):
    t, d = x2.shape
    wr_hi = wr.astype(BF16)
    wr_cat = jnp.concatenate([wr_hi, (wr - wr_hi.astype(F32)).astype(BF16)], axis=1)
    const = lambda *shape: pl.BlockSpec(shape, lambda i: (0,) * len(shape))
    return pl.pallas_call(
        _outproj_body,
        grid=(t // tm,),
        in_specs=[pl.BlockSpec((tm, d), lambda i: (i, 0)), pl.BlockSpec((tm, d), lambda i: (i, 0)),
                  const(d, d), const(1, d), const(d, 2 * ROUTE_LANES), const(1, ROUTE_LANES)],
        out_specs=[pl.BlockSpec((tm, d), lambda i: (i, 0)),
                   pl.BlockSpec((tm, d), lambda i: (i, 0)),
                   pl.BlockSpec((tm, ROUTE_LANES), lambda i: (i, 0)),
                   pl.BlockSpec((8, tm), lambda i: (0, i))],
        out_shape=[jax.ShapeDtypeStruct((t, d), F32),
                   jax.ShapeDtypeStruct((t, d), F32),
                   jax.ShapeDtypeStruct((t, ROUTE_LANES), F32),
                   jax.ShapeDtypeStruct((8, t), F32)],
        compiler_params=_cparams(("arbitrary",)),
        name="outproj_router",
    )(x2, merged, w_out, g.reshape(1, d), wr_cat, br)


def _count_body(ids_ref, cnt_ref):
    nc, _, c = ids_ref.shape
    eidx = lax.broadcasted_iota(I32, (N_EXPERTS, c), 0)

    def step(i, acc):
        return acc + jnp.where(ids_ref[i] == eidx, 1.0, 0.0)

    acc = lax.fori_loop(0, nc, step, jnp.zeros((N_EXPERTS, c), F32))
    tot = jnp.sum(acc, axis=1, keepdims=True)
    cnt_ref[...] = jnp.broadcast_to(tot, cnt_ref.shape).astype(I32)


def _expert_counts(ids3):
    return pl.pallas_call(
        _count_body,
        out_shape=jax.ShapeDtypeStruct((N_EXPERTS, LANES), I32),
        name="expert_counts",
    )(ids3)


def _pos_body(ids_ref, start_ref, incl_ref, pos_ref):
    nc, _, c = ids_ref.shape
    eidx = lax.broadcasted_iota(I32, (N_EXPERTS, c), 0)
    incl = incl_ref[...]

    def step(i, carry):
        onehot = ids_ref[i] == eidx
        cum = jnp.dot(jnp.where(onehot, 1.0, 0.0).astype(BF16), incl, preferred_element_type=F32)
        val = jnp.where(onehot, cum - 1.0 + carry, 0.0)
        pos_ref[i] = jnp.sum(val, axis=0, keepdims=True).astype(I32)
        return carry + cum[:, c - 1:c]

    lax.fori_loop(0, nc, step, start_ref[:, 0:1].astype(F32))


def _sorted_positions(ids3, starts):
    nc, _, c = ids3.shape
    r = jnp.arange(c)
    incl = (r[:, None] <= r[None, :]).astype(BF16)
    start_b = jnp.broadcast_to(starts.astype(I32)[:, None], (N_EXPERTS, LANES))
    return pl.pallas_call(
        _pos_body,
        out_shape=jax.ShapeDtypeStruct((nc, 1, c), I32),
        name="sorted_positions",
    )(ids3, start_b, incl)


def _dispatch_body(pos_ref, src_ref, dst_ref, buf, lsem, ssem, *, n_tok, tm):
    i = pl.program_id(0)
    nsteps = pl.num_programs(0)
    base = i * tm

    def load(tile):
        b = tile % DISPATCH_BUFS
        return pltpu.make_async_copy(src_ref.at[pl.ds(tile * tm, tm), :], buf.at[b], lsem.at[b])

    def drain(tile):
        b = tile % DISPATCH_BUFS
        for s in range(2):
            pltpu.make_async_copy(buf.at[b], dst_ref.at[pl.ds(0, tm), :], ssem.at[2 * b + s]).wait()

    @pl.when(i == 0)
    def _():
        load(0).start()

        @pl.when(nsteps > 1)
        def _():
            load(1).start()

    load(i).wait()
    b = i % DISPATCH_BUFS

    def issue(ii, carry):
        for k in range(DMA_UNROLL):
            j = ii * DMA_UNROLL + k
            row = buf.at[b, pl.ds(j, 1), :]
            pltpu.make_async_copy(row, dst_ref.at[pl.ds(pos_ref[base + j], 1), :], ssem.at[2 * b]).start(priority=0)
            pltpu.make_async_copy(row, dst_ref.at[pl.ds(pos_ref[n_tok + base + j], 1), :],
                                  ssem.at[2 * b + 1]).start(priority=1)
        return carry

    lax.fori_loop(0, tm // DMA_UNROLL, issue, 0)

    @pl.when(i >= 1)
    def _():
        drain(i - 1)

    @pl.when(i + 2 < nsteps)
    def _():
        load(i + 2).start()

    @pl.when(i == nsteps - 1)
    def _():
        drain(i)


def _dispatch(pos_flat, h2, tm=MOE_TM):
    n = pos_flat.shape[0]
    t, d = h2.shape
    assert t % tm == 0 and tm % DMA_UNROLL == 0
    return pl.pallas_call(
        functools.partial(_dispatch_body, n_tok=t, tm=tm),
        grid_spec=pltpu.PrefetchScalarGridSpec(
            num_scalar_prefetch=1, grid=(t // tm,),
            in_specs=[pl.BlockSpec(memory_space=pl.ANY)],
            out_specs=pl.BlockSpec(memory_space=pl.ANY),
            scratch_shapes=[pltpu.VMEM((DISPATCH_BUFS, tm, d), h2.dtype),
                            pltpu.SemaphoreType.DMA((DISPATCH_BUFS,)),
                            pltpu.SemaphoreType.DMA((2 * DISPATCH_BUFS,))]),
        out_shape=jax.ShapeDtypeStruct((n, d), h2.dtype),
        compiler_params=_cparams(("arbitrary",)),
        name="dispatch",
    )(pos_flat, h2)


def _gmm_body(tile_ref, exp_ref, nxt_ref, lo_ref, hi_ref, x_ref, wg_ref, wu_ref, wd_ref, o_ref,
              sg, su, sd, wgs, wus, wds, sem):
    w = pl.program_id(0)
    prev = jnp.maximum(w - 1, 0)
    new_expert = (w == 0) | (exp_ref[w] != exp_ref[prev])
    first_of_tile = (w == 0) | (tile_ref[w] != tile_ref[prev])
    lo = lo_ref[w]
    hi = hi_ref[w]

    def stage(e):
        return (pltpu.make_async_copy(wg_ref.at[e], sg, sem.at[0]),
                pltpu.make_async_copy(wu_ref.at[e], su, sem.at[1]),
                pltpu.make_async_copy(wd_ref.at[e], sd, sem.at[2]))

    @pl.when(w == 0)
    def _():
        for cp in stage(exp_ref[0]):
            cp.start()

    @pl.when(new_expert)
    def _():
        for cp in stage(exp_ref[w]):
            cp.wait()
        wgs[...] = sg[...].astype(BF16)
        wus[...] = su[...].astype(BF16)
        wds[...] = sd[...].astype(BF16)

        @pl.when(nxt_ref[w] >= 0)
        def _():
            for cp in stage(nxt_ref[w]):
                cp.start()

    @pl.when(hi > lo)
    def _():
        xx = x_ref[...].astype(BF16)
        g = jnp.dot(xx, wgs[...], preferred_element_type=F32)
        u = jnp.dot(xx, wus[...], preferred_element_type=F32)
        hid = (g * jax.nn.sigmoid(g)) * u
        y = jnp.dot(hid.astype(BF16), wds[...], preferred_element_type=F32)
        rows = lax.broadcasted_iota(I32, (y.shape[0], 1), 0)
        y = jnp.where((rows >= lo) & (rows < hi), y, 0.0)

        @pl.when(first_of_tile)
        def _():
            o_ref[...] = y

        @pl.when(jnp.logical_not(first_of_tile))
        def _():
            o_ref[...] += y


def _gmm(meta, xs, w_gate, w_up, w_down, tm):
    tile_id, expert_id, next_expert, row_lo, row_hi = meta
    n, _ = xs.shape
    e, d, f = w_gate.shape
    nw = tile_id.shape[0]
    hbm = pl.BlockSpec(memory_space=pl.ANY)
    row_tile = lambda w, ti, *_: (ti[w], 0)
    return pl.pallas_call(
        _gmm_body,
        grid_spec=pltpu.PrefetchScalarGridSpec(
            num_scalar_prefetch=5, grid=(nw,),
            in_specs=[pl.BlockSpec((tm, d), row_tile), hbm, hbm, hbm],
            out_specs=pl.BlockSpec((tm, d), row_tile),
            scratch_shapes=[pltpu.VMEM((d, f), F32), pltpu.VMEM((d, f), F32), pltpu.VMEM((f, d), F32),
                            pltpu.VMEM((d, f), BF16), pltpu.VMEM((d, f), BF16), pltpu.VMEM((f, d), BF16),
                            pltpu.SemaphoreType.DMA((3,))]),
        out_shape=jax.ShapeDtypeStruct((n, d), F32),
        compiler_params=_cparams(("arbitrary",)),
        name="expert_gmm",
    )(tile_id, expert_id, next_expert, row_lo, row_hi, xs, w_gate, w_up, w_down)


def _gmm_metadata(counts, n_rows, tm):
    nt = n_rows // tm
    nw = nt + N_EXPERTS - 1
    ends = jnp.cumsum(counts)
    starts = ends - counts
    first_tile = starts // tm
    n_items = jnp.where(counts > 0, (ends - 1) // tm - first_tile + 1, 0)
    item_end = jnp.cumsum(n_items)
    item_start = item_end - n_items
    total = item_end[-1]
    w = jnp.arange(nw, dtype=I32)
    wc = jnp.minimum(w, total - 1)
    ex = jnp.sum((item_end[None, :] <= wc[:, None]).astype(I32), axis=1)
    tile = first_tile[ex] + (wc - item_start[ex])
    lo = jnp.maximum(starts[ex], tile * tm) - tile * tm
    hi = jnp.minimum(ends[ex], (tile + 1) * tm) - tile * tm
    valid = w < total
    lo = jnp.where(valid, lo, 0)
    hi = jnp.where(valid, hi, 0)
    run_end = item_end[ex]
    nxt = jnp.where(run_end < total, ex[jnp.minimum(run_end, nw - 1)], -1)
    return tile.astype(I32), ex, nxt.astype(I32), lo.astype(I32), hi.astype(I32)


def _combine_body(pos_ref, x1_ref, rr_ref, ys_ref, o_ref, a0, a1, b0, b1, sem, *, n_tok):
    i = pl.program_id(0)
    nsteps = pl.num_programs(0)
    tm = x1_ref.shape[0]

    def gather(tile, bufs, sems):
        base = tile * tm

        def issue(jj, carry):
            for k in range(DMA_UNROLL):
                j = jj * DMA_UNROLL + k
                pltpu.make_async_copy(ys_ref.at[pl.ds(pos_ref[base + j], 1), :],
                                      bufs[0].at[pl.ds(j, 1), :], sem.at[sems[0]]).start(priority=0)
                pltpu.make_async_copy(ys_ref.at[pl.ds(pos_ref[n_tok + base + j], 1), :],
                                      bufs[1].at[pl.ds(j, 1), :], sem.at[sems[1]]).start(priority=1)
            return carry

        lax.fori_loop(0, tm // DMA_UNROLL, issue, 0)

    def step(cur, cur_sems, nxt, nxt_sems):
        @pl.when(i == 0)
        def _():
            gather(i, cur, cur_sems)

        @pl.when(i + 1 < nsteps)
        def _():
            gather(i + 1, nxt, nxt_sems)

        for buf, s in zip(cur, cur_sems):
            pltpu.make_async_copy(ys_ref.at[pl.ds(0, tm), :], buf, sem.at[s]).wait()
        rr = rr_ref[...]
        o_ref[...] = x1_ref[...] + rr[:, 2:3] * cur[0][...] + rr[:, 3:4] * cur[1][...]

    @pl.when(i % 2 == 0)
    def _():
        step((a0, a1), (0, 1), (b0, b1), (2, 3))

    @pl.when(i % 2 == 1)
    def _():
        step((b0, b1), (2, 3), (a0, a1), (0, 1))


def _combine(pos_flat, x1, route_rows, ys, tm=MOE_TM):
    t, d = x1.shape
    assert tm % DMA_UNROLL == 0
    buf = pltpu.VMEM((tm, d), F32)
    return pl.pallas_call(
        functools.partial(_combine_body, n_tok=t),
        grid_spec=pltpu.PrefetchScalarGridSpec(
            num_scalar_prefetch=1, grid=(t // tm,),
            in_specs=[pl.BlockSpec((tm, d), lambda i, p: (i, 0)),
                      pl.BlockSpec((tm, ROUTE_LANES), lambda i, p: (i, 0)),
                      pl.BlockSpec(memory_space=pl.ANY)],
            out_specs=pl.BlockSpec((tm, d), lambda i, p: (i, 0)),
            scratch_shapes=[buf, buf, buf, buf, pltpu.SemaphoreType.DMA((4,))]),
        out_shape=jax.ShapeDtypeStruct((t, d), F32),
        compiler_params=_cparams(("arbitrary",)),
        name="combine",
    )(pos_flat, x1, route_rows, ys)


def _moe(x1, h2, route_rows, route_t, w_gate, w_up, w_down, gmm_tm=MOE_TM, sort_chunk=SORT_CHUNK):
    t = x1.shape[0]
    n = 2 * t
    ids3 = route_t[0:2].astype(I32).reshape(n // sort_chunk, 1, sort_chunk)
    counts = _expert_counts(ids3)[:, 0]
    starts = jnp.cumsum(counts) - counts
    pos_flat = _sorted_positions(ids3, starts).reshape(n)
    xs = _dispatch(pos_flat, h2)
    ys = _gmm(_gmm_metadata(counts, n, gmm_tm), xs, w_gate, w_up, w_down, gmm_tm)
    return _combine(pos_flat, x1, route_rows, ys)


def _layer(x, attn_norm_g, w_in, lambda_re, lambda_im, log_dt, ssm_b_re, ssm_b_im, ssm_c_re, ssm_c_im,
           ssm_d, w_glu, q_norm_g, k_norm_g, w_branch_ssm, w_branch_att, w_out, ffn_norm_g,
           router_group_w, router_group_b, router_expert_w, router_expert_b,
           expert_w_gate, expert_w_up, expert_w_down):
    nb, seq, d = x.shape
    t = nb * seq
    ds = w_glu.shape[0]
    da = N_HEADS * HEAD_DIM
    x2 = x.reshape(t, d)

    o = 0
    u = _proj(x2, attn_norm_g, w_in, o, ds, "plain", F32, name="proj_u"); o += ds
    qg = q_norm_g.astype(F32) * (HEAD_DIM ** -0.5 * LOG2_E)
    q = _proj(x2, attn_norm_g, w_in, o, da, "headnorm", BF16, gain=qg, name="proj_q"); o += da
    k = _proj(x2, attn_norm_g, w_in, o, da, "headnorm", BF16, gain=k_norm_g, name="proj_k"); o += da
    v = _proj(x2, attn_norm_g, w_in, o, da, "plain", BF16, name="proj_v"); o += da
    gates = _proj(x2, attn_norm_g, w_in, o, w_in.shape[1] - o, "sigmoid", BF16, name="proj_gates")

    a_re, a_im, wb_re, wb_im, wc = _ssm_params(lambda_re, lambda_im, log_dt, ssm_b_re, ssm_b_im, ssm_c_re, ssm_c_im)
    y_ssm = _ssm(u.reshape(nb, seq, ds), a_re, a_im, wb_re, wb_im, wc, ssm_d.astype(F32), w_glu.astype(BF16))
    y_att = _attention(q.reshape(nb, seq, da), k.reshape(nb, seq, da), v.reshape(nb, seq, da))

    merged = _merge(y_ssm.reshape(t, ds), y_att.reshape(t, da), gates, w_branch_ssm, w_branch_att)

    n_pad = ROUTE_LANES - N_EXPERTS - N_EXPERT_GROUPS
    wr = jnp.concatenate([router_expert_w.astype(F32), router_group_w.astype(F32), jnp.zeros((d, n_pad), F32)], axis=1)
    br = jnp.concatenate([router_expert_b.astype(F32), router_group_b.astype(F32), jnp.zeros((n_pad,), F32)])[None, :]
    x1, h2, route_rows, route_t = _outproj(x2, merged, w_out.astype(BF16), ffn_norm_g, wr, br)

    out = _moe(x1, h2, route_rows, route_t, expert_w_gate, expert_w_up, expert_w_down)
    return out.reshape(nb, seq, d)


def kernel(x, attn_norm_g, w_in, lambda_re, lambda_im, log_dt, ssm_b_re, ssm_b_im, ssm_c_re, ssm_c_im, ssm_d, w_glu, q_norm_g, k_norm_g, w_branch_ssm, w_branch_att, w_out, ffn_norm_g, router_group_w, router_group_b, router_expert_w, router_expert_b, expert_w_gate, expert_w_up, expert_w_down):
    depth = attn_norm_g.shape[0]
    for l in range(depth):
        x = _layer(x, attn_norm_g[l], w_in[l], lambda_re[l], lambda_im[l], log_dt[l], ssm_b_re[l], ssm_b_im[l],
                   ssm_c_re[l], ssm_c_im[l], ssm_d[l], w_glu[l], q_norm_g[l], k_norm_g[l], w_branch_ssm[l],
                   w_branch_att[l], w_out[l], ffn_norm_g[l], router_group_w[l], router_group_b[l],
                   router_expert_w[l], router_expert_b[l], expert_w_gate[l], expert_w_up[l], expert_w_down[l])
    return x
```

```python
import functools

import jax
import jax.numpy as jnp
from jax import lax
from jax.experimental import pallas as pl
from jax.experimental.pallas import tpu as pltpu

F32 = jnp.float32
BF16 = jnp.bfloat16
I32 = jnp.int32

EPS = 1e-6
N_HEADS = 8
HEAD_DIM = 128
N_EXPERT_GROUPS = 4
EXPERTS_PER_GROUP = 8
N_EXPERTS = N_EXPERT_GROUPS * EXPERTS_PER_GROUP

LANES = 128
V7X_VMEM_BYTES = 64 * 1024 * 1024
V7X_VMEM_RESERVED_BYTES = 8 * 1024 * 1024
VMEM_LIMIT = V7X_VMEM_BYTES - V7X_VMEM_RESERVED_BYTES

PROJ_TM, PROJ_TN = 1024, 1024
MERGE_TM, MERGE_TN = 1024, 1024
OUTPROJ_TM = 512
SSM_TC = 64
SSM_LANE_CHUNK = 1024
SSM_GROUPS_PER_BLOCK = 16
ATTN_TQ = 256
ATTN_HEADS_PER_STEP = 8
SORT_CHUNK = 512
MOE_TM = 256
DMA_UNROLL = 16
SCAN_UNROLL = 4
DISPATCH_BUFS = 3

ROUTE_LANES = LANES
NEG_BIG = -1e30
EXP_ZERO_BELOW = -104.0
LOG2_E = 1.4426950408889634
SOFTPLUS2_LINEAR_ABOVE = 30.0


def _cparams(sem):
    return pltpu.CompilerParams(dimension_semantics=sem, vmem_limit_bytes=VMEM_LIMIT)


def _proj_body(x_ref, gn_ref, w_ref, g_ref, o_ref, wb_ref, *, mode):
    @pl.when(pl.program_id(1) == 0)
    def _():
        wb_ref[...] = (w_ref[...] * gn_ref[...]).astype(BF16)

    x = x_ref[...]
    acc = jnp.dot(x.astype(BF16), wb_ref[...], preferred_element_type=F32)
    acc = acc * lax.rsqrt(jnp.mean(x * x, axis=-1, keepdims=True) + EPS)
    if mode == "plain":
        o_ref[...] = acc.astype(o_ref.dtype)
    elif mode == "sigmoid":
        o_ref[...] = jax.nn.sigmoid(acc).astype(o_ref.dtype)
    else:
        n = acc.shape[1] // HEAD_DIM
        for hh in range(n):
            blk = acc[:, hh * HEAD_DIM:(hh + 1) * HEAD_DIM]
            ms = jnp.mean(blk * blk, axis=-1, keepdims=True)
            o_ref[:, hh * HEAD_DIM:(hh + 1) * HEAD_DIM] = (blk * lax.rsqrt(ms + EPS) * g_ref[...]).astype(o_ref.dtype)


def _proj(x2, norm_g, w, col0, n, mode, out_dtype, gain=None, tm=PROJ_TM, tn=PROJ_TN, name="proj"):
    t, k = x2.shape
    assert col0 % tn == 0 and n % tn == 0
    j0 = col0 // tn
    if gain is None:
        gain = jnp.ones((HEAD_DIM,), F32)
    return pl.pallas_call(
        functools.partial(_proj_body, mode=mode),
        grid=(n // tn, t // tm),
        in_specs=[pl.BlockSpec((tm, k), lambda j, i: (i, 0)),
                  pl.BlockSpec((k, 1), lambda j, i: (0, 0)),
                  pl.BlockSpec((k, tn), lambda j, i: (0, j0 + j)),
                  pl.BlockSpec((1, HEAD_DIM), lambda j, i: (0, 0))],
        out_specs=pl.BlockSpec((tm, tn), lambda j, i: (i, j)),
        out_shape=jax.ShapeDtypeStruct((t, n), out_dtype),
        scratch_shapes=[pltpu.VMEM((k, tn), BF16)],
        compiler_params=_cparams(("arbitrary", "arbitrary")),
        name=name,
    )(x2, norm_g.reshape(k, 1).astype(F32), w, gain.reshape(1, HEAD_DIM).astype(F32))


def _ssm_body(u_ref, are_ref, aim_ref, wbre_ref, wbim_ref, wc_ref, d_ref, wglu_ref,
              o_ref, tm_ref, bure_ref, buim_ref, sre_ref, sim_ref, *, tc, lane_chunk):
    nb = u_ref.shape[0]
    ds = u_ref.shape[2]
    rows = nb * tc
    nslab = ds // LANES
    nblk = wbre_ref.shape[0]
    cb = wbre_ref.shape[1]
    sb = wbre_ref.shape[2]
    n_state = nblk * sb

    @pl.when(pl.program_id(0) == 0)
    def _():
        sre_ref[...] = jnp.zeros_like(sre_ref)
        sim_ref[...] = jnp.zeros_like(sim_ref)

    for b in range(nb):
        for j in range(nslab):
            tm_ref[j, pl.ds(b, tc, stride=nb), :] = u_ref[b, :, j * LANES:(j + 1) * LANES]
    u_tm = jnp.concatenate([tm_ref[j] for j in range(nslab)], axis=1).astype(BF16)
    for gb in range(nblk):
        lhs = u_tm[:, gb * cb:(gb + 1) * cb]
        bure_ref[:, gb * sb:(gb + 1) * sb] = jnp.dot(lhs, wbre_ref[gb], preferred_element_type=F32)
        buim_ref[:, gb * sb:(gb + 1) * sb] = jnp.dot(lhs, wbim_ref[gb], preferred_element_type=F32)

    for lc in range(n_state // lane_chunk):
        sl = slice(lc * lane_chunk, (lc + 1) * lane_chunk)
        ar = jnp.broadcast_to(are_ref[:, sl], (nb, lane_chunk))
        ai = jnp.broadcast_to(aim_ref[:, sl], (nb, lane_chunk))

        def steps(tt, carry, sl=sl, ar=ar, ai=ai):
            xr, xi = carry
            for k in range(SCAN_UNROLL):
                r0 = pl.multiple_of((tt * SCAN_UNROLL + k) * nb, nb)
                br = bure_ref[pl.ds(r0, nb), sl]
                bi = buim_ref[pl.ds(r0, nb), sl]
                xr, xi = ar * xr - ai * xi + br, ar * xi + ai * xr + bi
                bure_ref[pl.ds(r0, nb), sl] = xr
                buim_ref[pl.ds(r0, nb), sl] = xi
            return xr, xi

        xr, xi = lax.fori_loop(0, tc // SCAN_UNROLL, steps, (sre_ref[:, sl], sim_ref[:, sl]))
        sre_ref[:, sl] = xr
        sim_ref[:, sl] = xi

    for gb in range(nblk):
        xs = jnp.concatenate([bure_ref[:, gb * sb:(gb + 1) * sb], buim_ref[:, gb * sb:(gb + 1) * sb]], axis=1)
        y_blk = jnp.dot(xs.astype(BF16), wc_ref[gb], preferred_element_type=F32)
        for jj in range(cb // LANES):
            tm_ref[gb * (cb // LANES) + jj] = y_blk[:, jj * LANES:(jj + 1) * LANES]
    y_bm = jnp.concatenate(
        [jnp.concatenate([tm_ref[j, pl.ds(b, tc, stride=nb), :] for j in range(nslab)], axis=1) for b in range(nb)],
        axis=0)
    y = y_bm + d_ref[...] * u_ref[...].reshape(rows, ds)
    z = jax.nn.gelu(y)
    gate = jax.nn.sigmoid(jnp.dot(z.astype(BF16), wglu_ref[...], preferred_element_type=F32))
    o_ref[...] = (z * gate).reshape(nb, tc, ds).astype(o_ref.dtype)


def _ssm(u, a_re, a_im, wb_re, wb_im, wc, d_skip, w_glu, tc=SSM_TC, lane_chunk=SSM_LANE_CHUNK):
    nb, seq, ds = u.shape
    rows = nb * tc
    n_state = a_re.shape[1]
    assert tc % SCAN_UNROLL == 0
    const = lambda *shape: pl.BlockSpec(shape, lambda c: (0,) * len(shape))
    return pl.pallas_call(
        functools.partial(_ssm_body, tc=tc, lane_chunk=lane_chunk),
        grid=(seq // tc,),
        in_specs=[pl.BlockSpec((nb, tc, ds), lambda c: (0, c, 0)),
                  const(1, n_state), const(1, n_state),
                  const(*wb_re.shape), const(*wb_im.shape), const(*wc.shape),
                  const(1, ds), const(ds, ds)],
        out_specs=pl.BlockSpec((nb, tc, ds), lambda c: (0, c, 0)),
        out_shape=jax.ShapeDtypeStruct((nb, seq, ds), BF16),
        scratch_shapes=[pltpu.VMEM((ds // LANES, rows, LANES), F32),
                        pltpu.VMEM((rows, n_state), F32), pltpu.VMEM((rows, n_state), F32),
                        pltpu.VMEM((nb, n_state), F32), pltpu.VMEM((nb, n_state), F32)],
        compiler_params=_cparams(("arbitrary",)),
        name="s5_mixer",
    )(u, a_re, a_im, wb_re, wb_im, wc, d_skip.reshape(1, ds), w_glu)


def _ssm_params(lambda_re, lambda_im, log_dt, b_re, b_im, c_re, c_im):
    g, p = lambda_re.shape
    h = b_re.shape[2]
    nblk = g // SSM_GROUPS_PER_BLOCK
    dt = jnp.exp(log_dt.astype(F32))[:, None]
    lr = lambda_re.astype(F32)
    li = lambda_im.astype(F32)
    mag = jnp.exp(lr * dt)
    abar_re, abar_im = mag * jnp.cos(li * dt), mag * jnp.sin(li * dt)
    nr, ni = abar_re - 1.0, abar_im
    den = lr * lr + li * li
    coef_re = (nr * lr + ni * li) / den
    coef_im = (ni * lr - nr * li) / den
    bbar_re = coef_re[..., None] * b_re - coef_im[..., None] * b_im
    bbar_im = coef_re[..., None] * b_im + coef_im[..., None] * b_re
    gpb = SSM_GROUPS_PER_BLOCK

    def block_diag(m, rows_per_group, cols_per_group):
        r = jnp.arange(gpb * rows_per_group)[:, None] // rows_per_group
        c = jnp.arange(gpb * cols_per_group)[None, :] // cols_per_group
        return jnp.where(r == c, jnp.tile(m, (1, gpb, 1)), 0.0)

    def pack_b(bb):
        m = bb.reshape(nblk, gpb, p, h).transpose(0, 3, 1, 2).reshape(nblk, h, gpb * p)
        return block_diag(m, h, p)

    def pack_c(cc):
        m = cc.reshape(nblk, gpb, h, p).transpose(0, 3, 1, 2).reshape(nblk, p, gpb * h)
        return block_diag(m, p, h)

    wc = jnp.concatenate([pack_c(c_re.astype(F32)), -pack_c(c_im.astype(F32))], axis=1)
    return (abar_re.reshape(1, g * p), abar_im.reshape(1, g * p),
            pack_b(bbar_re).astype(BF16), pack_b(bbar_im).astype(BF16), wc.astype(BF16))


def _attn_body(q_ref, k_ref, v_ref, tri_ref, o_ref, acc_ref, rs_ref, *, tq, nhs):
    qi = pl.program_id(2)
    tri = tri_ref[...]
    row = lax.broadcasted_iota(I32, (nhs * tq, tq), 0) % tq
    col = lax.broadcasted_iota(I32, (nhs * tq, tq), 1)
    causal = col < row

    def sweep(j, masked):
        k0 = pl.multiple_of(j * tq, tq)
        heads = [slice(h * HEAD_DIM, (h + 1) * HEAD_DIM) for h in range(nhs)]
        z = jnp.concatenate(
            [lax.dot_general(q_ref[0, :, hs], k_ref[0, pl.ds(k0, tq), hs], (((1,), (1,)), ((), ())),
                             preferred_element_type=F32) for hs in heads], axis=0)
        sp = jnp.where(z > SOFTPLUS2_LINEAR_ABOVE, z, jnp.log(1.0 + jnp.exp2(z)) * LOG2_E)
        spm = jnp.where(causal, sp, 0.0) if masked else sp
        spb = spm.astype(BF16)
        tail = jnp.dot(spb, tri, preferred_element_type=F32)
        rsum = rs_ref[...]
        w = jnp.exp2(z - (sp + tail + jnp.concatenate([rsum] * (tq // LANES), axis=1)))
        if masked:
            w = jnp.where(causal, w, 0.0)
        wb = w.astype(BF16)
        for h, hs in enumerate(heads):
            acc_ref[h] += jnp.dot(wb[h * tq:(h + 1) * tq], v_ref[0, pl.ds(k0, tq), hs], preferred_element_type=F32)
        total = tail[:, 0:1] + spb[:, 0:1].astype(F32)
        rsum = rsum + jnp.broadcast_to(total, rsum.shape)
        rs_ref[...] = rsum
        return (jnp.min(rsum) < -EXP_ZERO_BELOW * LOG2_E).astype(I32)

    acc_ref[...] = jnp.zeros_like(acc_ref)
    rs_ref[...] = jnp.zeros_like(rs_ref)
    live = sweep(qi, True)

    def cond(c):
        return (c[0] >= 0) & (c[1] > 0)

    def body(c):
        return c[0] - 1, sweep(c[0], False)

    lax.while_loop(cond, body, (qi - 1, live))
    for h in range(nhs):
        o_ref[0, :, h * HEAD_DIM:(h + 1) * HEAD_DIM] = acc_ref[h].astype(o_ref.dtype)


def _attention(q, k, v, tq=ATTN_TQ, nhs=ATTN_HEADS_PER_STEP):
    nb, seq, da = q.shape
    nh = da // HEAD_DIM
    wd = nhs * HEAD_DIM
    r = jnp.arange(tq)
    tri = (r[:, None] > r[None, :]).astype(BF16)
    return pl.pallas_call(
        functools.partial(_attn_body, tq=tq, nhs=nhs),
        grid=(nb, nh // nhs, seq // tq),
        in_specs=[pl.BlockSpec((1, tq, wd), lambda b, h, i: (b, i, h)),
                  pl.BlockSpec((1, seq, wd), lambda b, h, i: (b, 0, h)),
                  pl.BlockSpec((1, seq, wd), lambda b, h, i: (b, 0, h)),
                  pl.BlockSpec((tq, tq), lambda b, h, i: (0, 0))],
        out_specs=pl.BlockSpec((1, tq, wd), lambda b, h, i: (b, i, h)),
        out_shape=jax.ShapeDtypeStruct((nb, seq, da), BF16),
        scratch_shapes=[pltpu.VMEM((nhs, tq, HEAD_DIM), F32), pltpu.VMEM((nhs * tq, LANES), F32)],
        compiler_params=_cparams(("arbitrary", "arbitrary", "arbitrary")),
        name="stick_attention",
    )(q, k, v, tri)


def _merge_body(ys_ref, ya_ref, gs_ref, ga_ref, ws_ref, wa_ref, o_ref, wsb_ref, wab_ref):
    @pl.when(pl.program_id(1) == 0)
    def _():
        wsb_ref[...] = ws_ref[...].astype(BF16)
        wab_ref[...] = wa_ref[...].astype(BF16)

    ps = jnp.dot(ys_ref[...], wsb_ref[...], preferred_element_type=F32)
    pa = jnp.dot(ya_ref[...], wab_ref[...], preferred_element_type=F32)
    o_ref[...] = (gs_ref[...].astype(F32) * ps + ga_ref[...].astype(F32) * pa).astype(o_ref.dtype)


def _merge(y_ssm, y_att, gates, w_s, w_a, tm=MERGE_TM, tn=MERGE_TN):
    t, ks = y_ssm.shape
    d = w_s.shape[1]
    nj = d // tn
    return pl.pallas_call(
        _merge_body,
        grid=(nj, t // tm),
        in_specs=[pl.BlockSpec((tm, ks), lambda j, i: (i, 0)),
                  pl.BlockSpec((tm, y_att.shape[1]), lambda j, i: (i, 0)),
                  pl.BlockSpec((tm, tn), lambda j, i: (i, j)),
                  pl.BlockSpec((tm, tn), lambda j, i: (i, nj + j)),
                  pl.BlockSpec((ks, tn), lambda j, i: (0, j)),
                  pl.BlockSpec((w_a.shape[0], tn), lambda j, i: (0, j))],
        out_specs=pl.BlockSpec((tm, tn), lambda j, i: (i, j)),
        out_shape=jax.ShapeDtypeStruct((t, d), BF16),
        scratch_shapes=[pltpu.VMEM((ks, tn), BF16), pltpu.VMEM((w_a.shape[0], tn), BF16)],
        compiler_params=_cparams(("arbitrary", "arbitrary")),
        name="branch_merge",
    )(y_ssm, y_att, gates, gates, w_s, w_a)


def _lane_min_index(mask, lane):
    return jnp.min(jnp.where(mask, lane, float(ROUTE_LANES)), axis=1, keepdims=True)


def _outproj_body(x_ref, m_ref, w_ref, g_ref, wrc_ref, br_ref,
                  x1_ref, rr_ref, rt_ref):
    x1 = x_ref[...] + jnp.dot(m_ref[...], w_ref[...], preferred_element_type=F32)
    x1_ref[...] = x1
    ms = jnp.mean(x1 * x1, axis=-1, keepdims=True)
    h2 = x1 * lax.rsqrt(ms + EPS) * g_ref[...]
    hb = h2.astype(BF16)

    hl = (h2 - hb.astype(F32)).astype(BF16)
    p = jnp.dot(hb, wrc_ref[...], preferred_element_type=F32)
    logits = (p[:, :ROUTE_LANES] + p[:, ROUTE_LANES:]
              + jnp.dot(hl, wrc_ref[:, :ROUTE_LANES], preferred_element_type=F32)) + br_ref[...]
    tm = logits.shape[0]
    lane = lax.broadcasted_iota(I32, (tm, ROUTE_LANES), 1).astype(F32)
    is_group = (lane >= N_EXPERTS) & (lane < N_EXPERTS + N_EXPERT_GROUPS)
    gl = jnp.where(is_group, logits, NEG_BIG)
    gmax = jnp.max(gl, axis=1, keepdims=True)
    gidx = _lane_min_index(gl == gmax, lane) - N_EXPERTS
    g_top = 1.0 / jnp.sum(jnp.exp(gl - gmax), axis=1, keepdims=True)
    lo = gidx * EXPERTS_PER_GROUP
    in_grp = (lane >= lo) & (lane < lo + EXPERTS_PER_GROUP)
    el = jnp.where(in_grp, logits, NEG_BIG)
    m1 = jnp.max(el, axis=1, keepdims=True)
    i1 = _lane_min_index(el == m1, lane)
    el2 = jnp.where(lane == i1, NEG_BIG, el)
    m2 = jnp.max(el2, axis=1, keepdims=True)
    i2 = _lane_min_index(el2 == m2, lane)
    dlt = jnp.exp(m2 - m1)
    w1 = 1.0 / (1.0 + dlt)
    w2 = dlt * w1
    route = jnp.where(lane == 0, i1,
                      jnp.where(lane == 1, i2,
                                jnp.where(lane == 2, g_top * w1,
                                          jnp.where(lane == 3, g_top * w2, 0.0))))
    rr_ref[...] = route
    rt_ref[...] = jnp.transpose(route)[0:8, :]


def _outproj(x2, merged, w_out, g, wr, br, tm=OUTPROJ_TM):
    t, d = x2.shape
    wr_hi = wr.astype(BF16)
    wr_cat = jnp.concatenate([wr_hi, (wr - wr_hi.astype(F32)).astype(BF16)], axis=1)
    const = lambda *shape: pl.BlockSpec(shape, lambda i: (0,) * len(shape))
    return pl.pallas_call(
        _outproj_body,
        grid=(t // tm,),
        in_specs=[pl.BlockSpec((tm, d), lambda i: (i, 0)), pl.BlockSpec((tm, d), lambda i: (i, 0)),
                  const(d, d), const(1, d), const(d, 2 * ROUTE_LANES), const(1, ROUTE_LANES)],
        out_specs=[pl.BlockSpec((tm, d), lambda i: (i, 0)),
                   pl.BlockSpec((tm, ROUTE_LANES), lambda i: (i, 0)),
                   pl.BlockSpec((8, tm), lambda i: (0, i))],
        out_shape=[jax.ShapeDtypeStruct((t, d), F32),
                   jax.ShapeDtypeStruct((t, ROUTE_LANES), F32),
                   jax.ShapeDtypeStruct((8, t), F32)],
        compiler_params=_cparams(("arbitrary",)),
        name="outproj_router",
    )(x2, merged, w_out, g.reshape(1, d), wr_cat, br)


def _count_body(ids_ref, cnt_ref):
    nc, _, c = ids_ref.shape
    eidx = lax.broadcasted_iota(I32, (N_EXPERTS, c), 0)

    def step(i, acc):
        return acc + jnp.where(ids_ref[i] == eidx, 1.0, 0.0)

    acc = lax.fori_loop(0, nc, step, jnp.zeros((N_EXPERTS, c), F32))
    tot = jnp.sum(acc, axis=1, keepdims=True)
    cnt_ref[...] = jnp.broadcast_to(tot, cnt_ref.shape).astype(I32)


def _expert_counts(ids3):
    return pl.pallas_call(
        _count_body,
        out_shape=jax.ShapeDtypeStruct((N_EXPERTS, LANES), I32),
        name="expert_counts",
    )(ids3)


def _pos_body(ids_ref, start_ref, incl_ref, pos_ref):
    nc, _, c = ids_ref.shape
    eidx = lax.broadcasted_iota(I32, (N_EXPERTS, c), 0)
    incl = incl_ref[...]

    def step(i, carry):
        onehot = ids_ref[i] == eidx
        cum = jnp.dot(jnp.where(onehot, 1.0, 0.0).astype(BF16), incl, preferred_element_type=F32)
        val = jnp.where(onehot, cum - 1.0 + carry, 0.0)
        pos_ref[i] = jnp.sum(val, axis=0, keepdims=True).astype(I32)
        return carry + cum[:, c - 1:c]

    lax.fori_loop(0, nc, step, start_ref[:, 0:1].astype(F32))


def _sorted_positions(ids3, starts):
    nc, _, c = ids3.shape
    r = jnp.arange(c)
    incl = (r[:, None] <= r[None, :]).astype(BF16)
    start_b = jnp.broadcast_to(starts.astype(I32)[:, None], (N_EXPERTS, LANES))
    return pl.pallas_call(
        _pos_body,
        out_shape=jax.ShapeDtypeStruct((nc, 1, c), I32),
        name="sorted_positions",
    )(ids3, start_b, incl)


def _dispatch_body(pos_ref, g_ref, src_ref, dst_ref, buf, lsem, ssem, *, n_tok, tm):
    i = pl.program_id(0)
    nsteps = pl.num_programs(0)
    base = i * tm

    def load(tile):
        b = tile % DISPATCH_BUFS
        return pltpu.make_async_copy(src_ref.at[pl.ds(tile * tm, tm), :], buf.at[b], lsem.at[b])

    def drain(tile):
        b = tile % DISPATCH_BUFS
        for s in range(2):
            pltpu.make_async_copy(buf.at[b], dst_ref.at[pl.ds(0, tm), :], ssem.at[2 * b + s]).wait()

    @pl.when(i == 0)
    def _():
        load(0).start()

        @pl.when(nsteps > 1)
        def _():
            load(1).start()

    load(i).wait()
    b = i % DISPATCH_BUFS
    xt = buf[b]
    buf[b] = xt * lax.rsqrt(jnp.mean(xt * xt, axis=-1, keepdims=True) + EPS) * g_ref[...]

    def issue(ii, carry):
        for k in range(DMA_UNROLL):
            j = ii * DMA_UNROLL + k
            row = buf.at[b, pl.ds(j, 1), :]
            pltpu.make_async_copy(row, dst_ref.at[pl.ds(pos_ref[base + j], 1), :], ssem.at[2 * b]).start()
            pltpu.make_async_copy(row, dst_ref.at[pl.ds(pos_ref[n_tok + base + j], 1), :], ssem.at[2 * b + 1]).start()
        return carry

    lax.fori_loop(0, tm // DMA_UNROLL, issue, 0)

    @pl.when(i >= 1)
    def _():
        drain(i - 1)

    @pl.when(i + 2 < nsteps)
    def _():
        load(i + 2).start()

    @pl.when(i == nsteps - 1)
    def _():
        drain(i)


def _dispatch(pos_flat, x1, norm_g, tm=MOE_TM):
    n = pos_flat.shape[0]
    t, d = x1.shape
    assert t % tm == 0 and tm % DMA_UNROLL == 0
    return pl.pallas_call(
        functools.partial(_dispatch_body, n_tok=t, tm=tm),
        grid_spec=pltpu.PrefetchScalarGridSpec(
            num_scalar_prefetch=1, grid=(t // tm,),
            in_specs=[pl.BlockSpec((1, d), lambda i, p: (0, 0)), pl.BlockSpec(memory_space=pl.ANY)],
            out_specs=pl.BlockSpec(memory_space=pl.ANY),
            scratch_shapes=[pltpu.VMEM((DISPATCH_BUFS, tm, d), x1.dtype),
                            pltpu.SemaphoreType.DMA((DISPATCH_BUFS,)),
                            pltpu.SemaphoreType.DMA((2 * DISPATCH_BUFS,))]),
        out_shape=jax.ShapeDtypeStruct((n, d), x1.dtype),
        compiler_params=_cparams(("arbitrary",)),
        name="dispatch",
    )(pos_flat, norm_g.reshape(1, d).astype(F32), x1)


def _gmm_body(tile_ref, exp_ref, nxt_ref, lo_ref, hi_ref, x_ref, wg_ref, wu_ref, wd_ref, o_ref,
              sg, su, sd, wgs, wus, wds, sem):
    w = pl.program_id(0)
    prev = jnp.maximum(w - 1, 0)
    new_expert = (w == 0) | (exp_ref[w] != exp_ref[prev])
    first_of_tile = (w == 0) | (tile_ref[w] != tile_ref[prev])
    lo = lo_ref[w]
    hi = hi_ref[w]

    def stage(e):
        return (pltpu.make_async_copy(wg_ref.at[e], sg, sem.at[0]),
                pltpu.make_async_copy(wu_ref.at[e], su, sem.at[1]),
                pltpu.make_async_copy(wd_ref.at[e], sd, sem.at[2]))

    @pl.when(w == 0)
    def _():
        for cp in stage(exp_ref[0]):
            cp.start()

    @pl.when(new_expert)
    def _():
        for cp in stage(exp_ref[w]):
            cp.wait()
        wgs[...] = sg[...].astype(BF16)
        wus[...] = su[...].astype(BF16)
        wds[...] = sd[...].astype(BF16)

        @pl.when(nxt_ref[w] >= 0)
        def _():
            for cp in stage(nxt_ref[w]):
                cp.start()

    @pl.when(hi > lo)
    def _():
        xx = x_ref[...].astype(BF16)
        g = jnp.dot(xx, wgs[...], preferred_element_type=F32)
        u = jnp.dot(xx, wus[...], preferred_element_type=F32)
        hid = (g * jax.nn.sigmoid(g)) * u
        y = jnp.dot(hid.astype(BF16), wds[...], preferred_element_type=F32)
        rows = lax.broadcasted_iota(I32, (y.shape[0], 1), 0)
        y = jnp.where((rows >= lo) & (rows < hi), y, 0.0)

        @pl.when(first_of_tile)
        def _():
            o_ref[...] = y

        @pl.when(jnp.logical_not(first_of_tile))
        def _():
            o_ref[...] += y


def _gmm(meta, xs, w_gate, w_up, w_down, tm):
    tile_id, expert_id, next_expert, row_lo, row_hi = meta
    n, _ = xs.shape
    e, d, f = w_gate.shape
    nw = tile_id.shape[0]
    hbm = pl.BlockSpec(memory_space=pl.ANY)
    row_tile = lambda w, ti, *_: (ti[w], 0)
    return pl.pallas_call(
        _gmm_body,
        grid_spec=pltpu.PrefetchScalarGridSpec(
            num_scalar_prefetch=5, grid=(nw,),
            in_specs=[pl.BlockSpec((tm, d), row_tile), hbm, hbm, hbm],
            out_specs=pl.BlockSpec((tm, d), row_tile),
            scratch_shapes=[pltpu.VMEM((d, f), F32), pltpu.VMEM((d, f), F32), pltpu.VMEM((f, d), F32),
                            pltpu.VMEM((d, f), BF16), pltpu.VMEM((d, f), BF16), pltpu.VMEM((f, d), BF16),
                            pltpu.SemaphoreType.DMA((3,))]),
        out_shape=jax.ShapeDtypeStruct((n, d), F32),
        compiler_params=_cparams(("arbitrary",)),
        name="expert_gmm",
    )(tile_id, expert_id, next_expert, row_lo, row_hi, xs, w_gate, w_up, w_down)


def _gmm_metadata(counts, n_rows, tm):
    nt = n_rows // tm
    nw = nt + N_EXPERTS - 1
    ends = jnp.cumsum(counts)
    starts = ends - counts
    first_tile = starts // tm
    n_items = jnp.where(counts > 0, (ends - 1) // tm - first_tile + 1, 0)
    item_end = jnp.cumsum(n_items)
    item_start = item_end - n_items
    total = item_end[-1]
    w = jnp.arange(nw, dtype=I32)
    wc = jnp.minimum(w, total - 1)
    ex = jnp.sum((item_end[None, :] <= wc[:, None]).astype(I32), axis=1)
    tile = first_tile[ex] + (wc - item_start[ex])
    lo = jnp.maximum(starts[ex], tile * tm) - tile * tm
    hi = jnp.minimum(ends[ex], (tile + 1) * tm) - tile * tm
    valid = w < total
    lo = jnp.where(valid, lo, 0)
    hi = jnp.where(valid, hi, 0)
    run_end = item_end[ex]
    nxt = jnp.where(run_end < total, ex[jnp.minimum(run_end, nw - 1)], -1)
    return tile.astype(I32), ex, nxt.astype(I32), lo.astype(I32), hi.astype(I32)


def _combine_body(pos_ref, x1_ref, rr_ref, ys_ref, o_ref, a0, a1, b0, b1, sem, *, n_tok):
    i = pl.program_id(0)
    nsteps = pl.num_programs(0)
    tm = x1_ref.shape[0]

    def gather(tile, bufs, sems):
        base = tile * tm

        def issue(jj, carry):
            for k in range(DMA_UNROLL):
                j = jj * DMA_UNROLL + k
                pltpu.make_async_copy(ys_ref.at[pl.ds(pos_ref[base + j], 1), :],
                                      bufs[0].at[pl.ds(j, 1), :], sem.at[sems[0]]).start()
                pltpu.make_async_copy(ys_ref.at[pl.ds(pos_ref[n_tok + base + j], 1), :],
                                      bufs[1].at[pl.ds(j, 1), :], sem.at[sems[1]]).start()
            return carry

        lax.fori_loop(0, tm // DMA_UNROLL, issue, 0)

    def step(cur, cur_sems, nxt, nxt_sems):
        @pl.when(i == 0)
        def _():
            gather(i, cur, cur_sems)

        @pl.when(i + 1 < nsteps)
        def _():
            gather(i + 1, nxt, nxt_sems)

        for buf, s in zip(cur, cur_sems):
            pltpu.make_async_copy(ys_ref.at[pl.ds(0, tm), :], buf, sem.at[s]).wait()
        rr = rr_ref[...]
        o_ref[...] = x1_ref[...] + rr[:, 2:3] * cur[0][...] + rr[:, 3:4] * cur[1][...]

    @pl.when(i % 2 == 0)
    def _():
        step((a0, a1), (0, 1), (b0, b1), (2, 3))

    @pl.when(i % 2 == 1)
    def _():
        step((b0, b1), (2, 3), (a0, a1), (0, 1))


def _combine(pos_flat, x1, route_rows, ys, tm=MOE_TM):
    t, d = x1.shape
    assert tm % DMA_UNROLL == 0
    buf = pltpu.VMEM((tm, d), F32)
    return pl.pallas_call(
        functools.partial(_combine_body, n_tok=t),
        grid_spec=pltpu.PrefetchScalarGridSpec(
            num_scalar_prefetch=1, grid=(t // tm,),
            in_specs=[pl.BlockSpec((tm, d), lambda i, p: (i, 0)),
                      pl.BlockSpec((tm, ROUTE_LANES), lambda i, p: (i, 0)),
                      pl.BlockSpec(memory_space=pl.ANY)],
            out_specs=pl.BlockSpec((tm, d), lambda i, p: (i, 0)),
            scratch_shapes=[buf, buf, buf, buf, pltpu.SemaphoreType.DMA((4,))]),
        out_shape=jax.ShapeDtypeStruct((t, d), F32),
        compiler_params=_cparams(("arbitrary",)),
        name="combine",
    )(pos_flat, x1, route_rows, ys)


def _moe(x1, norm_g, route_rows, route_t, w_gate, w_up, w_down, gmm_tm=MOE_TM, sort_chunk=SORT_CHUNK):
    t = x1.shape[0]
    n = 2 * t
    ids3 = route_t[0:2].astype(I32).reshape(n // sort_chunk, 1, sort_chunk)
    counts = _expert_counts(ids3)[:, 0]
    starts = jnp.cumsum(counts) - counts
    pos_flat = _sorted_positions(ids3, starts).reshape(n)
    xs = _dispatch(pos_flat, x1, norm_g)
    ys = _gmm(_gmm_metadata(counts, n, gmm_tm), xs, w_gate, w_up, w_down, gmm_tm)
    return _combine(pos_flat, x1, route_rows, ys)


def _layer(x, attn_norm_g, w_in, lambda_re, lambda_im, log_dt, ssm_b_re, ssm_b_im, ssm_c_re, ssm_c_im,
           ssm_d, w_glu, q_norm_g, k_norm_g, w_branch_ssm, w_branch_att, w_out, ffn_norm_g,
           router_group_w, router_group_b, router_expert_w, router_expert_b,
           expert_w_gate, expert_w_up, expert_w_down):
    nb, seq, d = x.shape
    t = nb * seq
    ds = w_glu.shape[0]
    da = N_HEADS * HEAD_DIM
    x2 = x.reshape(t, d)

    o = 0
    u = _proj(x2, attn_norm_g, w_in, o, ds, "plain", F32, name="proj_u"); o += ds
    qg = q_norm_g.astype(F32) * (HEAD_DIM ** -0.5 * LOG2_E)
    q = _proj(x2, attn_norm_g, w_in, o, da, "headnorm", BF16, gain=qg, name="proj_q"); o += da
    k = _proj(x2, attn_norm_g, w_in, o, da, "headnorm", BF16, gain=k_norm_g, name="proj_k"); o += da
    v = _proj(x2, attn_norm_g, w_in, o, da, "plain", BF16, name="proj_v"); o += da
    gates = _proj(x2, attn_norm_g, w_in, o, w_in.shape[1] - o, "sigmoid", BF16, name="proj_gates")

    a_re, a_im, wb_re, wb_im, wc = _ssm_params(lambda_re, lambda_im, log_dt, ssm_b_re, ssm_b_im, ssm_c_re, ssm_c_im)
    y_ssm = _ssm(u.reshape(nb, seq, ds), a_re, a_im, wb_re, wb_im, wc, ssm_d.astype(F32), w_glu.astype(BF16))
    y_att = _attention(q.reshape(nb, seq, da), k.reshape(nb, seq, da), v.reshape(nb, seq, da))

    merged = _merge(y_ssm.reshape(t, ds), y_att.reshape(t, da), gates, w_branch_ssm, w_branch_att)

    n_pad = ROUTE_LANES - N_EXPERTS - N_EXPERT_GROUPS
    wr = jnp.concatenate([router_expert_w.astype(F32), router_group_w.astype(F32), jnp.zeros((d, n_pad), F32)], axis=1)
    br = jnp.concatenate([router_expert_b.astype(F32), router_group_b.astype(F32), jnp.zeros((n_pad,), F32)])[None, :]
    x1, route_rows, route_t = _outproj(x2, merged, w_out.astype(BF16), ffn_norm_g, wr, br)

    out = _moe(x1, ffn_norm_g, route_rows, route_t, expert_w_gate, expert_w_up, expert_w_down)
    return out.reshape(nb, seq, d)


def kernel(x, attn_norm_g, w_in, lambda_re, lambda_im, log_dt, ssm_b_re, ssm_b_im, ssm_c_re, ssm_c_im, ssm_d, w_glu, q_norm_g, k_norm_g, w_branch_ssm, w_branch_att, w_out, ffn_norm_g, router_group_w, router_group_b, router_expert_w, router_expert_b, expert_w_gate, expert_w_up, expert_w_down):
    depth = attn_norm_g.shape[0]
    for l in range(depth):
        x = _layer(x, attn_norm_g[l], w_in[l], lambda_re[l], lambda_im[l], log_dt[l], ssm_b_re[l], ssm_b_im[l],
                   ssm_c_re[l], ssm_c_im[l], ssm_d[l], w_glu[l], q_norm_g[l], k_norm_g[l], w_branch_ssm[l],
                   w_branch_att[l], w_out[l], ffn_norm_g[l], router_group_w[l], router_group_b[l],
                   router_expert_w[l], router_expert_b[l], expert_w_gate[l], expert_w_up[l], expert_w_down[l])
    return x
```
